```python
import jax, jax.numpy as jnp
from jax import lax
import numpy as np

D_MODEL = 1024
BATCH = 2
SEQ = 8192
DEPTH = 2

GRID_W = 64
NA_HEADS = 16
NA_HEAD_DIM = D_MODEL // NA_HEADS
NA_WIN_H = 8
NA_WIN_W = 16
MLA_HEADS = 16
MLA_Q_LORA = D_MODEL // 4
MLA_KV_LORA = D_MODEL // 8
MLA_NOPE = 64
MLA_ROPE = 32
MLA_V = 64
ROPE_THETA = 10000.0
Q_BLOCK = 128
FFN_DIM = 2816
N_EXPERTS = 8
TOP_K = 2
EXPERT_DIM = 1792
NORM_EPS = 1e-6

kernel_name = "hybrid_na_mla_moe_adaln_encoder"


def _rmsnorm(x):
    xf = x.astype(jnp.float32)
    y = xf * lax.rsqrt(jnp.mean(xf * xf, axis=-1, keepdims=True) + NORM_EPS)
    return y.astype(x.dtype)


def _modulate(x, shift, scale):
    return _rmsnorm(x) * (1 + scale[:, None, :]) + shift[:, None, :]


def _swiglu(h, w_gu, w_down):
    gate, up = jnp.split(h @ w_gu, 2, axis=-1)
    return (jax.nn.silu(gate) * up) @ w_down


def _rope_2d_tables(seq_len):
    t = jnp.arange(seq_len)
    row = (t // GRID_W).astype(jnp.float32)
    col = (t % GRID_W).astype(jnp.float32)
    n = MLA_ROPE // 4
    inv = ROPE_THETA ** (-jnp.arange(n, dtype=jnp.float32) / n)
    ang = jnp.concatenate([row[:, None] * inv, col[:, None] * inv], axis=-1)
    return jnp.cos(ang), jnp.sin(ang)


def _apply_rope(x, cos, sin):
    x1 = x[..., 0::2].astype(jnp.float32)
    x2 = x[..., 1::2].astype(jnp.float32)
    out = jnp.stack([x1 * cos - x2 * sin, x1 * sin + x2 * cos], axis=-1)
    return out.reshape(x.shape).astype(x.dtype)


def _neighborhood_attention(h, w_qkv, rpb, w_o):
    B, S, D = h.shape
    rows = S // GRID_W
    kh = min(NA_WIN_H, rows)
    kw = min(NA_WIN_W, GRID_W)
    qkv = (h @ w_qkv).reshape(B, rows, GRID_W, 3, NA_HEADS, NA_HEAD_DIM)
    q = qkv[:, :, :, 0] * (NA_HEAD_DIM ** -0.5)
    k = qkv[:, :, :, 1]
    v = qkv[:, :, :, 2]
    cols = jnp.arange(GRID_W)
    col_idx = jnp.clip(cols - kw // 2, 0, GRID_W - kw)[:, None] + jnp.arange(kw)[None, :]
    dc_idx = col_idx - cols[:, None] + NA_WIN_W - 1

    def row_block(r):
        rs = jnp.clip(r - kh // 2, 0, rows - kh)
        q_r = lax.dynamic_index_in_dim(q, r, axis=1, keepdims=False)
        k_g = lax.dynamic_slice_in_dim(k, rs, kh, axis=1)[:, :, col_idx]
        v_g = lax.dynamic_slice_in_dim(v, rs, kh, axis=1)[:, :, col_idx]
        dr_idx = rs + jnp.arange(kh) - r + NA_WIN_H - 1
        bias = rpb[:, dr_idx[None, :, None], dc_idx[:, None, :]]
        s = jnp.einsum('bqhd,bkqjhd->bhqkj', q_r, k_g).astype(jnp.float32) + bias.astype(jnp.float32)
        p = jax.nn.softmax(s.reshape(B, NA_HEADS, GRID_W, kh * kw), axis=-1)
        p = p.reshape(s.shape).astype(v.dtype)
        return jnp.einsum('bhqkj,bkqjhd->bqhd', p, v_g)

    o = lax.map(row_block, jnp.arange(rows))
    o = jnp.moveaxis(o, 0, 1).reshape(B, S, D)
    return o @ w_o


def _mla(h, w_down, q_norm, w_uq, kv_norm, w_ukv, w_o):
    B, S, D = h.shape
    down = h @ w_down
    c_q = _rmsnorm(down[..., :MLA_Q_LORA]) * q_norm
    c_kv = _rmsnorm(down[..., MLA_Q_LORA:MLA_Q_LORA + MLA_KV_LORA]) * kv_norm
    k_rope = down[..., MLA_Q_LORA + MLA_KV_LORA:]
    q = (c_q @ w_uq).reshape(B, S, MLA_HEADS, MLA_NOPE + MLA_ROPE)
    kv = (c_kv @ w_ukv).reshape(B, S, MLA_HEADS, MLA_NOPE + MLA_V)
    k_nope, v = kv[..., :MLA_NOPE], kv[..., MLA_NOPE:]
    cos, sin = _rope_2d_tables(S)
    q_nope = q[..., :MLA_NOPE]
    q_rope = _apply_rope(q[..., MLA_NOPE:], cos[:, None, :], sin[:, None, :])
    k_rope = _apply_rope(k_rope, cos, sin)
    scale = (MLA_NOPE + MLA_ROPE) ** -0.5
    nb = S // Q_BLOCK

    def to_blocks(t):
        return jnp.moveaxis(t.reshape(B, nb, Q_BLOCK, *t.shape[2:]), 1, 0)

    def q_block(blk):
        qn, qr = blk
        s = (jnp.einsum('bqhd,bkhd->bhqk', qn, k_nope)
             + jnp.einsum('bqhr,bkr->bhqk', qr, k_rope)).astype(jnp.float32) * scale
        p = jax.nn.softmax(s, axis=-1).astype(v.dtype)
        return jnp.einsum('bhqk,bkhd->bqhd', p, v)

    o = lax.map(q_block, (to_blocks(q_nope), to_blocks(q_rope)))
    o = jnp.moveaxis(o, 0, 1).reshape(B, S, MLA_HEADS * MLA_V)
    return o @ w_o


def _moe(h, w_router, w_gu, w_down):
    B, S, D = h.shape
    hf = h.reshape(B * S, D)
    probs = jax.nn.softmax((hf @ w_router).astype(jnp.float32), axis=-1)
    vals, idx = lax.top_k(probs, TOP_K)
    wts = vals / jnp.sum(vals, axis=-1, keepdims=True)
    combine = jnp.sum(jax.nn.one_hot(idx, N_EXPERTS, dtype=jnp.float32) * wts[..., None], axis=1)
    combine = combine.astype(h.dtype)
    y = jnp.zeros_like(hf)
    for e in range(N_EXPERTS):
        y = y + combine[:, e:e + 1] * _swiglu(hf, w_gu[e], w_down[e])
    return y.reshape(B, S, D)


def setup_inputs(seed: int = 0) -> dict:
    key = jax.random.key(seed)
    ks = iter(jax.random.split(key, 32))
    n_even = (DEPTH + 1) // 2
    n_odd = DEPTH // 2
    D = D_MODEL

    def nrm(shape, fan_in, mult=1.0):
        return jax.random.normal(next(ks), shape, jnp.float32) * (mult * fan_in ** -0.5)

    return {
        "x": jax.random.normal(next(ks), (BATCH, SEQ, D), jnp.float32),
        "c": jax.random.normal(next(ks), (BATCH, D), jnp.float32),
        "w_ada": nrm((DEPTH, D, 6 * D), D, 0.5),
        "b_ada": 0.02 * jax.random.normal(next(ks), (DEPTH, 6 * D), jnp.float32),
        "na_w_qkv": nrm((n_even, D, 3 * D), D),
        "na_rpb": 0.1 * jax.random.normal(next(ks), (n_even, NA_HEADS, 2 * NA_WIN_H - 1, 2 * NA_WIN_W - 1), jnp.float32),
        "na_w_o": nrm((n_even, D, D), D),
        "ffn_w_gu": nrm((n_even, D, 2 * FFN_DIM), D),
        "ffn_w_down": nrm((n_even, FFN_DIM, D), FFN_DIM),
        "mla_w_down": nrm((n_odd, D, MLA_Q_LORA + MLA_KV_LORA + MLA_ROPE), D),
        "mla_q_norm": 1.0 + 0.01 * jax.random.normal(next(ks), (n_odd, MLA_Q_LORA), jnp.float32),
        "mla_w_uq": nrm((n_odd, MLA_Q_LORA, MLA_HEADS * (MLA_NOPE + MLA_ROPE)), MLA_Q_LORA),
        "mla_kv_norm": 1.0 + 0.01 * jax.random.normal(next(ks), (n_odd, MLA_KV_LORA), jnp.float32),
        "mla_w_ukv": nrm((n_odd, MLA_KV_LORA, MLA_HEADS * (MLA_NOPE + MLA_V)), MLA_KV_LORA),
        "mla_w_o": nrm((n_odd, MLA_HEADS * MLA_V, D), MLA_HEADS * MLA_V),
        "moe_w_router": nrm((n_odd, D, N_EXPERTS), D),
        "moe_w_gu": nrm((n_odd, N_EXPERTS, D, 2 * EXPERT_DIM), D),
        "moe_w_down": nrm((n_odd, N_EXPERTS, EXPERT_DIM, D), EXPERT_DIM),
        "final_norm": 1.0 + 0.01 * jax.random.normal(next(ks), (D,), jnp.float32),
    }


def reference(x, c, w_ada, b_ada, na_w_qkv, na_rpb, na_w_o, ffn_w_gu, ffn_w_down,
              mla_w_down, mla_q_norm, mla_w_uq, mla_kv_norm, mla_w_ukv, mla_w_o,
              moe_w_router, moe_w_gu, moe_w_down, final_norm):
    c_act = jax.nn.silu(c)
    for i in range(DEPTH):
        j = i // 2
        mod = c_act @ w_ada[i] + b_ada[i]
        sh_a, sc_a, g_a, sh_f, sc_f, g_f = jnp.split(mod, 6, axis=-1)
        h = _modulate(x, sh_a, sc_a)
        if i % 2 == 0:
            m = _neighborhood_attention(h, na_w_qkv[j], na_rpb[j], na_w_o[j])
        else:
            m = _mla(h, mla_w_down[j], mla_q_norm[j], mla_w_uq[j], mla_kv_norm[j], mla_w_ukv[j], mla_w_o[j])
        x = x + g_a[:, None, :] * m
        h = _modulate(x, sh_f, sc_f)
        if i % 2 == 0:
            f = _swiglu(h, ffn_w_gu[j], ffn_w_down[j])
        else:
            f = _moe(h, moe_w_router[j], moe_w_gu[j], moe_w_down[j])
        x = x + g_f[:, None, :] * f
    return _rmsnorm(x) * final_norm
```

```python
import functools

import jax
import jax.numpy as jnp
from jax import lax
from jax.experimental import pallas as pl
from jax.experimental.pallas import tpu as pltpu

F32 = jnp.float32
BF16 = jnp.bfloat16
HIGHEST = lax.Precision.HIGHEST

GRID_W = 64
NA_HEADS = 16
NA_WIN_H = 8
NA_WIN_W = 16
MLA_HEADS = 16
MLA_NOPE = 64
MLA_ROPE = 32
MLA_V = 64
ROPE_THETA = 10000.0
TOP_K = 2
NORM_EPS = 1e-6

LANES = 128
MXU_DIM = 256
VMEM_LIMIT = 56 * 1024 * 1024
NEG_BIG = -1e30


def _params(*sem):
    return pltpu.CompilerParams(dimension_semantics=sem, vmem_limit_bytes=VMEM_LIMIT)


def _rms(x):
    return x * lax.rsqrt(jnp.mean(x * x, axis=-1, keepdims=True) + NORM_EPS)


def _modulate(x, sh, sc):
    return _rms(x) * (1.0 + sc) + sh


def _adaln_kernel(c_ref, w_ref, b_ref, o_ref):
    c = c_ref[...]
    ca = c * jax.nn.sigmoid(c)
    o_ref[0] = jnp.dot(ca, w_ref[0], precision=HIGHEST, preferred_element_type=F32) + b_ref[0]


def _adaln(c, w_ada, b_ada):
    depth, d, n = w_ada.shape
    b = c.shape[0]
    rows = 8
    cp = jnp.zeros((rows, d), F32).at[:b].set(c)
    tn = 1536
    out = pl.pallas_call(
        _adaln_kernel,
        grid=(depth, n // tn),
        in_specs=[
            pl.BlockSpec((rows, d), lambda l, j: (0, 0)),
            pl.BlockSpec((1, d, tn), lambda l, j: (l, 0, j)),
            pl.BlockSpec((1, 1, tn), lambda l, j: (l, 0, j)),
        ],
        out_specs=pl.BlockSpec((1, rows, tn), lambda l, j: (l, 0, j)),
        out_shape=jax.ShapeDtypeStruct((depth, rows, n), F32),
        compiler_params=_params("arbitrary", "arbitrary"),
        name="adaln",
    )(cp, w_ada, b_ada.reshape(depth, 1, n))
    return out[:, :b]


def _ln_matmul_kernel(x_ref, sh_ref, sc_ref, w_ref, o_ref, h_ref):
    @pl.when(pl.program_id(2) == 0)
    def _():
        h_ref[...] = _modulate(x_ref[0], sh_ref[0], sc_ref[0]).astype(BF16)

    o_ref[0] = jnp.dot(h_ref[...], w_ref[...], preferred_element_type=F32).astype(o_ref.dtype)


def _ln_matmul(x, sh, sc, w, tm, tn):
    b, s, d = x.shape
    n = w.shape[1]
    return pl.pallas_call(
        _ln_matmul_kernel,
        grid=(b, s // tm, n // tn),
        in_specs=[
            pl.BlockSpec((1, tm, d), lambda bi, i, j: (bi, i, 0)),
            pl.BlockSpec((1, 1, d), lambda bi, i, j: (bi, 0, 0)),
            pl.BlockSpec((1, 1, d), lambda bi, i, j: (bi, 0, 0)),
            pl.BlockSpec((d, tn), lambda bi, i, j: (0, j)),
        ],
        out_specs=pl.BlockSpec((1, tm, tn), lambda bi, i, j: (bi, i, j)),
        out_shape=jax.ShapeDtypeStruct((b, s, n), BF16),
        scratch_shapes=[pltpu.VMEM((tm, d), BF16)],
        compiler_params=_params("arbitrary", "arbitrary", "arbitrary"),
        name="ln_matmul",
    )(x, sh, sc, w)


def _na_kernel(q_ref, k_ref, v_ref, t_ref, o_ref, *, rows):
    win = NA_WIN_H * GRID_W
    lane = lax.broadcasted_iota(jnp.int32, (GRID_W, LANES), 1)
    first = lane < (LANES // 2)
    scale = (LANES // 2) ** -0.5

    def body(r, carry):
        rs = jnp.clip(r - NA_WIN_H // 2, 0, rows - NA_WIN_H)
        pat = r - rs
        q = q_ref[0, pl.ds(pl.multiple_of(r * GRID_W, GRID_W), GRID_W), :]
        kw = k_ref[0, pl.ds(pl.multiple_of(rs * GRID_W, GRID_W), win), :]
        vw = v_ref[0, pl.ds(pl.multiple_of(rs * GRID_W, GRID_W), win), :]
        outs = []
        for hh in range(2):
            sel = first if hh == 0 else jnp.logical_not(first)
            qm = jnp.where(sel, q, jnp.zeros_like(q))
            s = lax.dot_general(qm, kw, (((1,), (1,)), ((), ())), preferred_element_type=F32)
            s = s * scale + t_ref[0, pat, hh]
            m = jnp.max(s, axis=-1, keepdims=True)
            p = jnp.exp(s - m)
            l = jnp.sum(p, axis=-1, keepdims=True)
            o = jnp.dot(p.astype(BF16), vw, preferred_element_type=F32)
            outs.append(o / l)
        o = jnp.where(first, outs[0], outs[1])
        o_ref[0, pl.ds(pl.multiple_of(r * GRID_W, GRID_W), GRID_W), :] = o.astype(o_ref.dtype)
        return carry

    lax.fori_loop(0, rows, body, 0)


def _na_bias_table(rpb):
    h = rpb.shape[0]
    qc = jnp.arange(GRID_W)
    kc = jnp.arange(GRID_W)
    kr = jnp.arange(NA_WIN_H)
    pt = jnp.arange(NA_WIN_H)
    cs = jnp.clip(qc - NA_WIN_W // 2, 0, GRID_W - NA_WIN_W)
    inwin = (kc[None, :] >= cs[:, None]) & (kc[None, :] < cs[:, None] + NA_WIN_W)
    dc = jnp.clip(kc[None, :] - qc[:, None] + NA_WIN_W - 1, 0, 2 * NA_WIN_W - 2)
    dr = kr[None, :] - pt[:, None] + NA_WIN_H - 1
    t = rpb[:, dr[:, :, None, None], dc[None, None, :, :]]
    t = jnp.where(inwin[None, None, None], t, NEG_BIG)
    t = t.transpose(0, 1, 3, 2, 4).reshape(h, NA_WIN_H, GRID_W, NA_WIN_H * GRID_W)
    return t.reshape(h // 2, 2, NA_WIN_H, GRID_W, NA_WIN_H * GRID_W).transpose(0, 2, 1, 3, 4)


def _neighborhood_attention(qkv, rpb):
    b, s, d3 = qkv.shape
    d = d3 // 3
    rows = s // GRID_W
    assert rows >= NA_WIN_H and d // NA_HEADS == LANES // 2
    nslab = d // LANES
    table = _na_bias_table(rpb.astype(F32))
    win = NA_WIN_H * GRID_W
    return pl.pallas_call(
        functools.partial(_na_kernel, rows=rows),
        grid=(b, nslab),
        in_specs=[
            pl.BlockSpec((1, s, LANES), lambda bi, hp: (bi, 0, hp)),
            pl.BlockSpec((1, s, LANES), lambda bi, hp: (bi, 0, nslab + hp)),
            pl.BlockSpec((1, s, LANES), lambda bi, hp: (bi, 0, 2 * nslab + hp)),
            pl.BlockSpec((1, NA_WIN_H, 2, GRID_W, win), lambda bi, hp: (hp, 0, 0, 0, 0)),
        ],
        out_specs=pl.BlockSpec((1, s, LANES), lambda bi, hp: (bi, 0, hp)),
        out_shape=jax.ShapeDtypeStruct((b, s, d), BF16),
        compiler_params=_params("arbitrary", "arbitrary"),
        name="na_attn",
    )(qkv, qkv, qkv, table)


def _proj_res_kernel(a_ref, w_ref, x_ref, g_ref, o_ref, *, feature_major):
    a = a_ref[0]
    if feature_major:
        y = lax.dot_general(a, w_ref[...], (((0,), (0,)), ((), ())), preferred_element_type=F32)
    else:
        y = jnp.dot(a, w_ref[...], preferred_element_type=F32)
    o_ref[0] = x_ref[0] + g_ref[0] * y


def _proj_res(a, w, x, g, tm, feature_major):
    b, s, d = x.shape
    k = w.shape[0]
    if feature_major:
        a_spec = pl.BlockSpec((1, k, tm), lambda bi, i: (bi, 0, i))
    else:
        a_spec = pl.BlockSpec((1, tm, k), lambda bi, i: (bi, i, 0))
    return pl.pallas_call(
        functools.partial(_proj_res_kernel, feature_major=feature_major),
        grid=(b, s // tm),
        in_specs=[
            a_spec,
            pl.BlockSpec((k, d), lambda bi, i: (0, 0)),
            pl.BlockSpec((1, tm, d), lambda bi, i: (bi, i, 0)),
            pl.BlockSpec((1, 1, d), lambda bi, i: (bi, 0, 0)),
        ],
        out_specs=pl.BlockSpec((1, tm, d), lambda bi, i: (bi, i, 0)),
        out_shape=jax.ShapeDtypeStruct((b, s, d), F32),
        compiler_params=_params("arbitrary", "arbitrary"),
        name="proj_res",
    )(a, w, x, g)


def _swiglu_tile(h, wgu, wd, f):
    acc = None
    for c in range(f // MXU_DIM):
        lo = c * MXU_DIM
        g = jnp.dot(h, wgu[:, lo:lo + MXU_DIM], preferred_element_type=F32)
        u = jnp.dot(h, wgu[:, f + lo:f + lo + MXU_DIM], preferred_element_type=F32)
        a = (g * jax.nn.sigmoid(g) * u).astype(BF16)
        y = jnp.dot(a, wd[lo:lo + MXU_DIM, :], preferred_element_type=F32)
        acc = y if acc is None else acc + y
    return acc


def _ffn_kernel(x_ref, sh_ref, sc_ref, g_ref, wgu_ref, wd_ref, o_ref, *, f):
    x = x_ref[0]
    h = _modulate(x, sh_ref[0], sc_ref[0]).astype(BF16)
    o_ref[0] = x + g_ref[0] * _swiglu_tile(h, wgu_ref, wd_ref, f)


def _ffn(x, sh, sc, g, wgu, wd, tm):
    b, s, d = x.shape
    f = wd.shape[0]
    vec = pl.BlockSpec((1, 1, d), lambda bi, i: (bi, 0, 0))
    return pl.pallas_call(
        functools.partial(_ffn_kernel, f=f),
        grid=(b, s // tm),
        in_specs=[
            pl.BlockSpec((1, tm, d), lambda bi, i: (bi, i, 0)),
            vec, vec, vec,
            pl.BlockSpec((d, 2 * f), lambda bi, i: (0, 0)),
            pl.BlockSpec((f, d), lambda bi, i: (0, 0)),
        ],
        out_specs=pl.BlockSpec((1, tm, d), lambda bi, i: (bi, i, 0)),
        out_shape=jax.ShapeDtypeStruct((b, s, d), F32),
        compiler_params=_params("arbitrary", "arbitrary"),
        name="ffn",
    )(x, sh, sc, g, wgu, wd)


def _mla_proj_kernel(x_ref, sh_ref, sc_ref, wd_ref, qn_ref, kvn_ref, wq_ref, wk_ref, wv_ref,
                     cs_ref, sn_ref, cst_ref, snt_ref, q_ref, k_ref, v_ref, *, q_lora, kv_lora, qscale):
    h = _modulate(x_ref[0], sh_ref[0], sc_ref[0]).astype(BF16)
    down = jnp.dot(h, wd_ref[...], preferred_element_type=F32)
    cq = (_rms(down[:, :q_lora]) * qn_ref[...]).astype(BF16)
    ckv = (_rms(down[:, q_lora:q_lora + kv_lora]) * kvn_ref[...]).astype(BF16)
    r0 = q_lora + kv_lora
    kr = down[:, r0:r0 + LANES] * cs_ref[...] + down[:, r0 + LANES:r0 + 2 * LANES] * sn_ref[...]
    z = jnp.concatenate([ckv, kr.astype(BF16)], axis=1)
    k_ref[0] = jnp.dot(z, wk_ref[...], preferred_element_type=F32).astype(BF16)
    nt = (((1,), (1,)), ((), ()))
    v_ref[0] = lax.dot_general(wv_ref[...], ckv, nt, preferred_element_type=F32).astype(BF16)
    qt = lax.dot_general(wq_ref[...], cq, nt, preferred_element_type=F32)
    cst = cst_ref[...]
    snt = snt_ref[...]
    for hd in range(MLA_HEADS):
        base = hd * LANES
        blk = qt[base:base + LANES]
        rot = blk[MLA_NOPE:MLA_NOPE + MLA_ROPE] * cst + blk[MLA_NOPE + MLA_ROPE:] * snt
        q_ref[0, base:base + MLA_NOPE, :] = (blk[:MLA_NOPE] * qscale).astype(BF16)
        q_ref[0, base + MLA_NOPE:base + MLA_NOPE + MLA_ROPE, :] = (rot * qscale).astype(BF16)
        q_ref[0, base + MLA_NOPE + MLA_ROPE:base + LANES, :] = jnp.zeros(
            (LANES - MLA_NOPE - MLA_ROPE, rot.shape[1]), BF16)


def _rope_perm_weights(w_rope):
    x1 = w_rope[:, 0::2]
    x2 = w_rope[:, 1::2]
    return jnp.concatenate([x1, x2], axis=1), jnp.concatenate([-x2, x1], axis=1)


def _mla_proj(x, sh, sc, w_down, q_norm, w_uq, kv_norm, w_ukv, tm):
    b, s, d = x.shape
    q_lora = q_norm.shape[0]
    kv_lora = kv_norm.shape[0]
    hd_q = MLA_NOPE + MLA_ROPE
    assert hd_q <= LANES and MLA_ROPE % 2 == 0 and q_lora % LANES == 0 and kv_lora == LANES

    r_a, r_b = _rope_perm_weights(w_down[:, q_lora + kv_lora:])
    pad = jnp.zeros((d, LANES - MLA_ROPE), F32)
    wd_ext = jnp.concatenate([w_down[:, :q_lora + kv_lora], r_a, pad, r_b, pad], axis=1).astype(BF16)

    wq = w_uq.reshape(q_lora, MLA_HEADS, hd_q)
    blocks = []
    for hd in range(MLA_HEADS):
        ra, rb = _rope_perm_weights(wq[:, hd, MLA_NOPE:])
        blocks.append(jnp.concatenate([wq[:, hd, :MLA_NOPE], ra, rb], axis=1))
    wq_ext_t = jnp.concatenate(blocks, axis=1).T.astype(BF16)

    wkv = w_ukv.reshape(kv_lora, MLA_HEADS, MLA_NOPE + MLA_V)
    eye = jnp.eye(LANES, MLA_ROPE, dtype=F32)
    kblocks = []
    for hd in range(MLA_HEADS):
        top = jnp.concatenate([wkv[:, hd, :MLA_NOPE], jnp.zeros((kv_lora, LANES - MLA_NOPE), F32)], axis=1)
        bot = jnp.concatenate([jnp.zeros((LANES, MLA_NOPE), F32), eye,
                               jnp.zeros((LANES, LANES - MLA_NOPE - MLA_ROPE), F32)], axis=1)
        kblocks.append(jnp.concatenate([top, bot], axis=0))
    wk_ext = jnp.concatenate(kblocks, axis=1).astype(BF16)
    wv_t = wkv[:, :, MLA_NOPE:].reshape(kv_lora, MLA_HEADS * MLA_V).T.astype(BF16)

    t = jnp.arange(s)
    row = (t // GRID_W).astype(F32)
    col = (t % GRID_W).astype(F32)
    nf = MLA_ROPE // 4
    inv = ROPE_THETA ** (-jnp.arange(nf, dtype=F32) / nf)
    ang = jnp.concatenate([row[:, None] * inv, col[:, None] * inv], axis=-1)
    cos2 = jnp.concatenate([jnp.cos(ang), jnp.cos(ang)], axis=1)
    sin2 = jnp.concatenate([jnp.sin(ang), jnp.sin(ang)], axis=1)
    lpad = jnp.zeros((s, LANES - MLA_ROPE), F32)
    cos_tok = jnp.concatenate([cos2, lpad], axis=1)
    sin_tok = jnp.concatenate([sin2, lpad], axis=1)
    cos_t = cos2.T
    sin_t = sin2.T

    qscale = float(hd_q ** -0.5 * 1.4426950408889634)
    nq = MLA_HEADS * LANES
    nv = MLA_HEADS * MLA_V
    nd = wd_ext.shape[1]
    full = lambda shape: pl.BlockSpec(shape, lambda bi, i: (0,) * len(shape))
    vec = pl.BlockSpec((1, 1, d), lambda bi, i: (bi, 0, 0))
    return pl.pallas_call(
        functools.partial(_mla_proj_kernel, q_lora=q_lora, kv_lora=kv_lora, qscale=qscale),
        grid=(b, s // tm),
        in_specs=[
            pl.BlockSpec((1, tm, d), lambda bi, i: (bi, i, 0)),
            vec, vec,
            full((d, nd)), full((1, q_lora)), full((1, kv_lora)),
            full((nq, q_lora)), full((2 * LANES, nq)), full((nv, kv_lora)),
            pl.BlockSpec((tm, LANES), lambda bi, i: (i, 0)),
            pl.BlockSpec((tm, LANES), lambda bi, i: (i, 0)),
            pl.BlockSpec((MLA_ROPE, tm), lambda bi, i: (0, i)),
            pl.BlockSpec((MLA_ROPE, tm), lambda bi, i: (0, i)),
        ],
        out_specs=[
            pl.BlockSpec((1, nq, tm), lambda bi, i: (bi, 0, i)),
            pl.BlockSpec((1, tm, nq), lambda bi, i: (bi, i, 0)),
            pl.BlockSpec((1, nv, tm), lambda bi, i: (bi, 0, i)),
        ],
        out_shape=[
            jax.ShapeDtypeStruct((b, nq, s), BF16),
            jax.ShapeDtypeStruct((b, s, nq), BF16),
            jax.ShapeDtypeStruct((b, nv, s), BF16),
        ],
        compiler_params=_params("arbitrary", "arbitrary"),
        name="mla_proj",
    )(x, sh, sc, wd_ext, q_norm.reshape(1, -1), kv_norm.reshape(1, -1), wq_ext_t, wk_ext, wv_t,
      cos_tok, sin_tok, cos_t, sin_t)


def _mla_attn_kernel(q_ref, k_ref, v_ref, o_ref, *, tk, nkt):
    qt = q_ref[0]
    tq = qt.shape[1]

    def body(kt, carry):
        m, l, acc = carry
        off = pl.multiple_of(kt * tk, tk)
        kb = k_ref[0, pl.ds(off, tk), :]
        st = jnp.dot(kb, qt, preferred_element_type=F32)
        m_new = jnp.maximum(m, jnp.max(st, axis=0, keepdims=True))
        alpha = jnp.exp2(m - m_new)
        p = jnp.exp2(st - m_new)
        l = alpha * l + jnp.sum(p, axis=0, keepdims=True)
        vb = v_ref[0, :, pl.ds(off, tk)]
        acc = alpha * acc + jnp.dot(vb, p.astype(BF16), preferred_element_type=F32)
        return m_new, l, acc

    init = (jnp.full((1, tq), NEG_BIG, F32), jnp.zeros((1, tq), F32), jnp.zeros((MLA_V, tq), F32))
    m, l, acc = lax.fori_loop(0, nkt, body, init)
    o_ref[0] = (acc / l).astype(o_ref.dtype)


def _mla_attention(q_t, k, v_t, tq, tk):
    b, nq, s = q_t.shape
    nv = v_t.shape[1]
    return pl.pallas_call(
        functools.partial(_mla_attn_kernel, tk=tk, nkt=s // tk),
        grid=(b, MLA_HEADS, s // tq),
        in_specs=[
            pl.BlockSpec((1, LANES, tq), lambda bi, hd, i: (bi, hd, i)),
            pl.BlockSpec((1, s, LANES), lambda bi, hd, i: (bi, 0, hd)),
            pl.BlockSpec((1, MLA_V, s), lambda bi, hd, i: (bi, hd, 0)),
        ],
        out_specs=pl.BlockSpec((1, MLA_V, tq), lambda bi, hd, i: (bi, hd, i)),
        out_shape=jax.ShapeDtypeStruct((b, nv, s), BF16),
        compiler_params=_params("arbitrary", "arbitrary", "arbitrary"),
        name="mla_attn",
    )(q_t, k, v_t)


def _router_kernel(x_ref, sh_ref, sc_ref, wr_ref, o_ref, *, n_exp):
    h = _modulate(x_ref[0], sh_ref[0], sc_ref[0])
    logits = jnp.dot(h, wr_ref[...], precision=HIGHEST, preferred_element_type=F32)
    lane = lax.broadcasted_iota(jnp.int32, logits.shape, 1)
    valid = lane < n_exp
    logits = jnp.where(valid, logits, NEG_BIG)
    e = jnp.exp(logits - jnp.max(logits, axis=-1, keepdims=True))
    probs = e / jnp.sum(e, axis=-1, keepdims=True)
    probs = jnp.where(valid, probs, -1.0)
    chosen = []
    rest = probs
    for _ in range(TOP_K):
        v = jnp.max(rest, axis=-1, keepdims=True)
        idx = jnp.min(jnp.where(rest == v, lane, LANES), axis=-1, keepdims=True)
        hit = lane == idx
        chosen.append((v, hit))
        rest = jnp.where(hit, -1.0, rest)
    tot = chosen[0][0]
    for v, _ in chosen[1:]:
        tot = tot + v
    comb = jnp.zeros_like(probs)
    for v, hit in chosen:
        comb = comb + jnp.where(hit, v / tot, 0.0)
    o_ref[0] = comb


def _router(x, sh, sc, w_router, tm):
    b, s, d = x.shape
    n_exp = w_router.shape[1]
    wr = jnp.zeros((d, LANES), F32).at[:, :n_exp].set(w_router)
    vec = pl.BlockSpec((1, 1, d), lambda bi, i: (bi, 0, 0))
    return pl.pallas_call(
        functools.partial(_router_kernel, n_exp=n_exp),
        grid=(b, s // tm),
        in_specs=[
            pl.BlockSpec((1, tm, d), lambda bi, i: (bi, i, 0)),
            vec, vec,
            pl.BlockSpec((d, LANES), lambda bi, i: (0, 0)),
        ],
        out_specs=pl.BlockSpec((1, tm, LANES), lambda bi, i: (bi, i, 0)),
        out_shape=jax.ShapeDtypeStruct((b, s, LANES), F32),
        compiler_params=_params("arbitrary", "arbitrary"),
        name="router",
    )(x, sh, sc, wr)


def _moe_kernel(x_ref, sh_ref, sc_ref, g_ref, cw_ref, wgu_ref, wd_ref, fn_ref, o_ref, h_ref, acc_ref,
                *, f, n_exp):
    e = pl.program_id(2)

    @pl.when(e == 0)
    def _():
        h_ref[...] = _modulate(x_ref[0], sh_ref[0], sc_ref[0]).astype(BF16)
        acc_ref[...] = jnp.zeros_like(acc_ref)

    y = _swiglu_tile(h_ref[...], wgu_ref.at[0], wd_ref.at[0], f)
    cw = cw_ref[0]
    lane = lax.broadcasted_iota(jnp.int32, cw.shape, 1)
    w_e = jnp.sum(jnp.where(lane == e, cw, 0.0), axis=-1, keepdims=True)
    acc_ref[...] += w_e * y

    @pl.when(e == n_exp - 1)
    def _():
        x = x_ref[0] + g_ref[0] * acc_ref[...]
        o_ref[0] = _rms(x) * fn_ref[...]


def _moe(x, sh, sc, g, comb, wgu, wd, final_norm, tm):
    b, s, d = x.shape
    n_exp, f, _ = wd.shape
    vec = pl.BlockSpec((1, 1, d), lambda bi, i, e: (bi, 0, 0))
    return pl.pallas_call(
        functools.partial(_moe_kernel, f=f, n_exp=n_exp),
        grid=(b, s // tm, n_exp),
        in_specs=[
            pl.BlockSpec((1, tm, d), lambda bi, i, e: (bi, i, 0)),
            vec, vec, vec,
            pl.BlockSpec((1, tm, LANES), lambda bi, i, e: (bi, i, 0)),
            pl.BlockSpec((1, d, 2 * f), lambda bi, i, e: (e, 0, 0)),
            pl.BlockSpec((1, f, d), lambda bi, i, e: (e, 0, 0)),
            pl.BlockSpec((1, d), lambda bi, i, e: (0, 0)),
        ],
        out_specs=pl.BlockSpec((1, tm, d), lambda bi, i, e: (bi, i, 0)),
        out_shape=jax.ShapeDtypeStruct((b, s, d), F32),
        scratch_shapes=[pltpu.VMEM((tm, d), BF16), pltpu.VMEM((tm, d), F32)],
        compiler_params=_params("arbitrary", "arbitrary", "arbitrary"),
        name="moe",
    )(x, sh, sc, g, comb, wgu, wd, final_norm.reshape(1, d))


def kernel(x, c, w_ada, b_ada, na_w_qkv, na_rpb, na_w_o, ffn_w_gu, ffn_w_down, mla_w_down, mla_q_norm,
           mla_w_uq, mla_kv_norm, mla_w_ukv, mla_w_o, moe_w_router, moe_w_gu, moe_w_down, final_norm):
    b, s, d = x.shape
    mods = _adaln(c, w_ada, b_ada)

    def split(layer):
        return [v.reshape(b, 1, d) for v in jnp.split(mods[layer], 6, axis=-1)]

    tm = min(1024, s)
    tmh = min(512, s)

    sh_a, sc_a, g_a, sh_f, sc_f, g_f = split(0)
    qkv = _ln_matmul(x, sh_a, sc_a, na_w_qkv[0].astype(BF16), tm, 1024)
    att = _neighborhood_attention(qkv, na_rpb[0])
    x = _proj_res(att, na_w_o[0].astype(BF16), x, g_a, tm, feature_major=False)
    x = _ffn(x, sh_f, sc_f, g_f, ffn_w_gu[0].astype(BF16), ffn_w_down[0].astype(BF16), tmh)

    sh_a, sc_a, g_a, sh_f, sc_f, g_f = split(1)
    q_t, k, v_t = _mla_proj(x, sh_a, sc_a, mla_w_down[0], mla_q_norm[0], mla_w_uq[0], mla_kv_norm[0],
                            mla_w_ukv[0], tmh)
    o_t = _mla_attention(q_t, k, v_t, min(512, s), min(512, s))
    x = _proj_res(o_t, mla_w_o[0].astype(BF16), x, g_a, tm, feature_major=True)
    comb = _router(x, sh_f, sc_f, moe_w_router[0], tm)
    return _moe(x, sh_f, sc_f, g_f, comb, moe_w_gu[0].astype(BF16), moe_w_down[0].astype(BF16),
                final_norm, tmh)
```

```python
import functools

import jax
import jax.numpy as jnp
from jax import lax
from jax.experimental import pallas as pl
from jax.experimental.pallas import tpu as pltpu

F32 = jnp.float32
BF16 = jnp.bfloat16
HIGHEST = lax.Precision.HIGHEST

GRID_W = 64
NA_HEADS = 16
NA_WIN_H = 8
NA_WIN_W = 16
NA_ROWS_PER_STEP = 4
MLA_HEADS = 16
MLA_NOPE = 64
MLA_ROPE = 32
MLA_V = 64
ROPE_THETA = 10000.0
TOP_K = 2
NORM_EPS = 1e-6

LANES = 128
MXU_DIM = 256
SUM_ROWS = 16
VMEM_LIMIT = 56 * 1024 * 1024
NEG_BIG = -1e30


def _params(*sem):
    return pltpu.CompilerParams(dimension_semantics=sem, vmem_limit_bytes=VMEM_LIMIT)


def _rms(x):
    return x * lax.rsqrt(jnp.mean(x * x, axis=-1, keepdims=True) + NORM_EPS)


def _modulate(x, sh, sc):
    return _rms(x) * (1.0 + sc) + sh


def _adaln_kernel(c_ref, w_ref, b_ref, o_ref):
    c = c_ref[...]
    ca = c * jax.nn.sigmoid(c)
    o_ref[0] = jnp.dot(ca, w_ref[0], precision=HIGHEST, preferred_element_type=F32) + b_ref[0]


def _adaln(c, w_ada, b_ada):
    depth, d, n = w_ada.shape
    b = c.shape[0]
    rows = 8
    cp = jnp.zeros((rows, d), F32).at[:b].set(c)
    tn = 1536
    out = pl.pallas_call(
        _adaln_kernel,
        grid=(depth, n // tn),
        in_specs=[
            pl.BlockSpec((rows, d), lambda l, j: (0, 0)),
            pl.BlockSpec((1, d, tn), lambda l, j: (l, 0, j)),
            pl.BlockSpec((1, 1, tn), lambda l, j: (l, 0, j)),
        ],
        out_specs=pl.BlockSpec((1, rows, tn), lambda l, j: (l, 0, j)),
        out_shape=jax.ShapeDtypeStruct((depth, rows, n), F32),
        compiler_params=_params("arbitrary", "arbitrary"),
        name="adaln",
    )(cp, w_ada, b_ada.reshape(depth, 1, n))
    return out[:, :b]


def _ln_matmul_kernel(x_ref, sh_ref, sc_ref, w_ref, o_ref, h_ref):
    @pl.when(pl.program_id(2) == 0)
    def _():
        h_ref[...] = _modulate(x_ref[0], sh_ref[0], sc_ref[0]).astype(BF16)

    o_ref[0] = jnp.dot(h_ref[...], w_ref[...], preferred_element_type=F32).astype(o_ref.dtype)


def _ln_matmul(x, sh, sc, w, tm, tn):
    b, s, d = x.shape
    n = w.shape[1]
    return pl.pallas_call(
        _ln_matmul_kernel,
        grid=(b, s // tm, n // tn),
        in_specs=[
            pl.BlockSpec((1, tm, d), lambda bi, i, j: (bi, i, 0)),
            pl.BlockSpec((1, 1, d), lambda bi, i, j: (bi, 0, 0)),
            pl.BlockSpec((1, 1, d), lambda bi, i, j: (bi, 0, 0)),
            pl.BlockSpec((d, tn), lambda bi, i, j: (0, j)),
        ],
        out_specs=pl.BlockSpec((1, tm, tn), lambda bi, i, j: (bi, i, j)),
        out_shape=jax.ShapeDtypeStruct((b, s, n), BF16),
        scratch_shapes=[pltpu.VMEM((tm, d), BF16)],
        compiler_params=_params("arbitrary", "arbitrary", "arbitrary"),
        name="ln_matmul",
    )(x, sh, sc, w)


def _na_kernel(q_ref, k_ref, v_ref, t_ref, o_ref, *, rows):
    win = NA_WIN_H * GRID_W
    lane = lax.broadcasted_iota(jnp.int32, (GRID_W, LANES), 1)
    first = lane < (LANES // 2)
    scale = (LANES // 2) ** -0.5

    def body(i, carry):
        rr = [i * NA_ROWS_PER_STEP + j for j in range(NA_ROWS_PER_STEP)]
        rss = [jnp.clip(r - NA_WIN_H // 2, 0, rows - NA_WIN_H) for r in rr]
        ss = []
        for r, rs in zip(rr, rss):
            q = q_ref[0, pl.ds(pl.multiple_of(r * GRID_W, GRID_W), GRID_W), :]
            zero = jnp.zeros_like(q)
            q2 = jnp.concatenate([jnp.where(first, q, zero), jnp.where(first, zero, q)], axis=0)
            kw = k_ref[0, pl.ds(pl.multiple_of(rs * GRID_W, GRID_W), win), :]
            s = lax.dot_general(q2, kw, (((1,), (1,)), ((), ())), preferred_element_type=F32)
            ss.append(s * scale + t_ref[0, r - rs])
        ps, ls = [], []
        for s in ss:
            p = jnp.exp(s - jnp.max(s, axis=-1, keepdims=True))
            ls.append(jnp.sum(p, axis=-1, keepdims=True))
            ps.append(p.astype(BF16))
        for r, rs, p, l in zip(rr, rss, ps, ls):
            vw = v_ref[0, pl.ds(pl.multiple_of(rs * GRID_W, GRID_W), win), :]
            o = jnp.dot(p, vw, preferred_element_type=F32) / l
            o = jnp.where(first, o[:GRID_W], o[GRID_W:])
            o_ref[0, pl.ds(pl.multiple_of(r * GRID_W, GRID_W), GRID_W), :] = o.astype(o_ref.dtype)
        return carry

    lax.fori_loop(0, rows // NA_ROWS_PER_STEP, body, 0)


def _na_bias_table(rpb):
    h, ndr, ndc = rpb.shape
    qc = jnp.arange(GRID_W)
    kc = jnp.arange(GRID_W)
    cs = jnp.clip(qc - NA_WIN_W // 2, 0, GRID_W - NA_WIN_W)
    inwin = (kc[None, :] >= cs[:, None]) & (kc[None, :] < cs[:, None] + NA_WIN_W)
    period = 2 * GRID_W
    u = jnp.concatenate([rpb[:, :, NA_WIN_W - 1:], jnp.zeros((h, ndr, period - ndc), F32),
                         rpb[:, :, :NA_WIN_W - 1]], axis=-1)
    flat = jnp.tile(u, (1, 1, GRID_W))[:, :, :GRID_W * (period - 1)]
    col = flat.reshape(h, ndr, GRID_W, period - 1)[:, :, :, :GRID_W]
    t = jnp.stack([col[:, NA_WIN_H - 1 - p:2 * NA_WIN_H - 1 - p] for p in range(NA_WIN_H)], axis=1)
    t = jnp.where(inwin[None, None, None], t, NEG_BIG)
    t = t.transpose(0, 1, 3, 2, 4).reshape(h, NA_WIN_H, GRID_W, NA_WIN_H * GRID_W)
    t = t.reshape(h // 2, 2, NA_WIN_H, GRID_W, NA_WIN_H * GRID_W).transpose(0, 2, 1, 3, 4)
    return t.reshape(h // 2, NA_WIN_H, 2 * GRID_W, NA_WIN_H * GRID_W)


def _neighborhood_attention(qkv, rpb):
    b, s, d3 = qkv.shape
    d = d3 // 3
    rows = s // GRID_W
    assert rows >= NA_WIN_H and d // NA_HEADS == LANES // 2 and rows % NA_ROWS_PER_STEP == 0
    nslab = d // LANES
    table = _na_bias_table(rpb.astype(F32))
    win = NA_WIN_H * GRID_W
    return pl.pallas_call(
        functools.partial(_na_kernel, rows=rows),
        grid=(b, nslab),
        in_specs=[
            pl.BlockSpec((1, s, LANES), lambda bi, hp: (bi, 0, hp)),
            pl.BlockSpec((1, s, LANES), lambda bi, hp: (bi, 0, nslab + hp)),
            pl.BlockSpec((1, s, LANES), lambda bi, hp: (bi, 0, 2 * nslab + hp)),
            pl.BlockSpec((1, NA_WIN_H, 2 * GRID_W, win), lambda bi, hp: (hp, 0, 0, 0)),
        ],
        out_specs=pl.BlockSpec((1, s, LANES), lambda bi, hp: (bi, 0, hp)),
        out_shape=jax.ShapeDtypeStruct((b, s, d), BF16),
        compiler_params=_params("arbitrary", "arbitrary"),
        name="na_attn",
    )(qkv, qkv, qkv, table)


def _proj_res_kernel(a_ref, w_ref, x_ref, g_ref, o_ref, *, feature_major):
    a = a_ref[0]
    if feature_major:
        y = lax.dot_general(a, w_ref[...], (((0,), (0,)), ((), ())), preferred_element_type=F32)
    else:
        y = jnp.dot(a, w_ref[...], preferred_element_type=F32)
    o_ref[0] = x_ref[0] + g_ref[0] * y


def _proj_res(a, w, x, g, tm, feature_major):
    b, s, d = x.shape
    k = w.shape[0]
    if feature_major:
        a_spec = pl.BlockSpec((1, k, tm), lambda bi, i: (bi, 0, i))
    else:
        a_spec = pl.BlockSpec((1, tm, k), lambda bi, i: (bi, i, 0))
    return pl.pallas_call(
        functools.partial(_proj_res_kernel, feature_major=feature_major),
        grid=(b, s // tm),
        in_specs=[
            a_spec,
            pl.BlockSpec((k, d), lambda bi, i: (0, 0)),
            pl.BlockSpec((1, tm, d), lambda bi, i: (bi, i, 0)),
            pl.BlockSpec((1, 1, d), lambda bi, i: (bi, 0, 0)),
        ],
        out_specs=pl.BlockSpec((1, tm, d), lambda bi, i: (bi, i, 0)),
        out_shape=jax.ShapeDtypeStruct((b, s, d), F32),
        compiler_params=_params("arbitrary", "arbitrary"),
        name="proj_res",
    )(a, w, x, g)


def _swiglu_tile(h, wgu, wd, f):
    acc = None
    for c in range(f // MXU_DIM):
        lo = c * MXU_DIM
        g = jnp.dot(h, wgu[:, lo:lo + MXU_DIM], preferred_element_type=F32)
        u = jnp.dot(h, wgu[:, f + lo:f + lo + MXU_DIM], preferred_element_type=F32)
        a = (g * jax.nn.sigmoid(g) * u).astype(BF16)
        y = jnp.dot(a, wd[lo:lo + MXU_DIM, :], preferred_element_type=F32)
        acc = y if acc is None else acc + y
    return acc


def _ffn_kernel(x_ref, sh_ref, sc_ref, g_ref, wgu_ref, wd_ref, o_ref, *, f):
    x = x_ref[0]
    h = _modulate(x, sh_ref[0], sc_ref[0]).astype(BF16)
    o_ref[0] = x + g_ref[0] * _swiglu_tile(h, wgu_ref, wd_ref, f)


def _ffn(x, sh, sc, g, wgu, wd, tm):
    b, s, d = x.shape
    f = wd.shape[0]
    vec = pl.BlockSpec((1, 1, d), lambda bi, i: (bi, 0, 0))
    return pl.pallas_call(
        functools.partial(_ffn_kernel, f=f),
        grid=(b, s // tm),
        in_specs=[
            pl.BlockSpec((1, tm, d), lambda bi, i: (bi, i, 0)),
            vec, vec, vec,
            pl.BlockSpec((d, 2 * f), lambda bi, i: (0, 0)),
            pl.BlockSpec((f, d), lambda bi, i: (0, 0)),
        ],
        out_specs=pl.BlockSpec((1, tm, d), lambda bi, i: (bi, i, 0)),
        out_shape=jax.ShapeDtypeStruct((b, s, d), F32),
        compiler_params=_params("arbitrary", "arbitrary"),
        name="ffn",
    )(x, sh, sc, g, wgu, wd)


def _mla_proj_kernel(x_ref, sh_ref, sc_ref, wd_ref, qn_ref, kvn_ref, wq_ref, wk_ref, wv_ref,
                     cs_ref, sn_ref, cst_ref, snt_ref, q_ref, k_ref, v_ref, *, q_lora, kv_lora, qscale):
    h = _modulate(x_ref[0], sh_ref[0], sc_ref[0]).astype(BF16)
    down = jnp.dot(h, wd_ref[...], preferred_element_type=F32)
    cq = (_rms(down[:, :q_lora]) * qn_ref[...]).astype(BF16)
    ckv = (_rms(down[:, q_lora:q_lora + kv_lora]) * kvn_ref[...]).astype(BF16)
    r0 = q_lora + kv_lora
    kr = down[:, r0:r0 + LANES] * cs_ref[...] + down[:, r0 + LANES:r0 + 2 * LANES] * sn_ref[...]
    z = jnp.concatenate([ckv, kr.astype(BF16)], axis=1)
    k_ref[0] = jnp.dot(z, wk_ref[...], preferred_element_type=F32).astype(BF16)
    nt = (((1,), (1,)), ((), ()))
    v_ref[0] = lax.dot_general(wv_ref[...], ckv, nt, preferred_element_type=F32).astype(BF16)
    qt = lax.dot_general(wq_ref[...], cq, nt, preferred_element_type=F32)
    cst = cst_ref[...]
    snt = snt_ref[...]
    for hd in range(MLA_HEADS):
        base = hd * LANES
        blk = qt[base:base + LANES]
        rot = blk[MLA_NOPE:MLA_NOPE + MLA_ROPE] * cst + blk[MLA_NOPE + MLA_ROPE:] * snt
        q_ref[0, base:base + MLA_NOPE, :] = (blk[:MLA_NOPE] * qscale).astype(BF16)
        q_ref[0, base + MLA_NOPE:base + MLA_NOPE + MLA_ROPE, :] = (rot * qscale).astype(BF16)
        q_ref[0, base + MLA_NOPE + MLA_ROPE:base + LANES, :] = jnp.zeros(
            (LANES - MLA_NOPE - MLA_ROPE, rot.shape[1]), BF16)


def _rope_perm_weights(w_rope):
    x1 = w_rope[:, 0::2]
    x2 = w_rope[:, 1::2]
    return jnp.concatenate([x1, x2], axis=1), jnp.concatenate([-x2, x1], axis=1)


def _mla_proj(x, sh, sc, w_down, q_norm, w_uq, kv_norm, w_ukv, tm):
    b, s, d = x.shape
    q_lora = q_norm.shape[0]
    kv_lora = kv_norm.shape[0]
    hd_q = MLA_NOPE + MLA_ROPE
    assert hd_q <= LANES and MLA_ROPE % 2 == 0 and q_lora % LANES == 0 and kv_lora == LANES

    r_a, r_b = _rope_perm_weights(w_down[:, q_lora + kv_lora:])
    pad = jnp.zeros((d, LANES - MLA_ROPE), F32)
    wd_ext = jnp.concatenate([w_down[:, :q_lora + kv_lora], r_a, pad, r_b, pad], axis=1).astype(BF16)

    wq = w_uq.reshape(q_lora, MLA_HEADS, hd_q)
    blocks = []
    for hd in range(MLA_HEADS):
        ra, rb = _rope_perm_weights(wq[:, hd, MLA_NOPE:])
        blocks.append(jnp.concatenate([wq[:, hd, :MLA_NOPE], ra, rb], axis=1))
    wq_ext_t = jnp.concatenate(blocks, axis=1).T.astype(BF16)

    wkv = w_ukv.reshape(kv_lora, MLA_HEADS, MLA_NOPE + MLA_V)
    eye = jnp.eye(LANES, MLA_ROPE, dtype=F32)
    kblocks = []
    for hd in range(MLA_HEADS):
        top = jnp.concatenate([wkv[:, hd, :MLA_NOPE], jnp.zeros((kv_lora, LANES - MLA_NOPE), F32)], axis=1)
        bot = jnp.concatenate([jnp.zeros((LANES, MLA_NOPE), F32), eye,
                               jnp.zeros((LANES, LANES - MLA_NOPE - MLA_ROPE), F32)], axis=1)
        kblocks.append(jnp.concatenate([top, bot], axis=0))
    wk_ext = jnp.concatenate(kblocks, axis=1).astype(BF16)
    wv_t = wkv[:, :, MLA_NOPE:].reshape(kv_lora, MLA_HEADS * MLA_V).T.astype(BF16)

    t = jnp.arange(s)
    row = (t // GRID_W).astype(F32)
    col = (t % GRID_W).astype(F32)
    nf = MLA_ROPE // 4
    inv = ROPE_THETA ** (-jnp.arange(nf, dtype=F32) / nf)
    ang = jnp.concatenate([row[:, None] * inv, col[:, None] * inv], axis=-1)
    cos2 = jnp.concatenate([jnp.cos(ang), jnp.cos(ang)], axis=1)
    sin2 = jnp.concatenate([jnp.sin(ang), jnp.sin(ang)], axis=1)
    lpad = jnp.zeros((s, LANES - MLA_ROPE), F32)
    cos_tok = jnp.concatenate([cos2, lpad], axis=1)
    sin_tok = jnp.concatenate([sin2, lpad], axis=1)
    cos_t = cos2.T
    sin_t = sin2.T

    qscale = float(hd_q ** -0.5 * 1.4426950408889634)
    nq = MLA_HEADS * LANES
    nv = MLA_HEADS * MLA_V
    nd = wd_ext.shape[1]
    full = lambda shape: pl.BlockSpec(shape, lambda bi, i: (0,) * len(shape))
    vec = pl.BlockSpec((1, 1, d), lambda bi, i: (bi, 0, 0))
    return pl.pallas_call(
        functools.partial(_mla_proj_kernel, q_lora=q_lora, kv_lora=kv_lora, qscale=qscale),
        grid=(b, s // tm),
        in_specs=[
            pl.BlockSpec((1, tm, d), lambda bi, i: (bi, i, 0)),
            vec, vec,
            full((d, nd)), full((1, q_lora)), full((1, kv_lora)),
            full((nq, q_lora)), full((2 * LANES, nq)), full((nv, kv_lora)),
            pl.BlockSpec((tm, LANES), lambda bi, i: (i, 0)),
            pl.BlockSpec((tm, LANES), lambda bi, i: (i, 0)),
            pl.BlockSpec((MLA_ROPE, tm), lambda bi, i: (0, i)),
            pl.BlockSpec((MLA_ROPE, tm), lambda bi, i: (0, i)),
        ],
        out_specs=[
            pl.BlockSpec((1, nq, tm), lambda bi, i: (bi, 0, i)),
            pl.BlockSpec((1, tm, nq), lambda bi, i: (bi, i, 0)),
            pl.BlockSpec((1, nv, tm), lambda bi, i: (bi, 0, i)),
        ],
        out_shape=[
            jax.ShapeDtypeStruct((b, nq, s), BF16),
            jax.ShapeDtypeStruct((b, s, nq), BF16),
            jax.ShapeDtypeStruct((b, nv, s), BF16),
        ],
        compiler_params=_params("arbitrary", "arbitrary"),
        name="mla_proj",
    )(x, sh, sc, wd_ext, q_norm.reshape(1, -1), kv_norm.reshape(1, -1), wq_ext_t, wk_ext, wv_t,
      cos_tok, sin_tok, cos_t, sin_t)


def _mla_attn_kernel(q_ref, k_ref, v_ref, o_ref, s_ref, *, tk, nkt):
    qt = q_ref[0]
    tq = qt.shape[1]

    ones = jnp.ones((SUM_ROWS, tk), BF16)

    def stage_qk(t, slot):
        off = pl.multiple_of(t * tk, tk)
        s_ref[slot] = jnp.dot(k_ref[0, pl.ds(off, tk), :], qt, preferred_element_type=F32)

    def stage_softmax_pv(t, slot, carry):
        m, acc = carry
        st = s_ref[slot]
        m_new = jnp.maximum(m, jnp.max(st, axis=0, keepdims=True))
        p = jnp.exp2(st - m_new).astype(BF16)
        vb = v_ref[0, :, pl.ds(pl.multiple_of(t * tk, tk), tk)]
        vb = jnp.concatenate([vb, ones], axis=0)
        acc = jnp.exp2(m - m_new) * acc + jnp.dot(vb, p, preferred_element_type=F32)
        return m_new, acc

    stage_qk(0, 0)

    def body(i, carry):
        t = 2 * i
        stage_qk(t + 1, 1)
        carry = stage_softmax_pv(t, 0, carry)
        stage_qk(t + 2, 0)
        return stage_softmax_pv(t + 1, 1, carry)

    carry = (jnp.full((1, tq), NEG_BIG, F32), jnp.zeros((MLA_V + SUM_ROWS, tq), F32))
    carry = lax.fori_loop(0, nkt // 2 - 1, body, carry)
    stage_qk(nkt - 1, 1)
    carry = stage_softmax_pv(nkt - 2, 0, carry)
    m, acc = stage_softmax_pv(nkt - 1, 1, carry)
    o_ref[0] = (acc[:MLA_V] / acc[MLA_V:MLA_V + 1]).astype(o_ref.dtype)


def _mla_attention(q_t, k, v_t, tq, tk):
    b, nq, s = q_t.shape
    nv = v_t.shape[1]
    assert (s // tk) % 2 == 0
    return pl.pallas_call(
        functools.partial(_mla_attn_kernel, tk=tk, nkt=s // tk),
        grid=(b, MLA_HEADS, s // tq),
        in_specs=[
            pl.BlockSpec((1, LANES, tq), lambda bi, hd, i: (bi, hd, i)),
            pl.BlockSpec((1, s, LANES), lambda bi, hd, i: (bi, 0, hd)),
            pl.BlockSpec((1, MLA_V, s), lambda bi, hd, i: (bi, hd, 0)),
        ],
        out_specs=pl.BlockSpec((1, MLA_V, tq), lambda bi, hd, i: (bi, hd, i)),
        out_shape=jax.ShapeDtypeStruct((b, nv, s), BF16),
        scratch_shapes=[pltpu.VMEM((2, tk, tq), F32)],
        compiler_params=_params("arbitrary", "arbitrary", "arbitrary"),
        name="mla_attn",
    )(q_t, k, v_t)


def _router_kernel(x_ref, sh_ref, sc_ref, wr_ref, o_ref, *, n_exp):
    h = _modulate(x_ref[0], sh_ref[0], sc_ref[0])
    logits = jnp.dot(h, wr_ref[...], precision=HIGHEST, preferred_element_type=F32)
    lane = lax.broadcasted_iota(jnp.int32, logits.shape, 1)
    valid = lane < n_exp
    logits = jnp.where(valid, logits, NEG_BIG)
    e = jnp.exp(logits - jnp.max(logits, axis=-1, keepdims=True))
    probs = e / jnp.sum(e, axis=-1, keepdims=True)
    probs = jnp.where(valid, probs, -1.0)
    chosen = []
    rest = probs
    for _ in range(TOP_K):
        v = jnp.max(rest, axis=-1, keepdims=True)
        idx = jnp.min(jnp.where(rest == v, lane, LANES), axis=-1, keepdims=True)
        hit = lane == idx
        chosen.append((v, hit))
        rest = jnp.where(hit, -1.0, rest)
    tot = chosen[0][0]
    for v, _ in chosen[1:]:
        tot = tot + v
    comb = jnp.zeros_like(probs)
    for v, hit in chosen:
        comb = comb + jnp.where(hit, v / tot, 0.0)
    o_ref[0] = comb


def _router(x, sh, sc, w_router, tm):
    b, s, d = x.shape
    n_exp = w_router.shape[1]
    wr = jnp.zeros((d, LANES), F32).at[:, :n_exp].set(w_router)
    vec = pl.BlockSpec((1, 1, d), lambda bi, i: (bi, 0, 0))
    return pl.pallas_call(
        functools.partial(_router_kernel, n_exp=n_exp),
        grid=(b, s // tm),
        in_specs=[
            pl.BlockSpec((1, tm, d), lambda bi, i: (bi, i, 0)),
            vec, vec,
            pl.BlockSpec((d, LANES), lambda bi, i: (0, 0)),
        ],
        out_specs=pl.BlockSpec((1, tm, LANES), lambda bi, i: (bi, i, 0)),
        out_shape=jax.ShapeDtypeStruct((b, s, LANES), F32),
        compiler_params=_params("arbitrary", "arbitrary"),
        name="router",
    )(x, sh, sc, wr)


def _moe_kernel(x_ref, sh_ref, sc_ref, g_ref, cw_ref, wgu_ref, wd_ref, fn_ref, o_ref, h_ref, acc_ref,
                *, f, n_exp):
    e = pl.program_id(2)

    @pl.when(e == 0)
    def _():
        h_ref[...] = _modulate(x_ref[0], sh_ref[0], sc_ref[0]).astype(BF16)
        acc_ref[...] = jnp.zeros_like(acc_ref)

    y = _swiglu_tile(h_ref[...], wgu_ref.at[0], wd_ref.at[0], f)
    cw = cw_ref[0]
    lane = lax.broadcasted_iota(jnp.int32, cw.shape, 1)
    w_e = jnp.sum(jnp.where(lane == e, cw, 0.0), axis=-1, keepdims=True)
    acc_ref[...] += w_e * y

    @pl.when(e == n_exp - 1)
    def _():
        x = x_ref[0] + g_ref[0] * acc_ref[...]
        o_ref[0] = _rms(x) * fn_ref[...]


def _moe(x, sh, sc, g, comb, wgu, wd, final_norm, tm):
    b, s, d = x.shape
    n_exp, f, _ = wd.shape
    vec = pl.BlockSpec((1, 1, d), lambda bi, i, e: (bi, 0, 0))
    return pl.pallas_call(
        functools.partial(_moe_kernel, f=f, n_exp=n_exp),
        grid=(b, s // tm, n_exp),
        in_specs=[
            pl.BlockSpec((1, tm, d), lambda bi, i, e: (bi, i, 0)),
            vec, vec, vec,
            pl.BlockSpec((1, tm, LANES), lambda bi, i, e: (bi, i, 0)),
            pl.BlockSpec((1, d, 2 * f), lambda bi, i, e: (e, 0, 0)),
            pl.BlockSpec((1, f, d), lambda bi, i, e: (e, 0, 0)),
            pl.BlockSpec((1, d), lambda bi, i, e: (0, 0)),
        ],
        out_specs=pl.BlockSpec((1, tm, d), lambda bi, i, e: (bi, i, 0)),
        out_shape=jax.ShapeDtypeStruct((b, s, d), F32),
        scratch_shapes=[pltpu.VMEM((tm, d), BF16), pltpu.VMEM((tm, d), F32)],
        compiler_params=_params("arbitrary", "arbitrary", "arbitrary"),
        name="moe",
    )(x, sh, sc, g, comb, wgu, wd, final_norm.reshape(1, d))


def kernel(x, c, w_ada, b_ada, na_w_qkv, na_rpb, na_w_o, ffn_w_gu, ffn_w_down, mla_w_down, mla_q_norm,
           mla_w_uq, mla_kv_norm, mla_w_ukv, mla_w_o, moe_w_router, moe_w_gu, moe_w_down, final_norm):
    b, s, d = x.shape
    mods = _adaln(c, w_ada, b_ada)

    def split(layer):
        return [v.reshape(b, 1, d) for v in jnp.split(mods[layer], 6, axis=-1)]

    tm = min(1024, s)
    tmh = min(512, s)

    sh_a, sc_a, g_a, sh_f, sc_f, g_f = split(0)
    qkv = _ln_matmul(x, sh_a, sc_a, na_w_qkv[0].astype(BF16), tm, 1024)
    att = _neighborhood_attention(qkv, na_rpb[0])
    x = _proj_res(att, na_w_o[0].astype(BF16), x, g_a, tm, feature_major=False)
    x = _ffn(x, sh_f, sc_f, g_f, ffn_w_gu[0].astype(BF16), ffn_w_down[0].astype(BF16), tmh)

    sh_a, sc_a, g_a, sh_f, sc_f, g_f = split(1)
    q_t, k, v_t = _mla_proj(x, sh_a, sc_a, mla_w_down[0], mla_q_norm[0], mla_w_uq[0], mla_kv_norm[0],
                            mla_w_ukv[0], tmh)
    o_t = _mla_attention(q_t, k, v_t, min(512, s), min(1024, s // 2))
    x = _proj_res(o_t, mla_w_o[0].astype(BF16), x, g_a, tm, feature_major=True)
    comb = _router(x, sh_f, sc_f, moe_w_router[0], tm)
    return _moe(x, sh_f, sc_f, g_f, comb, moe_w_gu[0].astype(BF16), moe_w_down[0].astype(BF16),
                final_norm, tmh)
```

```python
import functools

import jax
import jax.numpy as jnp
from jax import lax
from jax.experimental import pallas as pl
from jax.experimental.pallas import tpu as pltpu

F32 = jnp.float32
BF16 = jnp.bfloat16
HIGHEST = lax.Precision.HIGHEST

GRID_W = 64
NA_HEADS = 16
NA_WIN_H = 8
NA_WIN_W = 16
NA_ROWS_PER_STEP = 4
MLA_HEADS = 16
MLA_NOPE = 64
MLA_ROPE = 32
MLA_V = 64
ROPE_THETA = 10000.0
TOP_K = 2
NORM_EPS = 1e-6

LANES = 128
MXU_DIM = 256
SUM_ROWS = 16
MOE_GATHER_ROWS = 2048
VMEM_LIMIT = 56 * 1024 * 1024
NEG_BIG = -1e30


def _params(*sem):
    return pltpu.CompilerParams(dimension_semantics=sem, vmem_limit_bytes=VMEM_LIMIT)


def _rms(x):
    return x * lax.rsqrt(jnp.mean(x * x, axis=-1, keepdims=True) + NORM_EPS)


def _modulate(x, sh, sc):
    return _rms(x) * (1.0 + sc) + sh


def _adaln_kernel(c_ref, w_ref, b_ref, o_ref):
    c = c_ref[...]
    ca = c * jax.nn.sigmoid(c)
    o_ref[0] = jnp.dot(ca, w_ref[0], precision=HIGHEST, preferred_element_type=F32) + b_ref[0]


def _adaln(c, w_ada, b_ada):
    depth, d, n = w_ada.shape
    b = c.shape[0]
    rows = 8
    cp = jnp.zeros((rows, d), F32).at[:b].set(c)
    tn = 1536
    out = pl.pallas_call(
        _adaln_kernel,
        grid=(depth, n // tn),
        in_specs=[
            pl.BlockSpec((rows, d), lambda l, j: (0, 0)),
            pl.BlockSpec((1, d, tn), lambda l, j: (l, 0, j)),
            pl.BlockSpec((1, 1, tn), lambda l, j: (l, 0, j)),
        ],
        out_specs=pl.BlockSpec((1, rows, tn), lambda l, j: (l, 0, j)),
        out_shape=jax.ShapeDtypeStruct((depth, rows, n), F32),
        compiler_params=_params("arbitrary", "arbitrary"),
        name="adaln",
    )(cp, w_ada, b_ada.reshape(depth, 1, n))
    return out[:, :b]


def _ln_matmul_kernel(x_ref, sh_ref, sc_ref, w_ref, o_ref, h_ref):
    @pl.when(pl.program_id(2) == 0)
    def _():
        h_ref[...] = _modulate(x_ref[0], sh_ref[0], sc_ref[0]).astype(BF16)

    o_ref[0] = jnp.dot(h_ref[...], w_ref[...], preferred_element_type=F32).astype(o_ref.dtype)


def _ln_matmul(x, sh, sc, w, tm, tn):
    b, s, d = x.shape
    n = w.shape[1]
    return pl.pallas_call(
        _ln_matmul_kernel,
        grid=(b, s // tm, n // tn),
        in_specs=[
            pl.BlockSpec((1, tm, d), lambda bi, i, j: (bi, i, 0)),
            pl.BlockSpec((1, 1, d), lambda bi, i, j: (bi, 0, 0)),
            pl.BlockSpec((1, 1, d), lambda bi, i, j: (bi, 0, 0)),
            pl.BlockSpec((d, tn), lambda bi, i, j: (0, j)),
        ],
        out_specs=pl.BlockSpec((1, tm, tn), lambda bi, i, j: (bi, i, j)),
        out_shape=jax.ShapeDtypeStruct((b, s, n), BF16),
        scratch_shapes=[pltpu.VMEM((tm, d), BF16)],
        compiler_params=_params("arbitrary", "arbitrary", "arbitrary"),
        name="ln_matmul",
    )(x, sh, sc, w)


def _na_kernel(q_ref, k_ref, v_ref, t_ref, o_ref, *, rows):
    win = NA_WIN_H * GRID_W
    lane = lax.broadcasted_iota(jnp.int32, (GRID_W, LANES), 1)
    first = lane < (LANES // 2)
    scale = (LANES // 2) ** -0.5

    def body(i, carry):
        rr = [i * NA_ROWS_PER_STEP + j for j in range(NA_ROWS_PER_STEP)]
        rss = [jnp.clip(r - NA_WIN_H // 2, 0, rows - NA_WIN_H) for r in rr]
        ss = []
        for r, rs in zip(rr, rss):
            q = q_ref[0, pl.ds(pl.multiple_of(r * GRID_W, GRID_W), GRID_W), :]
            zero = jnp.zeros_like(q)
            q2 = jnp.concatenate([jnp.where(first, q, zero), jnp.where(first, zero, q)], axis=0)
            kw = k_ref[0, pl.ds(pl.multiple_of(rs * GRID_W, GRID_W), win), :]
            s = lax.dot_general(q2, kw, (((1,), (1,)), ((), ())), preferred_element_type=F32)
            ss.append(s * scale + t_ref[0, r - rs])
        ps, ls = [], []
        for s in ss:
            p = jnp.exp(s - jnp.max(s, axis=-1, keepdims=True))
            ls.append(jnp.sum(p, axis=-1, keepdims=True))
            ps.append(p.astype(BF16))
        for r, rs, p, l in zip(rr, rss, ps, ls):
            vw = v_ref[0, pl.ds(pl.multiple_of(rs * GRID_W, GRID_W), win), :]
            o = jnp.dot(p, vw, preferred_element_type=F32) / l
            o = jnp.where(first, o[:GRID_W], o[GRID_W:])
            o_ref[0, pl.ds(pl.multiple_of(r * GRID_W, GRID_W), GRID_W), :] = o.astype(o_ref.dtype)
        return carry

    lax.fori_loop(0, rows // NA_ROWS_PER_STEP, body, 0)


def _na_bias_table(rpb):
    h, ndr, ndc = rpb.shape
    qc = jnp.arange(GRID_W)
    kc = jnp.arange(GRID_W)
    cs = jnp.clip(qc - NA_WIN_W // 2, 0, GRID_W - NA_WIN_W)
    inwin = (kc[None, :] >= cs[:, None]) & (kc[None, :] < cs[:, None] + NA_WIN_W)
    period = 2 * GRID_W
    u = jnp.concatenate([rpb[:, :, NA_WIN_W - 1:], jnp.zeros((h, ndr, period - ndc), F32),
                         rpb[:, :, :NA_WIN_W - 1]], axis=-1)
    flat = jnp.tile(u, (1, 1, GRID_W))[:, :, :GRID_W * (period - 1)]
    col = flat.reshape(h, ndr, GRID_W, period - 1)[:, :, :, :GRID_W]
    t = jnp.stack([col[:, NA_WIN_H - 1 - p:2 * NA_WIN_H - 1 - p] for p in range(NA_WIN_H)], axis=1)
    t = jnp.where(inwin[None, None, None], t, NEG_BIG)
    t = t.transpose(0, 1, 3, 2, 4).reshape(h, NA_WIN_H, GRID_W, NA_WIN_H * GRID_W)
    t = t.reshape(h // 2, 2, NA_WIN_H, GRID_W, NA_WIN_H * GRID_W).transpose(0, 2, 1, 3, 4)
    return t.reshape(h // 2, NA_WIN_H, 2 * GRID_W, NA_WIN_H * GRID_W)


def _neighborhood_attention(qkv, rpb):
    b, s, d3 = qkv.shape
    d = d3 // 3
    rows = s // GRID_W
    assert rows >= NA_WIN_H and d // NA_HEADS == LANES // 2 and rows % NA_ROWS_PER_STEP == 0
    nslab = d // LANES
    table = _na_bias_table(rpb.astype(F32))
    win = NA_WIN_H * GRID_W
    return pl.pallas_call(
        functools.partial(_na_kernel, rows=rows),
        grid=(b, nslab),
        in_specs=[
            pl.BlockSpec((1, s, LANES), lambda bi, hp: (bi, 0, hp)),
            pl.BlockSpec((1, s, LANES), lambda bi, hp: (bi, 0, nslab + hp)),
            pl.BlockSpec((1, s, LANES), lambda bi, hp: (bi, 0, 2 * nslab + hp)),
            pl.BlockSpec((1, NA_WIN_H, 2 * GRID_W, win), lambda bi, hp: (hp, 0, 0, 0)),
        ],
        out_specs=pl.BlockSpec((1, s, LANES), lambda bi, hp: (bi, 0, hp)),
        out_shape=jax.ShapeDtypeStruct((b, s, d), BF16),
        compiler_params=_params("arbitrary", "arbitrary"),
        name="na_attn",
    )(qkv, qkv, qkv, table)


def _proj_res_kernel(a_ref, w_ref, x_ref, g_ref, o_ref, *, feature_major):
    a = a_ref[0]
    if feature_major:
        y = lax.dot_general(a, w_ref[...], (((0,), (0,)), ((), ())), preferred_element_type=F32)
    else:
        y = jnp.dot(a, w_ref[...], preferred_element_type=F32)
    o_ref[0] = x_ref[0] + g_ref[0] * y


def _proj_res(a, w, x, g, tm, feature_major):
    b, s, d = x.shape
    k = w.shape[0]
    if feature_major:
        a_spec = pl.BlockSpec((1, k, tm), lambda bi, i: (bi, 0, i))
    else:
        a_spec = pl.BlockSpec((1, tm, k), lambda bi, i: (bi, i, 0))
    return pl.pallas_call(
        functools.partial(_proj_res_kernel, feature_major=feature_major),
        grid=(b, s // tm),
        in_specs=[
            a_spec,
            pl.BlockSpec((k, d), lambda bi, i: (0, 0)),
            pl.BlockSpec((1, tm, d), lambda bi, i: (bi, i, 0)),
            pl.BlockSpec((1, 1, d), lambda bi, i: (bi, 0, 0)),
        ],
        out_specs=pl.BlockSpec((1, tm, d), lambda bi, i: (bi, i, 0)),
        out_shape=jax.ShapeDtypeStruct((b, s, d), F32),
        compiler_params=_params("arbitrary", "arbitrary"),
        name="proj_res",
    )(a, w, x, g)


def _swiglu_tile(h, wgu, wd, f):
    acc = None
    for c in range(f // MXU_DIM):
        lo = c * MXU_DIM
        g = jnp.dot(h, wgu[:, lo:lo + MXU_DIM], preferred_element_type=F32)
        u = jnp.dot(h, wgu[:, f + lo:f + lo + MXU_DIM], preferred_element_type=F32)
        a = (g * jax.nn.sigmoid(g) * u).astype(BF16)
        y = jnp.dot(a, wd[lo:lo + MXU_DIM, :], preferred_element_type=F32)
        acc = y if acc is None else acc + y
    return acc


def _ffn_kernel(x_ref, sh_ref, sc_ref, g_ref, wgu_ref, wd_ref, o_ref, *, f):
    x = x_ref[0]
    h = _modulate(x, sh_ref[0], sc_ref[0]).astype(BF16)
    o_ref[0] = x + g_ref[0] * _swiglu_tile(h, wgu_ref, wd_ref, f)


def _ffn(x, sh, sc, g, wgu, wd, tm):
    b, s, d = x.shape
    f = wd.shape[0]
    vec = pl.BlockSpec((1, 1, d), lambda bi, i: (bi, 0, 0))
    return pl.pallas_call(
        functools.partial(_ffn_kernel, f=f),
        grid=(b, s // tm),
        in_specs=[
            pl.BlockSpec((1, tm, d), lambda bi, i: (bi, i, 0)),
            vec, vec, vec,
            pl.BlockSpec((d, 2 * f), lambda bi, i: (0, 0)),
            pl.BlockSpec((f, d), lambda bi, i: (0, 0)),
        ],
        out_specs=pl.BlockSpec((1, tm, d), lambda bi, i: (bi, i, 0)),
        out_shape=jax.ShapeDtypeStruct((b, s, d), F32),
        compiler_params=_params("arbitrary", "arbitrary"),
        name="ffn",
    )(x, sh, sc, g, wgu, wd)


def _mla_proj_kernel(x_ref, sh_ref, sc_ref, wd_ref, qn_ref, kvn_ref, wq_ref, wk_ref, wv_ref,
                     cs_ref, sn_ref, cst_ref, snt_ref, q_ref, k_ref, v_ref, *, q_lora, kv_lora, qscale):
    h = _modulate(x_ref[0], sh_ref[0], sc_ref[0]).astype(BF16)
    down = jnp.dot(h, wd_ref[...], preferred_element_type=F32)
    cq = (_rms(down[:, :q_lora]) * qn_ref[...]).astype(BF16)
    ckv = (_rms(down[:, q_lora:q_lora + kv_lora]) * kvn_ref[...]).astype(BF16)
    r0 = q_lora + kv_lora
    kr = down[:, r0:r0 + LANES] * cs_ref[...] + down[:, r0 + LANES:r0 + 2 * LANES] * sn_ref[...]
    z = jnp.concatenate([ckv, kr.astype(BF16)], axis=1)
    k_ref[0] = jnp.dot(z, wk_ref[...], preferred_element_type=F32).astype(BF16)
    nt = (((1,), (1,)), ((), ()))
    v_ref[0] = lax.dot_general(wv_ref[...], ckv, nt, preferred_element_type=F32).astype(BF16)
    qt = lax.dot_general(wq_ref[...], cq, nt, preferred_element_type=F32)
    cst = cst_ref[...]
    snt = snt_ref[...]
    for hd in range(MLA_HEADS):
        base = hd * LANES
        blk = qt[base:base + LANES]
        rot = blk[MLA_NOPE:MLA_NOPE + MLA_ROPE] * cst + blk[MLA_NOPE + MLA_ROPE:] * snt
        q_ref[0, base:base + MLA_NOPE, :] = (blk[:MLA_NOPE] * qscale).astype(BF16)
        q_ref[0, base + MLA_NOPE:base + MLA_NOPE + MLA_ROPE, :] = (rot * qscale).astype(BF16)
        q_ref[0, base + MLA_NOPE + MLA_ROPE:base + LANES, :] = jnp.zeros(
            (LANES - MLA_NOPE - MLA_ROPE, rot.shape[1]), BF16)


def _rope_perm_weights(w_rope):
    x1 = w_rope[:, 0::2]
    x2 = w_rope[:, 1::2]
    return jnp.concatenate([x1, x2], axis=1), jnp.concatenate([-x2, x1], axis=1)


def _mla_proj(x, sh, sc, w_down, q_norm, w_uq, kv_norm, w_ukv, tm):
    b, s, d = x.shape
    q_lora = q_norm.shape[0]
    kv_lora = kv_norm.shape[0]
    hd_q = MLA_NOPE + MLA_ROPE
    assert hd_q <= LANES and MLA_ROPE % 2 == 0 and q_lora % LANES == 0 and kv_lora == LANES

    r_a, r_b = _rope_perm_weights(w_down[:, q_lora + kv_lora:])
    pad = jnp.zeros((d, LANES - MLA_ROPE), F32)
    wd_ext = jnp.concatenate([w_down[:, :q_lora + kv_lora], r_a, pad, r_b, pad], axis=1).astype(BF16)

    wq = w_uq.reshape(q_lora, MLA_HEADS, hd_q)
    blocks = []
    for hd in range(MLA_HEADS):
        ra, rb = _rope_perm_weights(wq[:, hd, MLA_NOPE:])
        blocks.append(jnp.concatenate([wq[:, hd, :MLA_NOPE], ra, rb], axis=1))
    wq_ext_t = jnp.concatenate(blocks, axis=1).T.astype(BF16)

    wkv = w_ukv.reshape(kv_lora, MLA_HEADS, MLA_NOPE + MLA_V)
    eye = jnp.eye(LANES, MLA_ROPE, dtype=F32)
    kblocks = []
    for hd in range(MLA_HEADS):
        top = jnp.concatenate([wkv[:, hd, :MLA_NOPE], jnp.zeros((kv_lora, LANES - MLA_NOPE), F32)], axis=1)
        bot = jnp.concatenate([jnp.zeros((LANES, MLA_NOPE), F32), eye,
                               jnp.zeros((LANES, LANES - MLA_NOPE - MLA_ROPE), F32)], axis=1)
        kblocks.append(jnp.concatenate([top, bot], axis=0))
    wk_ext = jnp.concatenate(kblocks, axis=1).astype(BF16)
    wv_t = wkv[:, :, MLA_NOPE:].reshape(kv_lora, MLA_HEADS * MLA_V).T.astype(BF16)

    t = jnp.arange(s)
    row = (t // GRID_W).astype(F32)
    col = (t % GRID_W).astype(F32)
    nf = MLA_ROPE // 4
    inv = ROPE_THETA ** (-jnp.arange(nf, dtype=F32) / nf)
    ang = jnp.concatenate([row[:, None] * inv, col[:, None] * inv], axis=-1)
    cos2 = jnp.concatenate([jnp.cos(ang), jnp.cos(ang)], axis=1)
    sin2 = jnp.concatenate([jnp.sin(ang), jnp.sin(ang)], axis=1)
    lpad = jnp.zeros((s, LANES - MLA_ROPE), F32)
    cos_tok = jnp.concatenate([cos2, lpad], axis=1)
    sin_tok = jnp.concatenate([sin2, lpad], axis=1)
    cos_t = cos2.T
    sin_t = sin2.T

    qscale = float(hd_q ** -0.5 * 1.4426950408889634)
    nq = MLA_HEADS * LANES
    nv = MLA_HEADS * MLA_V
    nd = wd_ext.shape[1]
    full = lambda shape: pl.BlockSpec(shape, lambda bi, i: (0,) * len(shape))
    vec = pl.BlockSpec((1, 1, d), lambda bi, i: (bi, 0, 0))
    return pl.pallas_call(
        functools.partial(_mla_proj_kernel, q_lora=q_lora, kv_lora=kv_lora, qscale=qscale),
        grid=(b, s // tm),
        in_specs=[
            pl.BlockSpec((1, tm, d), lambda bi, i: (bi, i, 0)),
            vec, vec,
            full((d, nd)), full((1, q_lora)), full((1, kv_lora)),
            full((nq, q_lora)), full((2 * LANES, nq)), full((nv, kv_lora)),
            pl.BlockSpec((tm, LANES), lambda bi, i: (i, 0)),
            pl.BlockSpec((tm, LANES), lambda bi, i: (i, 0)),
            pl.BlockSpec((MLA_ROPE, tm), lambda bi, i: (0, i)),
            pl.BlockSpec((MLA_ROPE, tm), lambda bi, i: (0, i)),
        ],
        out_specs=[
            pl.BlockSpec((1, nq, tm), lambda bi, i: (bi, 0, i)),
            pl.BlockSpec((1, tm, nq), lambda bi, i: (bi, i, 0)),
            pl.BlockSpec((1, nv, tm), lambda bi, i: (bi, 0, i)),
        ],
        out_shape=[
            jax.ShapeDtypeStruct((b, nq, s), BF16),
            jax.ShapeDtypeStruct((b, s, nq), BF16),
            jax.ShapeDtypeStruct((b, nv, s), BF16),
        ],
        compiler_params=_params("arbitrary", "arbitrary"),
        name="mla_proj",
    )(x, sh, sc, wd_ext, q_norm.reshape(1, -1), kv_norm.reshape(1, -1), wq_ext_t, wk_ext, wv_t,
      cos_tok, sin_tok, cos_t, sin_t)


def _mla_attn_kernel(q_ref, k_ref, v_ref, o_ref, s_ref, *, tk, nkt):
    qt = q_ref[0]
    tq = qt.shape[1]

    ones = jnp.ones((SUM_ROWS, tk), BF16)

    def stage_qk(t, slot):
        off = pl.multiple_of(t * tk, tk)
        s_ref[slot] = jnp.dot(k_ref[0, pl.ds(off, tk), :], qt, preferred_element_type=F32)

    def stage_softmax_pv(t, slot, carry):
        m, acc = carry
        st = s_ref[slot]
        m_new = jnp.maximum(m, jnp.max(st, axis=0, keepdims=True))
        p = jnp.exp2(st - m_new).astype(BF16)
        vb = v_ref[0, :, pl.ds(pl.multiple_of(t * tk, tk), tk)]
        vb = jnp.concatenate([vb, ones], axis=0)
        acc = jnp.exp2(m - m_new) * acc + jnp.dot(vb, p, preferred_element_type=F32)
        return m_new, acc

    stage_qk(0, 0)

    def body(i, carry):
        t = 2 * i
        stage_qk(t + 1, 1)
        carry = stage_softmax_pv(t, 0, carry)
        stage_qk(t + 2, 0)
        return stage_softmax_pv(t + 1, 1, carry)

    carry = (jnp.full((1, tq), NEG_BIG, F32), jnp.zeros((MLA_V + SUM_ROWS, tq), F32))
    carry = lax.fori_loop(0, nkt // 2 - 1, body, carry)
    stage_qk(nkt - 1, 1)
    carry = stage_softmax_pv(nkt - 2, 0, carry)
    m, acc = stage_softmax_pv(nkt - 1, 1, carry)
    o_ref[0] = (acc[:MLA_V] / acc[MLA_V:MLA_V + 1]).astype(o_ref.dtype)


def _mla_attention(q_t, k, v_t, tq, tk):
    b, nq, s = q_t.shape
    nv = v_t.shape[1]
    assert (s // tk) % 2 == 0
    return pl.pallas_call(
        functools.partial(_mla_attn_kernel, tk=tk, nkt=s // tk),
        grid=(b, MLA_HEADS, s // tq),
        in_specs=[
            pl.BlockSpec((1, LANES, tq), lambda bi, hd, i: (bi, hd, i)),
            pl.BlockSpec((1, s, LANES), lambda bi, hd, i: (bi, 0, hd)),
            pl.BlockSpec((1, MLA_V, s), lambda bi, hd, i: (bi, hd, 0)),
        ],
        out_specs=pl.BlockSpec((1, MLA_V, tq), lambda bi, hd, i: (bi, hd, i)),
        out_shape=jax.ShapeDtypeStruct((b, nv, s), BF16),
        scratch_shapes=[pltpu.VMEM((2, tk, tq), F32)],
        compiler_params=_params("arbitrary", "arbitrary", "arbitrary"),
        name="mla_attn",
    )(q_t, k, v_t)


def _router_kernel(x_ref, sh_ref, sc_ref, wr_ref, h_ref, i_ref, w_ref, *, n_exp):
    h = _modulate(x_ref[0], sh_ref[0], sc_ref[0])
    h_ref[0] = h
    logits = jnp.dot(h, wr_ref[...], precision=HIGHEST, preferred_element_type=F32)
    lane = lax.broadcasted_iota(jnp.int32, logits.shape, 1)
    valid = lane < n_exp
    logits = jnp.where(valid, logits, NEG_BIG)
    e = jnp.exp(logits - jnp.max(logits, axis=-1, keepdims=True))
    probs = e / jnp.sum(e, axis=-1, keepdims=True)
    probs = jnp.where(valid, probs, -1.0)
    vals, idxs = [], []
    rest = probs
    for _ in range(TOP_K):
        v = jnp.max(rest, axis=-1, keepdims=True)
        idx = jnp.min(jnp.where(rest == v, lane, LANES), axis=-1, keepdims=True)
        vals.append(v)
        idxs.append(idx)
        rest = jnp.where(lane == idx, -1.0, rest)
    tot = vals[0]
    for v in vals[1:]:
        tot = tot + v
    i_out = jnp.zeros(probs.shape, jnp.int32)
    w_out = jnp.zeros_like(probs)
    for k in range(TOP_K):
        i_out = jnp.where(lane == k, idxs[k], i_out)
        w_out = jnp.where(lane == k, vals[k] / tot, w_out)
    i_ref[0] = i_out
    w_ref[0] = w_out


def _router(x, sh, sc, w_router, tm):
    b, s, d = x.shape
    n_exp = w_router.shape[1]
    wr = jnp.zeros((d, LANES), F32).at[:, :n_exp].set(w_router)
    vec = pl.BlockSpec((1, 1, d), lambda bi, i: (bi, 0, 0))
    lanes_spec = pl.BlockSpec((1, tm, LANES), lambda bi, i: (bi, i, 0))
    return pl.pallas_call(
        functools.partial(_router_kernel, n_exp=n_exp),
        grid=(b, s // tm),
        in_specs=[
            pl.BlockSpec((1, tm, d), lambda bi, i: (bi, i, 0)),
            vec, vec,
            pl.BlockSpec((d, LANES), lambda bi, i: (0, 0)),
        ],
        out_specs=[pl.BlockSpec((1, tm, d), lambda bi, i: (bi, i, 0)), lanes_spec, lanes_spec],
        out_shape=[jax.ShapeDtypeStruct((b, s, d), F32),
                   jax.ShapeDtypeStruct((b, s, LANES), jnp.int32),
                   jax.ShapeDtypeStruct((b, s, LANES), F32)],
        compiler_params=_params("arbitrary", "arbitrary"),
        name="router",
    )(x, sh, sc, wr)


def _route_plan(idx, wts, n_exp, tile):
    t, k = idx.shape
    e_flat = idx.reshape(t * k)
    onehot = (e_flat[:, None] == jnp.arange(n_exp, dtype=jnp.int32)[None, :]).astype(jnp.int32)
    csum = jnp.cumsum(onehot, axis=0)
    counts = csum[-1]
    rank = jnp.sum(onehot * csum, axis=1) - 1
    padded = ((counts + tile - 1) // tile) * tile
    gend = jnp.cumsum(padded)
    gstart = gend - padded
    dest = jnp.sum(onehot * gstart[None, :], axis=1) + rank
    p_rows = t * k + n_exp * tile
    src_tok = jnp.zeros((p_rows,), jnp.int32).at[dest].set(jnp.arange(t * k, dtype=jnp.int32) // k)
    w_row = jnp.zeros((p_rows,), F32).at[dest].set(wts.reshape(t * k))
    tile_start = jnp.arange(p_rows // tile, dtype=jnp.int32) * tile
    tile_expert = jnp.minimum(jnp.sum((tile_start[:, None] >= gend[None, :]).astype(jnp.int32), axis=1),
                              n_exp - 1)
    tile_live = (tile_start < gend[-1]).astype(jnp.int32)
    return dest.astype(jnp.int32), src_tok, w_row.reshape(p_rows, 1), tile_expert, tile_live


def _row_gather_kernel(idx_ref, src_ref, dst_ref, sem, *, rows):
    base = pl.program_id(0) * rows

    def row_copy(r):
        return pltpu.make_async_copy(src_ref.at[pl.ds(idx_ref[r], 1)], dst_ref.at[pl.ds(base + r, 1)], sem)

    def issue(r, carry):
        row_copy(r).start()
        return carry

    lax.fori_loop(0, rows, issue, 0, unroll=8)
    pltpu.make_async_copy(src_ref.at[pl.ds(0, rows)], dst_ref.at[pl.ds(base, rows)], sem).wait()


def _row_gather(src, src_tok, rows):
    p_rows = src_tok.shape[0]
    d = src.shape[1]
    return pl.pallas_call(
        functools.partial(_row_gather_kernel, rows=rows),
        grid=(p_rows // rows,),
        in_specs=[pl.BlockSpec((rows,), lambda i: (i,), memory_space=pltpu.SMEM),
                  pl.BlockSpec(memory_space=pl.ANY)],
        out_specs=pl.BlockSpec(memory_space=pl.ANY),
        out_shape=jax.ShapeDtypeStruct((p_rows, d), src.dtype),
        scratch_shapes=[pltpu.SemaphoreType.DMA(())],
        compiler_params=_params("arbitrary"),
        name="moe_dispatch",
    )(src_tok, src)


def _experts_kernel(te_ref, live_ref, x_ref, w_ref, wgu_ref, wd_ref, o_ref, *, f):
    i = pl.program_id(0)

    @pl.when(live_ref[i] != 0)
    def _():
        y = _swiglu_tile(x_ref[...].astype(BF16), wgu_ref.at[0], wd_ref.at[0], f)
        o_ref[...] = w_ref[...] * y

    @pl.when(live_ref[i] == 0)
    def _():
        o_ref[...] = jnp.zeros_like(o_ref)


def _experts(xs, w_row, tile_expert, tile_live, wgu, wd, tile):
    p_rows, d = xs.shape
    n_exp, f, _ = wd.shape
    grid_spec = pltpu.PrefetchScalarGridSpec(
        num_scalar_prefetch=2,
        grid=(p_rows // tile,),
        in_specs=[
            pl.BlockSpec((tile, d), lambda i, te, lv: (i, 0)),
            pl.BlockSpec((tile, 1), lambda i, te, lv: (i, 0)),
            pl.BlockSpec((1, d, 2 * f), lambda i, te, lv: (te[i], 0, 0)),
            pl.BlockSpec((1, f, d), lambda i, te, lv: (te[i], 0, 0)),
        ],
        out_specs=pl.BlockSpec((tile, d), lambda i, te, lv: (i, 0)),
    )
    return pl.pallas_call(
        functools.partial(_experts_kernel, f=f),
        grid_spec=grid_spec,
        out_shape=jax.ShapeDtypeStruct((p_rows, d), F32),
        compiler_params=_params("arbitrary"),
        name="moe_experts",
    )(tile_expert, tile_live, xs, w_row, wgu, wd)


def _combine_kernel(dest_ref, x_ref, g_ref, fn_ref, ys_ref, o_ref, buf_ref, sem, *, tm):
    def row_copy(r, k):
        return pltpu.make_async_copy(ys_ref.at[pl.ds(dest_ref[TOP_K * r + k], 1)],
                                     buf_ref.at[k, pl.ds(r, 1)], sem)

    def issue(r, carry):
        for k in range(TOP_K):
            row_copy(r, k).start()
        return carry

    lax.fori_loop(0, tm, issue, 0, unroll=4)
    for k in range(TOP_K):
        pltpu.make_async_copy(ys_ref.at[pl.ds(0, tm)], buf_ref.at[k], sem).wait()
    y = buf_ref[0]
    for k in range(1, TOP_K):
        y = y + buf_ref[k]
    o_ref[...] = _rms(x_ref[...] + g_ref[0] * y) * fn_ref[...]


def _combine(x, g, final_norm, ys, dest, tm):
    b, s, d = x.shape
    tiles_per_batch = s // tm
    return pl.pallas_call(
        functools.partial(_combine_kernel, tm=tm),
        grid=(b * tiles_per_batch,),
        in_specs=[
            pl.BlockSpec((TOP_K * tm,), lambda i: (i,), memory_space=pltpu.SMEM),
            pl.BlockSpec((tm, d), lambda i: (i, 0)),
            pl.BlockSpec((1, 1, d), lambda i: (i // tiles_per_batch, 0, 0)),
            pl.BlockSpec((1, d), lambda i: (0, 0)),
            pl.BlockSpec(memory_space=pl.ANY),
        ],
        out_specs=pl.BlockSpec((tm, d), lambda i: (i, 0)),
        out_shape=jax.ShapeDtypeStruct((b * s, d), F32),
        scratch_shapes=[pltpu.VMEM((TOP_K, tm, d), F32), pltpu.SemaphoreType.DMA(())],
        compiler_params=_params("arbitrary"),
        name="moe_combine",
    )(dest, x.reshape(b * s, d), g, final_norm.reshape(1, d), ys).reshape(b, s, d)


def kernel(x, c, w_ada, b_ada, na_w_qkv, na_rpb, na_w_o, ffn_w_gu, ffn_w_down, mla_w_down, mla_q_norm,
           mla_w_uq, mla_kv_norm, mla_w_ukv, mla_w_o, moe_w_router, moe_w_gu, moe_w_down, final_norm):
    b, s, d = x.shape
    mods = _adaln(c, w_ada, b_ada)

    def split(layer):
        return [v.reshape(b, 1, d) for v in jnp.split(mods[layer], 6, axis=-1)]

    tm = min(1024, s)
    tmh = min(512, s)

    sh_a, sc_a, g_a, sh_f, sc_f, g_f = split(0)
    qkv = _ln_matmul(x, sh_a, sc_a, na_w_qkv[0].astype(BF16), tm, 1024)
    att = _neighborhood_attention(qkv, na_rpb[0])
    x = _proj_res(att, na_w_o[0].astype(BF16), x, g_a, tm, feature_major=False)
    x = _ffn(x, sh_f, sc_f, g_f, ffn_w_gu[0].astype(BF16), ffn_w_down[0].astype(BF16), tmh)

    sh_a, sc_a, g_a, sh_f, sc_f, g_f = split(1)
    q_t, k, v_t = _mla_proj(x, sh_a, sc_a, mla_w_down[0], mla_q_norm[0], mla_w_uq[0], mla_kv_norm[0],
                            mla_w_ukv[0], tmh)
    o_t = _mla_attention(q_t, k, v_t, min(512, s), min(1024, s // 2))
    x = _proj_res(o_t, mla_w_o[0].astype(BF16), x, g_a, tm, feature_major=True)
    h, idx, wts = _router(x, sh_f, sc_f, moe_w_router[0], tm)
    n_exp = moe_w_router.shape[-1]
    dest, src_tok, w_row, tile_expert, tile_live = _route_plan(
        idx[..., :TOP_K].reshape(b * s, TOP_K), wts[..., :TOP_K].reshape(b * s, TOP_K), n_exp, tmh)
    xs = _row_gather(h.reshape(b * s, d), src_tok, MOE_GATHER_ROWS)
    ys = _experts(xs, w_row, tile_expert, tile_live, moe_w_gu[0].astype(BF16), moe_w_down[0].astype(BF16), tmh)
    return _combine(x, g_f, final_norm, ys, dest, tmh)
```

```python
import functools

import jax
import jax.numpy as jnp
from jax import lax
from jax.experimental import pallas as pl
from jax.experimental.pallas import tpu as pltpu

F32 = jnp.float32
BF16 = jnp.bfloat16
HIGHEST = lax.Precision.HIGHEST

GRID_W = 64
NA_HEADS = 16
NA_WIN_H = 8
NA_WIN_W = 16
NA_ROWS_PER_STEP = 4
MLA_HEADS = 16
MLA_NOPE = 64
MLA_ROPE = 32
MLA_V = 64
ROPE_THETA = 10000.0
TOP_K = 2
NORM_EPS = 1e-6

LANES = 128
MXU_DIM = 256
SUM_ROWS = 16
VMEM_LIMIT = 56 * 1024 * 1024
NEG_BIG = -1e30


def _params(*sem):
    return pltpu.CompilerParams(dimension_semantics=sem, vmem_limit_bytes=VMEM_LIMIT)


def _rms(x):
    return x * lax.rsqrt(jnp.mean(x * x, axis=-1, keepdims=True) + NORM_EPS)


def _modulate(x, sh, sc):
    return _rms(x) * (1.0 + sc) + sh


def _adaln_kernel(c_ref, w_ref, b_ref, o_ref):
    c = c_ref[...]
    ca = c * jax.nn.sigmoid(c)
    o_ref[0] = jnp.dot(ca, w_ref[0], precision=HIGHEST, preferred_element_type=F32) + b_ref[0]


def _adaln(c, w_ada, b_ada):
    depth, d, n = w_ada.shape
    b = c.shape[0]
    rows = 8
    cp = jnp.zeros((rows, d), F32).at[:b].set(c)
    tn = 1536
    out = pl.pallas_call(
        _adaln_kernel,
        grid=(depth, n // tn),
        in_specs=[
            pl.BlockSpec((rows, d), lambda l, j: (0, 0)),
            pl.BlockSpec((1, d, tn), lambda l, j: (l, 0, j)),
            pl.BlockSpec((1, 1, tn), lambda l, j: (l, 0, j)),
        ],
        out_specs=pl.BlockSpec((1, rows, tn), lambda l, j: (l, 0, j)),
        out_shape=jax.ShapeDtypeStruct((depth, rows, n), F32),
        compiler_params=_params("arbitrary", "arbitrary"),
        name="adaln",
    )(cp, w_ada, b_ada.reshape(depth, 1, n))
    return out[:, :b]


def _ln_matmul_kernel(x_ref, sh_ref, sc_ref, w_ref, o_ref, h_ref):
    @pl.when(pl.program_id(2) == 0)
    def _():
        h_ref[...] = _modulate(x_ref[0], sh_ref[0], sc_ref[0]).astype(BF16)

    o_ref[0] = jnp.dot(h_ref[...], w_ref[...], preferred_element_type=F32).astype(o_ref.dtype)


def _ln_matmul(x, sh, sc, w, tm, tn):
    b, s, d = x.shape
    n = w.shape[1]
    return pl.pallas_call(
        _ln_matmul_kernel,
        grid=(b, s // tm, n // tn),
        in_specs=[
            pl.BlockSpec((1, tm, d), lambda bi, i, j: (bi, i, 0)),
            pl.BlockSpec((1, 1, d), lambda bi, i, j: (bi, 0, 0)),
            pl.BlockSpec((1, 1, d), lambda bi, i, j: (bi, 0, 0)),
            pl.BlockSpec((d, tn), lambda bi, i, j: (0, j)),
        ],
        out_specs=pl.BlockSpec((1, tm, tn), lambda bi, i, j: (bi, i, j)),
        out_shape=jax.ShapeDtypeStruct((b, s, n), BF16),
        scratch_shapes=[pltpu.VMEM((tm, d), BF16)],
        compiler_params=_params("arbitrary", "arbitrary", "arbitrary"),
        name="ln_matmul",
    )(x, sh, sc, w)


def _na_kernel(q_ref, k_ref, v_ref, t_ref, o_ref, *, rows):
    win = NA_WIN_H * GRID_W
    lane = lax.broadcasted_iota(jnp.int32, (GRID_W, LANES), 1)
    first = lane < (LANES // 2)
    scale = (LANES // 2) ** -0.5

    def body(i, carry):
        rr = [i * NA_ROWS_PER_STEP + j for j in range(NA_ROWS_PER_STEP)]
        rss = [jnp.clip(r - NA_WIN_H // 2, 0, rows - NA_WIN_H) for r in rr]
        ss = []
        for r, rs in zip(rr, rss):
            q = q_ref[0, pl.ds(pl.multiple_of(r * GRID_W, GRID_W), GRID_W), :]
            zero = jnp.zeros_like(q)
            q2 = jnp.concatenate([jnp.where(first, q, zero), jnp.where(first, zero, q)], axis=0)
            kw = k_ref[0, pl.ds(pl.multiple_of(rs * GRID_W, GRID_W), win), :]
            s = lax.dot_general(q2, kw, (((1,), (1,)), ((), ())), preferred_element_type=F32)
            ss.append(s * scale + t_ref[0, r - rs])
        ps, ls = [], []
        for s in ss:
            p = jnp.exp(s - jnp.max(s, axis=-1, keepdims=True))
            ls.append(jnp.sum(p, axis=-1, keepdims=True))
            ps.append(p.astype(BF16))
        for r, rs, p, l in zip(rr, rss, ps, ls):
            vw = v_ref[0, pl.ds(pl.multiple_of(rs * GRID_W, GRID_W), win), :]
            o = jnp.dot(p, vw, preferred_element_type=F32) / l
            o = jnp.where(first, o[:GRID_W], o[GRID_W:])
            o_ref[0, pl.ds(pl.multiple_of(r * GRID_W, GRID_W), GRID_W), :] = o.astype(o_ref.dtype)
        return carry

    lax.fori_loop(0, rows // NA_ROWS_PER_STEP, body, 0)


def _na_bias_table(rpb):
    h, ndr, ndc = rpb.shape
    qc = jnp.arange(GRID_W)
    kc = jnp.arange(GRID_W)
    cs = jnp.clip(qc - NA_WIN_W // 2, 0, GRID_W - NA_WIN_W)
    inwin = (kc[None, :] >= cs[:, None]) & (kc[None, :] < cs[:, None] + NA_WIN_W)
    period = 2 * GRID_W
    u = jnp.concatenate([rpb[:, :, NA_WIN_W - 1:], jnp.zeros((h, ndr, period - ndc), F32),
                         rpb[:, :, :NA_WIN_W - 1]], axis=-1)
    flat = jnp.tile(u, (1, 1, GRID_W))[:, :, :GRID_W * (period - 1)]
    col = flat.reshape(h, ndr, GRID_W, period - 1)[:, :, :, :GRID_W]
    t = jnp.stack([col[:, NA_WIN_H - 1 - p:2 * NA_WIN_H - 1 - p] for p in range(NA_WIN_H)], axis=1)
    t = jnp.where(inwin[None, None, None], t, NEG_BIG)
    t = t.transpose(0, 1, 3, 2, 4).reshape(h, NA_WIN_H, GRID_W, NA_WIN_H * GRID_W)
    t = t.reshape(h // 2, 2, NA_WIN_H, GRID_W, NA_WIN_H * GRID_W).transpose(0, 2, 1, 3, 4)
    return t.reshape(h // 2, NA_WIN_H, 2 * GRID_W, NA_WIN_H * GRID_W)


def _neighborhood_attention(qkv, rpb):
    b, s, d3 = qkv.shape
    d = d3 // 3
    rows = s // GRID_W
    assert rows >= NA_WIN_H and d // NA_HEADS == LANES // 2 and rows % NA_ROWS_PER_STEP == 0
    nslab = d // LANES
    table = _na_bias_table(rpb.astype(F32))
    win = NA_WIN_H * GRID_W
    return pl.pallas_call(
        functools.partial(_na_kernel, rows=rows),
        grid=(b, nslab),
        in_specs=[
            pl.BlockSpec((1, s, LANES), lambda bi, hp: (bi, 0, hp)),
            pl.BlockSpec((1, s, LANES), lambda bi, hp: (bi, 0, nslab + hp)),
            pl.BlockSpec((1, s, LANES), lambda bi, hp: (bi, 0, 2 * nslab + hp)),
            pl.BlockSpec((1, NA_WIN_H, 2 * GRID_W, win), lambda bi, hp: (hp, 0, 0, 0)),
        ],
        out_specs=pl.BlockSpec((1, s, LANES), lambda bi, hp: (bi, 0, hp)),
        out_shape=jax.ShapeDtypeStruct((b, s, d), BF16),
        compiler_params=_params("arbitrary", "arbitrary"),
        name="na_attn",
    )(qkv, qkv, qkv, table)


def _proj_res_kernel(a_ref, w_ref, x_ref, g_ref, o_ref, *, feature_major):
    a = a_ref[0]
    if feature_major:
        y = lax.dot_general(a, w_ref[...], (((0,), (0,)), ((), ())), preferred_element_type=F32)
    else:
        y = jnp.dot(a, w_ref[...], preferred_element_type=F32)
    o_ref[0] = x_ref[0] + g_ref[0] * y


def _proj_res(a, w, x, g, tm, feature_major):
    b, s, d = x.shape
    k = w.shape[0]
    if feature_major:
        a_spec = pl.BlockSpec((1, k, tm), lambda bi, i: (bi, 0, i))
    else:
        a_spec = pl.BlockSpec((1, tm, k), lambda bi, i: (bi, i, 0))
    return pl.pallas_call(
        functools.partial(_proj_res_kernel, feature_major=feature_major),
        grid=(b, s // tm),
        in_specs=[
            a_spec,
            pl.BlockSpec((k, d), lambda bi, i: (0, 0)),
            pl.BlockSpec((1, tm, d), lambda bi, i: (bi, i, 0)),
            pl.BlockSpec((1, 1, d), lambda bi, i: (bi, 0, 0)),
        ],
        out_specs=pl.BlockSpec((1, tm, d), lambda bi, i: (bi, i, 0)),
        out_shape=jax.ShapeDtypeStruct((b, s, d), F32),
        compiler_params=_params("arbitrary", "arbitrary"),
        name="proj_res",
    )(a, w, x, g)


def _swiglu_tile(h, wgu, wd, f):
    acc = None
    for c in range(f // MXU_DIM):
        lo = c * MXU_DIM
        g = jnp.dot(h, wgu[:, lo:lo + MXU_DIM], preferred_element_type=F32)
        u = jnp.dot(h, wgu[:, f + lo:f + lo + MXU_DIM], preferred_element_type=F32)
        a = (g * jax.nn.sigmoid(g) * u).astype(BF16)
        y = jnp.dot(a, wd[lo:lo + MXU_DIM, :], preferred_element_type=F32)
        acc = y if acc is None else acc + y
    return acc


def _ffn_kernel(x_ref, sh_ref, sc_ref, g_ref, wgu_ref, wd_ref, o_ref, *, f):
    x = x_ref[0]
    h = _modulate(x, sh_ref[0], sc_ref[0]).astype(BF16)
    o_ref[0] = x + g_ref[0] * _swiglu_tile(h, wgu_ref, wd_ref, f)


def _ffn(x, sh, sc, g, wgu, wd, tm):
    b, s, d = x.shape
    f = wd.shape[0]
    vec = pl.BlockSpec((1, 1, d), lambda bi, i: (bi, 0, 0))
    return pl.pallas_call(
        functools.partial(_ffn_kernel, f=f),
        grid=(b, s // tm),
        in_specs=[
            pl.BlockSpec((1, tm, d), lambda bi, i: (bi, i, 0)),
            vec, vec, vec,
            pl.BlockSpec((d, 2 * f), lambda bi, i: (0, 0)),
            pl.BlockSpec((f, d), lambda bi, i: (0, 0)),
        ],
        out_specs=pl.BlockSpec((1, tm, d), lambda bi, i: (bi, i, 0)),
        out_shape=jax.ShapeDtypeStruct((b, s, d), F32),
        compiler_params=_params("arbitrary", "arbitrary"),
        name="ffn",
    )(x, sh, sc, g, wgu, wd)


def _mla_proj_kernel(x_ref, sh_ref, sc_ref, wd_ref, qn_ref, kvn_ref, wq_ref, wk_ref, wv_ref,
                     cs_ref, sn_ref, cst_ref, snt_ref, q_ref, k_ref, v_ref, *, q_lora, kv_lora, qscale):
    h = _modulate(x_ref[0], sh_ref[0], sc_ref[0]).astype(BF16)
    down = jnp.dot(h, wd_ref[...], preferred_element_type=F32)
    cq = (_rms(down[:, :q_lora]) * qn_ref[...]).astype(BF16)
    ckv = (_rms(down[:, q_lora:q_lora + kv_lora]) * kvn_ref[...]).astype(BF16)
    r0 = q_lora + kv_lora
    kr = down[:, r0:r0 + LANES] * cs_ref[...] + down[:, r0 + LANES:r0 + 2 * LANES] * sn_ref[...]
    z = jnp.concatenate([ckv, kr.astype(BF16)], axis=1)
    k_ref[0] = jnp.dot(z, wk_ref[...], preferred_element_type=F32).astype(BF16)
    nt = (((1,), (1,)), ((), ()))
    v_ref[0] = lax.dot_general(wv_ref[...], ckv, nt, preferred_element_type=F32).astype(BF16)
    qt = lax.dot_general(wq_ref[...], cq, nt, preferred_element_type=F32)
    cst = cst_ref[...]
    snt = snt_ref[...]
    for hd in range(MLA_HEADS):
        base = hd * LANES
        blk = qt[base:base + LANES]
        rot = blk[MLA_NOPE:MLA_NOPE + MLA_ROPE] * cst + blk[MLA_NOPE + MLA_ROPE:] * snt
        q_ref[0, base:base + MLA_NOPE, :] = (blk[:MLA_NOPE] * qscale).astype(BF16)
        q_ref[0, base + MLA_NOPE:base + MLA_NOPE + MLA_ROPE, :] = (rot * qscale).astype(BF16)
        q_ref[0, base + MLA_NOPE + MLA_ROPE:base + LANES, :] = jnp.zeros(
            (LANES - MLA_NOPE - MLA_ROPE, rot.shape[1]), BF16)


def _rope_perm_weights(w_rope):
    x1 = w_rope[:, 0::2]
    x2 = w_rope[:, 1::2]
    return jnp.concatenate([x1, x2], axis=1), jnp.concatenate([-x2, x1], axis=1)


def _mla_proj(x, sh, sc, w_down, q_norm, w_uq, kv_norm, w_ukv, tm):
    b, s, d = x.shape
    q_lora = q_norm.shape[0]
    kv_lora = kv_norm.shape[0]
    hd_q = MLA_NOPE + MLA_ROPE
    assert hd_q <= LANES and MLA_ROPE % 2 == 0 and q_lora % LANES == 0 and kv_lora == LANES

    r_a, r_b = _rope_perm_weights(w_down[:, q_lora + kv_lora:])
    pad = jnp.zeros((d, LANES - MLA_ROPE), F32)
    wd_ext = jnp.concatenate([w_down[:, :q_lora + kv_lora], r_a, pad, r_b, pad], axis=1).astype(BF16)

    wq = w_uq.reshape(q_lora, MLA_HEADS, hd_q)
    blocks = []
    for hd in range(MLA_HEADS):
        ra, rb = _rope_perm_weights(wq[:, hd, MLA_NOPE:])
        blocks.append(jnp.concatenate([wq[:, hd, :MLA_NOPE], ra, rb], axis=1))
    wq_ext_t = jnp.concatenate(blocks, axis=1).T.astype(BF16)

    wkv = w_ukv.reshape(kv_lora, MLA_HEADS, MLA_NOPE + MLA_V)
    eye = jnp.eye(LANES, MLA_ROPE, dtype=F32)
    kblocks = []
    for hd in range(MLA_HEADS):
        top = jnp.concatenate([wkv[:, hd, :MLA_NOPE], jnp.zeros((kv_lora, LANES - MLA_NOPE), F32)], axis=1)
        bot = jnp.concatenate([jnp.zeros((LANES, MLA_NOPE), F32), eye,
                               jnp.zeros((LANES, LANES - MLA_NOPE - MLA_ROPE), F32)], axis=1)
        kblocks.append(jnp.concatenate([top, bot], axis=0))
    wk_ext = jnp.concatenate(kblocks, axis=1).astype(BF16)
    wv_t = wkv[:, :, MLA_NOPE:].reshape(kv_lora, MLA_HEADS * MLA_V).T.astype(BF16)

    t = jnp.arange(s)
    row = (t // GRID_W).astype(F32)
    col = (t % GRID_W).astype(F32)
    nf = MLA_ROPE // 4
    inv = ROPE_THETA ** (-jnp.arange(nf, dtype=F32) / nf)
    ang = jnp.concatenate([row[:, None] * inv, col[:, None] * inv], axis=-1)
    cos2 = jnp.concatenate([jnp.cos(ang), jnp.cos(ang)], axis=1)
    sin2 = jnp.concatenate([jnp.sin(ang), jnp.sin(ang)], axis=1)
    lpad = jnp.zeros((s, LANES - MLA_ROPE), F32)
    cos_tok = jnp.concatenate([cos2, lpad], axis=1)
    sin_tok = jnp.concatenate([sin2, lpad], axis=1)
    cos_t = cos2.T
    sin_t = sin2.T

    qscale = float(hd_q ** -0.5 * 1.4426950408889634)
    nq = MLA_HEADS * LANES
    nv = MLA_HEADS * MLA_V
    nd = wd_ext.shape[1]
    full = lambda shape: pl.BlockSpec(shape, lambda bi, i: (0,) * len(shape))
    vec = pl.BlockSpec((1, 1, d), lambda bi, i: (bi, 0, 0))
    return pl.pallas_call(
        functools.partial(_mla_proj_kernel, q_lora=q_lora, kv_lora=kv_lora, qscale=qscale),
        grid=(b, s // tm),
        in_specs=[
            pl.BlockSpec((1, tm, d), lambda bi, i: (bi, i, 0)),
            vec, vec,
            full((d, nd)), full((1, q_lora)), full((1, kv_lora)),
            full((nq, q_lora)), full((2 * LANES, nq)), full((nv, kv_lora)),
            pl.BlockSpec((tm, LANES), lambda bi, i: (i, 0)),
            pl.BlockSpec((tm, LANES), lambda bi, i: (i, 0)),
            pl.BlockSpec((MLA_ROPE, tm), lambda bi, i: (0, i)),
            pl.BlockSpec((MLA_ROPE, tm), lambda bi, i: (0, i)),
        ],
        out_specs=[
            pl.BlockSpec((1, nq, tm), lambda bi, i: (bi, 0, i)),
            pl.BlockSpec((1, tm, nq), lambda bi, i: (bi, i, 0)),
            pl.BlockSpec((1, nv, tm), lambda bi, i: (bi, 0, i)),
        ],
        out_shape=[
            jax.ShapeDtypeStruct((b, nq, s), BF16),
            jax.ShapeDtypeStruct((b, s, nq), BF16),
            jax.ShapeDtypeStruct((b, nv, s), BF16),
        ],
        compiler_params=_params("arbitrary", "arbitrary"),
        name="mla_proj",
    )(x, sh, sc, wd_ext, q_norm.reshape(1, -1), kv_norm.reshape(1, -1), wq_ext_t, wk_ext, wv_t,
      cos_tok, sin_tok, cos_t, sin_t)


def _mla_attn_kernel(q_ref, k_ref, v_ref, o_ref, s_ref, *, tk, nkt):
    qt = q_ref[0]
    tq = qt.shape[1]

    ones = jnp.ones((SUM_ROWS, tk), BF16)

    def stage_qk(t, slot):
        off = pl.multiple_of(t * tk, tk)
        s_ref[slot] = jnp.dot(k_ref[0, pl.ds(off, tk), :], qt, preferred_element_type=F32)

    def stage_softmax_pv(t, slot, carry):
        m, acc = carry
        st = s_ref[slot]
        m_new = jnp.maximum(m, jnp.max(st, axis=0, keepdims=True))
        p = jnp.exp2(st - m_new).astype(BF16)
        vb = v_ref[0, :, pl.ds(pl.multiple_of(t * tk, tk), tk)]
        vb = jnp.concatenate([vb, ones], axis=0)
        acc = jnp.exp2(m - m_new) * acc + jnp.dot(vb, p, preferred_element_type=F32)
        return m_new, acc

    stage_qk(0, 0)

    def body(i, carry):
        t = 2 * i
        stage_qk(t + 1, 1)
        carry = stage_softmax_pv(t, 0, carry)
        stage_qk(t + 2, 0)
        return stage_softmax_pv(t + 1, 1, carry)

    carry = (jnp.full((1, tq), NEG_BIG, F32), jnp.zeros((MLA_V + SUM_ROWS, tq), F32))
    carry = lax.fori_loop(0, nkt // 2 - 1, body, carry)
    stage_qk(nkt - 1, 1)
    carry = stage_softmax_pv(nkt - 2, 0, carry)
    m, acc = stage_softmax_pv(nkt - 1, 1, carry)
    o_ref[0] = (acc[:MLA_V] / acc[MLA_V:MLA_V + 1]).astype(o_ref.dtype)


def _mla_attention(q_t, k, v_t, tq, tk):
    b, nq, s = q_t.shape
    nv = v_t.shape[1]
    assert (s // tk) % 2 == 0
    return pl.pallas_call(
        functools.partial(_mla_attn_kernel, tk=tk, nkt=s // tk),
        grid=(b, MLA_HEADS, s // tq),
        in_specs=[
            pl.BlockSpec((1, LANES, tq), lambda bi, hd, i: (bi, hd, i)),
            pl.BlockSpec((1, s, LANES), lambda bi, hd, i: (bi, 0, hd)),
            pl.BlockSpec((1, MLA_V, s), lambda bi, hd, i: (bi, hd, 0)),
        ],
        out_specs=pl.BlockSpec((1, MLA_V, tq), lambda bi, hd, i: (bi, hd, i)),
        out_shape=jax.ShapeDtypeStruct((b, nv, s), BF16),
        scratch_shapes=[pltpu.VMEM((2, tk, tq), F32)],
        compiler_params=_params("arbitrary", "arbitrary", "arbitrary"),
        name="mla_attn",
    )(q_t, k, v_t)


def _router_kernel(x_ref, sh_ref, sc_ref, wr_ref, h_ref, i_ref, w_ref, *, n_exp):
    h = _modulate(x_ref[0], sh_ref[0], sc_ref[0])
    h_ref[0] = h
    logits = jnp.dot(h, wr_ref[...], precision=HIGHEST, preferred_element_type=F32)
    lane = lax.broadcasted_iota(jnp.int32, logits.shape, 1)
    valid = lane < n_exp
    logits = jnp.where(valid, logits, NEG_BIG)
    e = jnp.exp(logits - jnp.max(logits, axis=-1, keepdims=True))
    probs = e / jnp.sum(e, axis=-1, keepdims=True)
    probs = jnp.where(valid, probs, -1.0)
    vals, idxs = [], []
    rest = probs
    for _ in range(TOP_K):
        v = jnp.max(rest, axis=-1, keepdims=True)
        idx = jnp.min(jnp.where(rest == v, lane, LANES), axis=-1, keepdims=True)
        vals.append(v)
        idxs.append(idx)
        rest = jnp.where(lane == idx, -1.0, rest)
    tot = vals[0]
    for v in vals[1:]:
        tot = tot + v
    i_out = jnp.zeros(probs.shape, jnp.int32)
    w_out = jnp.zeros_like(probs)
    for k in range(TOP_K):
        i_out = jnp.where(lane == k, idxs[k], i_out)
        w_out = jnp.where(lane == k, vals[k] / tot, w_out)
    i_ref[0] = i_out
    w_ref[0] = w_out


def _router(x, sh, sc, w_router, tm):
    b, s, d = x.shape
    n_exp = w_router.shape[1]
    wr = jnp.zeros((d, LANES), F32).at[:, :n_exp].set(w_router)
    vec = pl.BlockSpec((1, 1, d), lambda bi, i: (bi, 0, 0))
    lanes_spec = pl.BlockSpec((1, tm, LANES), lambda bi, i: (bi, i, 0))
    return pl.pallas_call(
        functools.partial(_router_kernel, n_exp=n_exp),
        grid=(b, s // tm),
        in_specs=[
            pl.BlockSpec((1, tm, d), lambda bi, i: (bi, i, 0)),
            vec, vec,
            pl.BlockSpec((d, LANES), lambda bi, i: (0, 0)),
        ],
        out_specs=[pl.BlockSpec((1, tm, d), lambda bi, i: (bi, i, 0)), lanes_spec, lanes_spec],
        out_shape=[jax.ShapeDtypeStruct((b, s, d), F32),
                   jax.ShapeDtypeStruct((b, s, LANES), jnp.int32),
                   jax.ShapeDtypeStruct((b, s, LANES), F32)],
        compiler_params=_params("arbitrary", "arbitrary"),
        name="router",
    )(x, sh, sc, wr)


def _route_plan(idx, n_exp, tile):
    t, k = idx.shape
    e_flat = idx.reshape(t * k)
    onehot = (e_flat[:, None] == jnp.arange(n_exp, dtype=jnp.int32)[None, :]).astype(jnp.int32)
    csum = jnp.cumsum(onehot, axis=0)
    counts = csum[-1]
    rank = jnp.sum(onehot * csum, axis=1) - 1
    padded = ((counts + tile - 1) // tile) * tile
    gend = jnp.cumsum(padded)
    gstart = gend - padded
    dest = jnp.sum(onehot * gstart[None, :], axis=1) + rank
    p_rows = t * k + n_exp * tile
    src_tok = jnp.zeros((p_rows,), jnp.int32).at[dest].set(jnp.arange(t * k, dtype=jnp.int32) // k)
    tile_start = jnp.arange(p_rows // tile, dtype=jnp.int32) * tile
    tile_expert = jnp.minimum(jnp.sum((tile_start[:, None] >= gend[None, :]).astype(jnp.int32), axis=1),
                              n_exp - 1)
    tile_live = (tile_start < gend[-1]).astype(jnp.int32)
    return dest.astype(jnp.int32), src_tok, tile_expert, tile_live


def _row_gather_kernel(live_ref, idx_ref, src_ref, o_ref, buf_ref, sem, *, rows):
    i = pl.program_id(0)

    def row_copy(r):
        return pltpu.make_async_copy(src_ref.at[pl.ds(idx_ref[r], 1)], buf_ref.at[pl.ds(r, 1)], sem)

    def issue(r, carry):
        row_copy(r).start()
        return carry

    @pl.when(live_ref[i] != 0)
    def _():
        lax.fori_loop(0, rows, issue, 0, unroll=8)
        pltpu.make_async_copy(src_ref.at[pl.ds(0, rows)], buf_ref, sem).wait()
        o_ref[...] = buf_ref[...].astype(o_ref.dtype)

    @pl.when(live_ref[i] == 0)
    def _():
        o_ref[...] = jnp.zeros_like(o_ref)


def _row_gather(src, src_tok, tile_live, rows):
    p_rows = src_tok.shape[0]
    d = src.shape[1]
    grid_spec = pltpu.PrefetchScalarGridSpec(
        num_scalar_prefetch=1,
        grid=(p_rows // rows,),
        in_specs=[pl.BlockSpec((rows,), lambda i, lv: (i,), memory_space=pltpu.SMEM),
                  pl.BlockSpec(memory_space=pl.ANY)],
        out_specs=pl.BlockSpec((rows, d), lambda i, lv: (i, 0)),
        scratch_shapes=[pltpu.VMEM((rows, d), src.dtype), pltpu.SemaphoreType.DMA(())],
    )
    return pl.pallas_call(
        functools.partial(_row_gather_kernel, rows=rows),
        grid_spec=grid_spec,
        out_shape=jax.ShapeDtypeStruct((p_rows, d), BF16),
        compiler_params=_params("arbitrary"),
        name="moe_dispatch",
    )(tile_live, src_tok, src)


def _experts_kernel(te_ref, live_ref, x_ref, wgu_ref, wd_ref, o_ref, *, f):
    i = pl.program_id(0)

    @pl.when(live_ref[i] != 0)
    def _():
        o_ref[...] = _swiglu_tile(x_ref[...], wgu_ref.at[0], wd_ref.at[0], f)

    @pl.when(live_ref[i] == 0)
    def _():
        o_ref[...] = jnp.zeros_like(o_ref)


def _experts(xs, tile_expert, tile_live, wgu, wd, tile):
    p_rows, d = xs.shape
    n_exp, f, _ = wd.shape
    grid_spec = pltpu.PrefetchScalarGridSpec(
        num_scalar_prefetch=2,
        grid=(p_rows // tile,),
        in_specs=[
            pl.BlockSpec((tile, d), lambda i, te, lv: (i, 0)),
            pl.BlockSpec((1, d, 2 * f), lambda i, te, lv: (te[i], 0, 0)),
            pl.BlockSpec((1, f, d), lambda i, te, lv: (te[i], 0, 0)),
        ],
        out_specs=pl.BlockSpec((tile, d), lambda i, te, lv: (i, 0)),
    )
    return pl.pallas_call(
        functools.partial(_experts_kernel, f=f),
        grid_spec=grid_spec,
        out_shape=jax.ShapeDtypeStruct((p_rows, d), F32),
        compiler_params=_params("arbitrary"),
        name="moe_experts",
    )(tile_expert, tile_live, xs, wgu, wd)


def _combine_kernel(dest_ref, x_ref, g_ref, fn_ref, w_ref, ys_ref, o_ref, buf_ref, sem, *, tm):
    def row_copy(r, k):
        return pltpu.make_async_copy(ys_ref.at[pl.ds(dest_ref[TOP_K * r + k], 1)],
                                     buf_ref.at[k, pl.ds(r, 1)], sem)

    def issue(r, carry):
        for k in range(TOP_K):
            row_copy(r, k).start()
        return carry

    lax.fori_loop(0, tm, issue, 0, unroll=4)
    for k in range(TOP_K):
        pltpu.make_async_copy(ys_ref.at[pl.ds(0, tm)], buf_ref.at[k], sem).wait()
    w = w_ref[...]
    y = w[:, 0:1] * buf_ref[0]
    for k in range(1, TOP_K):
        y = y + w[:, k:k + 1] * buf_ref[k]
    o_ref[...] = _rms(x_ref[...] + g_ref[0] * y) * fn_ref[...]


def _combine(x, g, final_norm, wts, ys, dest, tm):
    b, s, d = x.shape
    tiles_per_batch = s // tm
    return pl.pallas_call(
        functools.partial(_combine_kernel, tm=tm),
        grid=(b * tiles_per_batch,),
        in_specs=[
            pl.BlockSpec((TOP_K * tm,), lambda i: (i,), memory_space=pltpu.SMEM),
            pl.BlockSpec((tm, d), lambda i: (i, 0)),
            pl.BlockSpec((1, 1, d), lambda i: (i // tiles_per_batch, 0, 0)),
            pl.BlockSpec((1, d), lambda i: (0, 0)),
            pl.BlockSpec((tm, LANES), lambda i: (i, 0)),
            pl.BlockSpec(memory_space=pl.ANY),
        ],
        out_specs=pl.BlockSpec((tm, d), lambda i: (i, 0)),
        out_shape=jax.ShapeDtypeStruct((b * s, d), F32),
        scratch_shapes=[pltpu.VMEM((TOP_K, tm, d), F32), pltpu.SemaphoreType.DMA(())],
        compiler_params=_params("arbitrary"),
        name="moe_combine",
    )(dest, x.reshape(b * s, d), g, final_norm.reshape(1, d), wts.reshape(b * s, LANES), ys).reshape(b, s, d)


def kernel(x, c, w_ada, b_ada, na_w_qkv, na_rpb, na_w_o, ffn_w_gu, ffn_w_down, mla_w_down, mla_q_norm,
           mla_w_uq, mla_kv_norm, mla_w_ukv, mla_w_o, moe_w_router, moe_w_gu, moe_w_down, final_norm):
    b, s, d = x.shape
    mods = _adaln(c, w_ada, b_ada)

    def split(layer):
        return [v.reshape(b, 1, d) for v in jnp.split(mods[layer], 6, axis=-1)]

    tm = min(1024, s)
    tmh = min(512, s)

    sh_a, sc_a, g_a, sh_f, sc_f, g_f = split(0)
    qkv = _ln_matmul(x, sh_a, sc_a, na_w_qkv[0].astype(BF16), tm, 1024)
    att = _neighborhood_attention(qkv, na_rpb[0])
    x = _proj_res(att, na_w_o[0].astype(BF16), x, g_a, tm, feature_major=False)
    x = _ffn(x, sh_f, sc_f, g_f, ffn_w_gu[0].astype(BF16), ffn_w_down[0].astype(BF16), tmh)

    sh_a, sc_a, g_a, sh_f, sc_f, g_f = split(1)
    q_t, k, v_t = _mla_proj(x, sh_a, sc_a, mla_w_down[0], mla_q_norm[0], mla_w_uq[0], mla_kv_norm[0],
                            mla_w_ukv[0], tmh)
    o_t = _mla_attention(q_t, k, v_t, min(512, s), min(1024, s // 2))
    x = _proj_res(o_t, mla_w_o[0].astype(BF16), x, g_a, tm, feature_major=True)
    h, idx, wts = _router(x, sh_f, sc_f, moe_w_router[0], tm)
    n_exp = moe_w_router.shape[-1]
    dest, src_tok, tile_expert, tile_live = _route_plan(idx[..., :TOP_K].reshape(b * s, TOP_K), n_exp, tmh)
    xs = _row_gather(h.reshape(b * s, d), src_tok, tile_live, tmh)
    ys = _experts(xs, tile_expert, tile_live, moe_w_gu[0].astype(BF16), moe_w_down[0].astype(BF16), tmh)
    return _combine(x, g_f, final_norm, wts, ys, dest, tmh)
```

```python
import functools

import jax
import jax.numpy as jnp
from jax import lax
from jax.experimental import pallas as pl
from jax.experimental.pallas import tpu as pltpu

F32 = jnp.float32
BF16 = jnp.bfloat16
HIGHEST = lax.Precision.HIGHEST

GRID_W = 64
NA_HEADS = 16
NA_WIN_H = 8
NA_WIN_W = 16
NA_ROWS_PER_STEP = 4
MLA_HEADS = 16
MLA_NOPE = 64
MLA_ROPE = 32
MLA_V = 64
ROPE_THETA = 10000.0
TOP_K = 2
NORM_EPS = 1e-6

LANES = 128
MXU_DIM = 256
SUM_ROWS = 16
MLA_SCORE_AHEAD = 2
MLA_SCORE_SLOTS = 4
VMEM_LIMIT = 56 * 1024 * 1024
NEG_BIG = -1e30
LOG2E = 1.4426950408889634


def _params(*sem):
    return pltpu.CompilerParams(dimension_semantics=sem, vmem_limit_bytes=VMEM_LIMIT)


def _rms(x):
    return x * lax.rsqrt(jnp.mean(x * x, axis=-1, keepdims=True) + NORM_EPS)


def _modulate(x, sh, sc):
    return _rms(x) * (1.0 + sc) + sh


def _adaln_kernel(c_ref, w_ref, b_ref, o_ref):
    c = c_ref[...]
    ca = c * jax.nn.sigmoid(c)
    o_ref[0] = jnp.dot(ca, w_ref[0], precision=HIGHEST, preferred_element_type=F32) + b_ref[0]


def _adaln(c, w_ada, b_ada):
    depth, d, n = w_ada.shape
    b = c.shape[0]
    rows = 8
    cp = jnp.zeros((rows, d), F32).at[:b].set(c)
    tn = 1536
    out = pl.pallas_call(
        _adaln_kernel,
        grid=(depth, n // tn),
        in_specs=[
            pl.BlockSpec((rows, d), lambda l, j: (0, 0)),
            pl.BlockSpec((1, d, tn), lambda l, j: (l, 0, j)),
            pl.BlockSpec((1, 1, tn), lambda l, j: (l, 0, j)),
        ],
        out_specs=pl.BlockSpec((1, rows, tn), lambda l, j: (l, 0, j)),
        out_shape=jax.ShapeDtypeStruct((depth, rows, n), F32),
        compiler_params=_params("arbitrary", "arbitrary"),
        name="adaln",
    )(cp, w_ada, b_ada.reshape(depth, 1, n))
    return out[:, :b]


def _ln_matmul_kernel(x_ref, sh_ref, sc_ref, w_ref, o_ref, h_ref):
    @pl.when(pl.program_id(2) == 0)
    def _():
        h_ref[...] = _modulate(x_ref[0], sh_ref[0], sc_ref[0]).astype(BF16)

    o_ref[0] = jnp.dot(h_ref[...], w_ref[...], preferred_element_type=F32).astype(o_ref.dtype)


def _ln_matmul(x, sh, sc, w, tm, tn):
    b, s, d = x.shape
    n = w.shape[1]
    return pl.pallas_call(
        _ln_matmul_kernel,
        grid=(b, s // tm, n // tn),
        in_specs=[
            pl.BlockSpec((1, tm, d), lambda bi, i, j: (bi, i, 0)),
            pl.BlockSpec((1, 1, d), lambda bi, i, j: (bi, 0, 0)),
            pl.BlockSpec((1, 1, d), lambda bi, i, j: (bi, 0, 0)),
            pl.BlockSpec((d, tn), lambda bi, i, j: (0, j)),
        ],
        out_specs=pl.BlockSpec((1, tm, tn), lambda bi, i, j: (bi, i, j)),
        out_shape=jax.ShapeDtypeStruct((b, s, n), BF16),
        scratch_shapes=[pltpu.VMEM((tm, d), BF16)],
        compiler_params=_params("arbitrary", "arbitrary", "arbitrary"),
        name="ln_matmul",
    )(x, sh, sc, w)


def _na_kernel(q_ref, k_ref, v_ref, t_ref, o_ref, *, rows):
    win = NA_WIN_H * GRID_W
    lane = lax.broadcasted_iota(jnp.int32, (GRID_W, LANES), 1)
    first = lane < (LANES // 2)
    scale = (LANES // 2) ** -0.5 * LOG2E

    def bias(pat):
        d0 = NA_WIN_H - 1 - pat
        return jnp.concatenate([t_ref[0, d0 + 2 * j] for j in range(NA_WIN_H // 2)], axis=1)

    def body(i, carry):
        rr = [i * NA_ROWS_PER_STEP + j for j in range(NA_ROWS_PER_STEP)]
        rss = [jnp.clip(r - NA_WIN_H // 2, 0, rows - NA_WIN_H) for r in rr]
        ss = []
        for r, rs in zip(rr, rss):
            q = q_ref[0, pl.ds(pl.multiple_of(r * GRID_W, GRID_W), GRID_W), :]
            zero = jnp.zeros_like(q)
            q2 = jnp.concatenate([jnp.where(first, q, zero), jnp.where(first, zero, q)], axis=0)
            kw = k_ref[0, pl.ds(pl.multiple_of(rs * GRID_W, GRID_W), win), :]
            s = lax.dot_general(q2, kw, (((1,), (1,)), ((), ())), preferred_element_type=F32)
            ss.append(s * scale + bias(r - rs))
        ps, ls = [], []
        for s in ss:
            p = jnp.exp2(s - jnp.max(s, axis=-1, keepdims=True))
            ls.append(jnp.sum(p, axis=-1, keepdims=True))
            ps.append(p.astype(BF16))
        for r, rs, p, l in zip(rr, rss, ps, ls):
            vw = v_ref[0, pl.ds(pl.multiple_of(rs * GRID_W, GRID_W), win), :]
            o = jnp.dot(p, vw, preferred_element_type=F32) / l
            o = jnp.where(first, o[:GRID_W], o[GRID_W:])
            o_ref[0, pl.ds(pl.multiple_of(r * GRID_W, GRID_W), GRID_W), :] = o.astype(o_ref.dtype)
        return carry

    lax.fori_loop(0, rows // NA_ROWS_PER_STEP, body, 0)


def _na_bias_table(rpb):
    h, ndr, ndc = rpb.shape
    qc = jnp.arange(GRID_W)
    kc = jnp.arange(GRID_W)
    cs = jnp.clip(qc - NA_WIN_W // 2, 0, GRID_W - NA_WIN_W)
    inwin = (kc[None, :] >= cs[:, None]) & (kc[None, :] < cs[:, None] + NA_WIN_W)
    period = 2 * GRID_W
    u = jnp.concatenate([rpb[:, :, NA_WIN_W - 1:], jnp.zeros((h, ndr, period - ndc), F32),
                         rpb[:, :, :NA_WIN_W - 1]], axis=-1)
    flat = jnp.tile(u, (1, 1, GRID_W))[:, :, :GRID_W * (period - 1)]
    col = flat.reshape(h, ndr, GRID_W, period - 1)[:, :, :, :GRID_W]
    col = jnp.where(inwin[None, None], col * LOG2E, NEG_BIG)
    pair = jnp.concatenate([col[:, :-1], col[:, 1:]], axis=-1)
    pair = pair.reshape(h // 2, 2, ndr - 1, GRID_W, 2 * GRID_W).transpose(0, 2, 1, 3, 4)
    return pair.reshape(h // 2, ndr - 1, 2 * GRID_W, 2 * GRID_W)


def _neighborhood_attention(qkv, rpb):
    b, s, d3 = qkv.shape
    d = d3 // 3
    rows = s // GRID_W
    assert rows >= NA_WIN_H and d // NA_HEADS == LANES // 2 and rows % NA_ROWS_PER_STEP == 0
    nslab = d // LANES
    table = _na_bias_table(rpb.astype(F32))
    win = NA_WIN_H * GRID_W
    return pl.pallas_call(
        functools.partial(_na_kernel, rows=rows),
        grid=(b, nslab),
        in_specs=[
            pl.BlockSpec((1, s, LANES), lambda bi, hp: (bi, 0, hp)),
            pl.BlockSpec((1, s, LANES), lambda bi, hp: (bi, 0, nslab + hp)),
            pl.BlockSpec((1, s, LANES), lambda bi, hp: (bi, 0, 2 * nslab + hp)),
            pl.BlockSpec((1,) + table.shape[1:], lambda bi, hp: (hp, 0, 0, 0)),
        ],
        out_specs=pl.BlockSpec((1, s, LANES), lambda bi, hp: (bi, 0, hp)),
        out_shape=jax.ShapeDtypeStruct((b, s, d), BF16),
        compiler_params=_params("arbitrary", "arbitrary"),
        name="na_attn",
    )(qkv, qkv, qkv, table)


def _proj_res_kernel(a_ref, w_ref, x_ref, g_ref, o_ref, *, feature_major):
    a = a_ref[0]
    if feature_major:
        y = lax.dot_general(a, w_ref[...], (((0,), (0,)), ((), ())), preferred_element_type=F32)
    else:
        y = jnp.dot(a, w_ref[...], preferred_element_type=F32)
    o_ref[0] = x_ref[0] + g_ref[0] * y


def _proj_res(a, w, x, g, tm, feature_major):
    b, s, d = x.shape
    k = w.shape[0]
    if feature_major:
        a_spec = pl.BlockSpec((1, k, tm), lambda bi, i: (bi, 0, i))
    else:
        a_spec = pl.BlockSpec((1, tm, k), lambda bi, i: (bi, i, 0))
    return pl.pallas_call(
        functools.partial(_proj_res_kernel, feature_major=feature_major),
        grid=(b, s // tm),
        in_specs=[
            a_spec,
            pl.BlockSpec((k, d), lambda bi, i: (0, 0)),
            pl.BlockSpec((1, tm, d), lambda bi, i: (bi, i, 0)),
            pl.BlockSpec((1, 1, d), lambda bi, i: (bi, 0, 0)),
        ],
        out_specs=pl.BlockSpec((1, tm, d), lambda bi, i: (bi, i, 0)),
        out_shape=jax.ShapeDtypeStruct((b, s, d), F32),
        compiler_params=_params("arbitrary", "arbitrary"),
        name="proj_res",
    )(a, w, x, g)


def _swiglu_tile(h, wgu, wd, f):
    acc = None
    for c in range(f // MXU_DIM):
        lo = c * MXU_DIM
        g = jnp.dot(h, wgu[:, lo:lo + MXU_DIM], preferred_element_type=F32)
        u = jnp.dot(h, wgu[:, f + lo:f + lo + MXU_DIM], preferred_element_type=F32)
        a = (g * jax.nn.sigmoid(g) * u).astype(BF16)
        y = jnp.dot(a, wd[lo:lo + MXU_DIM, :], preferred_element_type=F32)
        acc = y if acc is None else acc + y
    return acc


def _ffn_kernel(x_ref, sh_ref, sc_ref, g_ref, wgu_ref, wd_ref, o_ref, *, f):
    x = x_ref[0]
    h = _modulate(x, sh_ref[0], sc_ref[0]).astype(BF16)
    o_ref[0] = x + g_ref[0] * _swiglu_tile(h, wgu_ref, wd_ref, f)


def _ffn(x, sh, sc, g, wgu, wd, tm):
    b, s, d = x.shape
    f = wd.shape[0]
    vec = pl.BlockSpec((1, 1, d), lambda bi, i: (bi, 0, 0))
    return pl.pallas_call(
        functools.partial(_ffn_kernel, f=f),
        grid=(b, s // tm),
        in_specs=[
            pl.BlockSpec((1, tm, d), lambda bi, i: (bi, i, 0)),
            vec, vec, vec,
            pl.BlockSpec((d, 2 * f), lambda bi, i: (0, 0)),
            pl.BlockSpec((f, d), lambda bi, i: (0, 0)),
        ],
        out_specs=pl.BlockSpec((1, tm, d), lambda bi, i: (bi, i, 0)),
        out_shape=jax.ShapeDtypeStruct((b, s, d), F32),
        compiler_params=_params("arbitrary", "arbitrary"),
        name="ffn",
    )(x, sh, sc, g, wgu, wd)


def _mla_proj_kernel(x_ref, sh_ref, sc_ref, wd_ref, qn_ref, kvn_ref, wq_ref, wk_ref, wv_ref,
                     cs_ref, sn_ref, cst_ref, snt_ref, q_ref, k_ref, v_ref, *, q_lora, kv_lora, qscale):
    h = _modulate(x_ref[0], sh_ref[0], sc_ref[0]).astype(BF16)
    down = jnp.dot(h, wd_ref[...], preferred_element_type=F32)
    cq = (_rms(down[:, :q_lora]) * qn_ref[...]).astype(BF16)
    ckv = (_rms(down[:, q_lora:q_lora + kv_lora]) * kvn_ref[...]).astype(BF16)
    r0 = q_lora + kv_lora
    kr = down[:, r0:r0 + LANES] * cs_ref[...] + down[:, r0 + LANES:r0 + 2 * LANES] * sn_ref[...]
    z = jnp.concatenate([ckv, kr.astype(BF16)], axis=1)
    k_ref[0] = jnp.dot(z, wk_ref[...], preferred_element_type=F32).astype(BF16)
    nt = (((1,), (1,)), ((), ()))
    v_ref[0] = lax.dot_general(wv_ref[...], ckv, nt, preferred_element_type=F32).astype(BF16)
    qt = lax.dot_general(wq_ref[...], cq, nt, preferred_element_type=F32)
    cst = cst_ref[...]
    snt = snt_ref[...]
    for hd in range(MLA_HEADS):
        base = hd * LANES
        blk = qt[base:base + LANES]
        rot = blk[MLA_NOPE:MLA_NOPE + MLA_ROPE] * cst + blk[MLA_NOPE + MLA_ROPE:] * snt
        q_ref[0, base:base + MLA_NOPE, :] = (blk[:MLA_NOPE] * qscale).astype(BF16)
        q_ref[0, base + MLA_NOPE:base + MLA_NOPE + MLA_ROPE, :] = (rot * qscale).astype(BF16)
        q_ref[0, base + MLA_NOPE + MLA_ROPE:base + LANES, :] = jnp.zeros(
            (LANES - MLA_NOPE - MLA_ROPE, rot.shape[1]), BF16)


def _rope_perm_weights(w_rope):
    x1 = w_rope[..., 0::2]
    x2 = w_rope[..., 1::2]
    return jnp.concatenate([x1, x2], axis=-1), jnp.concatenate([-x2, x1], axis=-1)


def _mla_proj(x, sh, sc, w_down, q_norm, w_uq, kv_norm, w_ukv, tm):
    b, s, d = x.shape
    q_lora = q_norm.shape[0]
    kv_lora = kv_norm.shape[0]
    hd_q = MLA_NOPE + MLA_ROPE
    assert hd_q <= LANES and MLA_ROPE % 2 == 0 and q_lora % LANES == 0 and kv_lora == LANES

    r_a, r_b = _rope_perm_weights(w_down[:, q_lora + kv_lora:])
    pad = jnp.zeros((d, LANES - MLA_ROPE), F32)
    wd_ext = jnp.concatenate([w_down[:, :q_lora + kv_lora], r_a, pad, r_b, pad], axis=1).astype(BF16)

    wq = w_uq.reshape(q_lora, MLA_HEADS, hd_q)
    ra, rb = _rope_perm_weights(wq[..., MLA_NOPE:])
    wq_ext = jnp.concatenate([wq[..., :MLA_NOPE], ra, rb], axis=-1)
    wq_ext_t = wq_ext.reshape(q_lora, MLA_HEADS * LANES).T.astype(BF16)

    wkv = w_ukv.reshape(kv_lora, MLA_HEADS, MLA_NOPE + MLA_V)
    top = jnp.concatenate([wkv[..., :MLA_NOPE], jnp.zeros((kv_lora, MLA_HEADS, LANES - MLA_NOPE), F32)], axis=-1)
    copy = jnp.concatenate([jnp.zeros((LANES, MLA_NOPE), F32), jnp.eye(LANES, MLA_ROPE, dtype=F32),
                            jnp.zeros((LANES, LANES - MLA_NOPE - MLA_ROPE), F32)], axis=1)
    bot = jnp.broadcast_to(copy[:, None, :], (LANES, MLA_HEADS, LANES))
    wk_ext = jnp.concatenate([top, bot], axis=0).reshape(2 * LANES, MLA_HEADS * LANES).astype(BF16)
    wv_t = wkv[:, :, MLA_NOPE:].reshape(kv_lora, MLA_HEADS * MLA_V).T.astype(BF16)

    t = jnp.arange(s)
    row = (t // GRID_W).astype(F32)
    col = (t % GRID_W).astype(F32)
    nf = MLA_ROPE // 4
    inv = ROPE_THETA ** (-jnp.arange(nf, dtype=F32) / nf)
    ang = jnp.concatenate([row[:, None] * inv, col[:, None] * inv], axis=-1)
    cos2 = jnp.concatenate([jnp.cos(ang), jnp.cos(ang)], axis=1)
    sin2 = jnp.concatenate([jnp.sin(ang), jnp.sin(ang)], axis=1)
    lpad = jnp.zeros((s, LANES - MLA_ROPE), F32)
    cos_tok = jnp.concatenate([cos2, lpad], axis=1)
    sin_tok = jnp.concatenate([sin2, lpad], axis=1)
    cos_t = cos2.T
    sin_t = sin2.T

    qscale = float(hd_q ** -0.5 * LOG2E)
    nq = MLA_HEADS * LANES
    nv = MLA_HEADS * MLA_V
    nd = wd_ext.shape[1]
    full = lambda shape: pl.BlockSpec(shape, lambda bi, i: (0,) * len(shape))
    vec = pl.BlockSpec((1, 1, d), lambda bi, i: (bi, 0, 0))
    return pl.pallas_call(
        functools.partial(_mla_proj_kernel, q_lora=q_lora, kv_lora=kv_lora, qscale=qscale),
        grid=(b, s // tm),
        in_specs=[
            pl.BlockSpec((1, tm, d), lambda bi, i: (bi, i, 0)),
            vec, vec,
            full((d, nd)), full((1, q_lora)), full((1, kv_lora)),
            full((nq, q_lora)), full((2 * LANES, nq)), full((nv, kv_lora)),
            pl.BlockSpec((tm, LANES), lambda bi, i: (i, 0)),
            pl.BlockSpec((tm, LANES), lambda bi, i: (i, 0)),
            pl.BlockSpec((MLA_ROPE, tm), lambda bi, i: (0, i)),
            pl.BlockSpec((MLA_ROPE, tm), lambda bi, i: (0, i)),
        ],
        out_specs=[
            pl.BlockSpec((1, nq, tm), lambda bi, i: (bi, 0, i)),
            pl.BlockSpec((1, tm, nq), lambda bi, i: (bi, i, 0)),
            pl.BlockSpec((1, nv, tm), lambda bi, i: (bi, 0, i)),
        ],
        out_shape=[
            jax.ShapeDtypeStruct((b, nq, s), BF16),
            jax.ShapeDtypeStruct((b, s, nq), BF16),
            jax.ShapeDtypeStruct((b, nv, s), BF16),
        ],
        compiler_params=_params("arbitrary", "arbitrary"),
        name="mla_proj",
    )(x, sh, sc, wd_ext, q_norm.reshape(1, -1), kv_norm.reshape(1, -1), wq_ext_t, wk_ext, wv_t,
      cos_tok, sin_tok, cos_t, sin_t)


def _mla_attn_kernel(q_ref, k_ref, v_ref, o_ref, *s_refs, tq, tk, nkt, nq):
    ones = jnp.ones((SUM_ROWS, tk), BF16)

    def scores(qi, kt):
        qt = q_ref[0, :, pl.ds(pl.multiple_of(qi * tq, tq), tq)]
        s_refs[kt % MLA_SCORE_SLOTS][...] = jnp.dot(k_ref[0, kt * tk:(kt + 1) * tk, :], qt,
                                                    preferred_element_type=F32)

    def softmax_pv(kt, carry):
        m, acc = carry
        st = s_refs[kt % MLA_SCORE_SLOTS][...]
        m_new = jnp.maximum(m, jnp.max(st, axis=0, keepdims=True))
        p = jnp.exp2(st - m_new).astype(BF16)
        vb = jnp.concatenate([v_ref[0, :, kt * tk:(kt + 1) * tk], ones], axis=0)
        acc = jnp.exp2(m - m_new) * acc + jnp.dot(vb, p, preferred_element_type=F32)
        return m_new, acc

    for kt in range(MLA_SCORE_AHEAD):
        scores(0, kt)

    def body(qi, carry):
        nxt = jnp.minimum(qi + 1, nq - 1)
        carry = (jnp.full((1, tq), NEG_BIG, F32), jnp.zeros((MLA_V + SUM_ROWS, tq), F32))
        for kt in range(nkt):
            ahead = kt + MLA_SCORE_AHEAD
            if ahead < nkt:
                scores(qi, ahead)
            else:
                scores(nxt, ahead - nkt)
            carry = softmax_pv(kt, carry)
        m, acc = carry
        out = acc[:MLA_V] / acc[MLA_V:MLA_V + 1]
        o_ref[0, :, pl.ds(pl.multiple_of(qi * tq, tq), tq)] = out.astype(o_ref.dtype)
        return 0

    lax.fori_loop(0, nq, body, 0)


def _mla_attention(q_t, k, v_t, tq, tk):
    b, _, s = q_t.shape
    nv = v_t.shape[1]
    nkt = s // tk
    assert nkt % MLA_SCORE_SLOTS == 0 and nkt >= MLA_SCORE_AHEAD
    return pl.pallas_call(
        functools.partial(_mla_attn_kernel, tq=tq, tk=tk, nkt=nkt, nq=s // tq),
        grid=(b, MLA_HEADS),
        in_specs=[
            pl.BlockSpec((1, LANES, s), lambda bi, hd: (bi, hd, 0)),
            pl.BlockSpec((1, s, LANES), lambda bi, hd: (bi, 0, hd)),
            pl.BlockSpec((1, MLA_V, s), lambda bi, hd: (bi, hd, 0)),
        ],
        out_specs=pl.BlockSpec((1, MLA_V, s), lambda bi, hd: (bi, hd, 0)),
        out_shape=jax.ShapeDtypeStruct((b, nv, s), BF16),
        scratch_shapes=[pltpu.VMEM((tk, tq), F32)] * MLA_SCORE_SLOTS,
        compiler_params=_params("arbitrary", "arbitrary"),
        name="mla_attn",
    )(q_t, k, v_t)


def _router_kernel(x_ref, sh_ref, sc_ref, wr_ref, h_ref, i_ref, w_ref, *, n_exp):
    h = _modulate(x_ref[0], sh_ref[0], sc_ref[0])
    h_ref[0] = h
    w = wr_ref[...]
    h_hi = h.astype(BF16)
    h_lo = (h - h_hi.astype(F32)).astype(BF16)
    w_hi = w.astype(BF16)
    w_lo = (w - w_hi.astype(F32)).astype(BF16)
    logits = (jnp.dot(h_hi, w_hi, preferred_element_type=F32)
              + (jnp.dot(h_lo, w_hi, preferred_element_type=F32)
                 + jnp.dot(h_hi, w_lo, preferred_element_type=F32)))
    lane = lax.broadcasted_iota(jnp.int32, logits.shape, 1)
    valid = lane < n_exp
    logits = jnp.where(valid, logits, NEG_BIG)
    e = jnp.exp(logits - jnp.max(logits, axis=-1, keepdims=True))
    probs = e / jnp.sum(e, axis=-1, keepdims=True)
    probs = jnp.where(valid, probs, -1.0)
    vals, idxs = [], []
    rest = probs
    for _ in range(TOP_K):
        v = jnp.max(rest, axis=-1, keepdims=True)
        idx = jnp.min(jnp.where(rest == v, lane, LANES), axis=-1, keepdims=True)
        vals.append(v)
        idxs.append(idx)
        rest = jnp.where(lane == idx, -1.0, rest)
    tot = vals[0]
    for v in vals[1:]:
        tot = tot + v
    i_out = jnp.zeros(probs.shape, jnp.int32)
    w_out = jnp.zeros_like(probs)
    for k in range(TOP_K):
        i_out = jnp.where(lane == k, idxs[k], i_out)
        w_out = jnp.where(lane == k, vals[k] / tot, w_out)
    i_ref[0] = i_out
    w_ref[0] = w_out


def _router(x, sh, sc, w_router, tm):
    b, s, d = x.shape
    n_exp = w_router.shape[1]
    wr = jnp.zeros((d, LANES), F32).at[:, :n_exp].set(w_router)
    vec = pl.BlockSpec((1, 1, d), lambda bi, i: (bi, 0, 0))
    lanes_spec = pl.BlockSpec((1, tm, LANES), lambda bi, i: (bi, i, 0))
    return pl.pallas_call(
        functools.partial(_router_kernel, n_exp=n_exp),
        grid=(b, s // tm),
        in_specs=[
            pl.BlockSpec((1, tm, d), lambda bi, i: (bi, i, 0)),
            vec, vec,
            pl.BlockSpec((d, LANES), lambda bi, i: (0, 0)),
        ],
        out_specs=[pl.BlockSpec((1, tm, d), lambda bi, i: (bi, i, 0)), lanes_spec, lanes_spec],
        out_shape=[jax.ShapeDtypeStruct((b, s, d), F32),
                   jax.ShapeDtypeStruct((b, s, LANES), jnp.int32),
                   jax.ShapeDtypeStruct((b, s, LANES), F32)],
        compiler_params=_params("arbitrary", "arbitrary"),
        name="router",
    )(x, sh, sc, wr)


def _route_plan(idx, n_exp, tile):
    t, k = idx.shape
    e_flat = idx.reshape(t * k)
    onehot = (e_flat[:, None] == jnp.arange(n_exp, dtype=jnp.int32)[None, :]).astype(jnp.int32)
    csum = jnp.cumsum(onehot, axis=0)
    counts = csum[-1]
    rank = jnp.sum(onehot * csum, axis=1) - 1
    padded = ((counts + tile - 1) // tile) * tile
    gend = jnp.cumsum(padded)
    gstart = gend - padded
    dest = jnp.sum(onehot * gstart[None, :], axis=1) + rank
    p_rows = t * k + n_exp * tile
    src_tok = jnp.zeros((p_rows,), jnp.int32).at[dest].set(jnp.arange(t * k, dtype=jnp.int32) // k)
    tile_start = jnp.arange(p_rows // tile, dtype=jnp.int32) * tile
    tile_expert = jnp.minimum(jnp.sum((tile_start[:, None] >= gend[None, :]).astype(jnp.int32), axis=1),
                              n_exp - 1)
    tile_live = (tile_start < gend[-1]).astype(jnp.int32)
    return dest.astype(jnp.int32), src_tok, tile_expert, tile_live


def _row_gather_kernel(live_ref, idx_ref, src_ref, o_ref, buf_ref, sem, *, rows):
    i = pl.program_id(0)

    def row_copy(r):
        return pltpu.make_async_copy(src_ref.at[pl.ds(idx_ref[r], 1)], buf_ref.at[pl.ds(r, 1)], sem)

    def issue(r, carry):
        row_copy(r).start()
        return carry

    @pl.when(live_ref[i] != 0)
    def _():
        lax.fori_loop(0, rows, issue, 0, unroll=8)
        pltpu.make_async_copy(src_ref.at[pl.ds(0, rows)], buf_ref, sem).wait()
        o_ref[...] = buf_ref[...].astype(o_ref.dtype)

    @pl.when(live_ref[i] == 0)
    def _():
        o_ref[...] = jnp.zeros_like(o_ref)


def _row_gather(src, src_tok, tile_live, rows):
    p_rows = src_tok.shape[0]
    d = src.shape[1]
    grid_spec = pltpu.PrefetchScalarGridSpec(
        num_scalar_prefetch=1,
        grid=(p_rows // rows,),
        in_specs=[pl.BlockSpec((rows,), lambda i, lv: (i,), memory_space=pltpu.SMEM),
                  pl.BlockSpec(memory_space=pl.ANY)],
        out_specs=pl.BlockSpec((rows, d), lambda i, lv: (i, 0)),
        scratch_shapes=[pltpu.VMEM((rows, d), src.dtype), pltpu.SemaphoreType.DMA(())],
    )
    return pl.pallas_call(
        functools.partial(_row_gather_kernel, rows=rows),
        grid_spec=grid_spec,
        out_shape=jax.ShapeDtypeStruct((p_rows, d), BF16),
        compiler_params=_params("arbitrary"),
        name="moe_dispatch",
    )(tile_live, src_tok, src)


def _experts_kernel(te_ref, live_ref, x_ref, wgu_ref, wd_ref, o_ref, *, f):
    i = pl.program_id(0)

    @pl.when(live_ref[i] != 0)
    def _():
        o_ref[...] = _swiglu_tile(x_ref[...], wgu_ref.at[0], wd_ref.at[0], f)

    @pl.when(live_ref[i] == 0)
    def _():
        o_ref[...] = jnp.zeros_like(o_ref)


def _experts(xs, tile_expert, tile_live, wgu, wd, tile):
    p_rows, d = xs.shape
    n_exp, f, _ = wd.shape
    grid_spec = pltpu.PrefetchScalarGridSpec(
        num_scalar_prefetch=2,
        grid=(p_rows // tile,),
        in_specs=[
            pl.BlockSpec((tile, d), lambda i, te, lv: (i, 0)),
            pl.BlockSpec((1, d, 2 * f), lambda i, te, lv: (te[i], 0, 0)),
            pl.BlockSpec((1, f, d), lambda i, te, lv: (te[i], 0, 0)),
        ],
        out_specs=pl.BlockSpec((tile, d), lambda i, te, lv: (i, 0)),
    )
    return pl.pallas_call(
        functools.partial(_experts_kernel, f=f),
        grid_spec=grid_spec,
        out_shape=jax.ShapeDtypeStruct((p_rows, d), F32),
        compiler_params=_params("arbitrary"),
        name="moe_experts",
    )(tile_expert, tile_live, xs, wgu, wd)


def _combine_kernel(dest_ref, x_ref, g_ref, fn_ref, w_ref, ys_ref, o_ref, buf_ref, sem, *, tm):
    def row_copy(r, k):
        return pltpu.make_async_copy(ys_ref.at[pl.ds(dest_ref[TOP_K * r + k], 1)],
                                     buf_ref.at[k, pl.ds(r, 1)], sem)

    def issue(r, carry):
        for k in range(TOP_K):
            row_copy(r, k).start()
        return carry

    lax.fori_loop(0, tm, issue, 0, unroll=4)
    for k in range(TOP_K):
        pltpu.make_async_copy(ys_ref.at[pl.ds(0, tm)], buf_ref.at[k], sem).wait()
    w = w_ref[...]
    y = w[:, 0:1] * buf_ref[0]
    for k in range(1, TOP_K):
        y = y + w[:, k:k + 1] * buf_ref[k]
    o_ref[...] = _rms(x_ref[...] + g_ref[0] * y) * fn_ref[...]


def _combine(x, g, final_norm, wts, ys, dest, tm):
    b, s, d = x.shape
    tiles_per_batch = s // tm
    return pl.pallas_call(
        functools.partial(_combine_kernel, tm=tm),
        grid=(b * tiles_per_batch,),
        in_specs=[
            pl.BlockSpec((TOP_K * tm,), lambda i: (i,), memory_space=pltpu.SMEM),
            pl.BlockSpec((tm, d), lambda i: (i, 0)),
            pl.BlockSpec((1, 1, d), lambda i: (i // tiles_per_batch, 0, 0)),
            pl.BlockSpec((1, d), lambda i: (0, 0)),
            pl.BlockSpec((tm, LANES), lambda i: (i, 0)),
            pl.BlockSpec(memory_space=pl.ANY),
        ],
        out_specs=pl.BlockSpec((tm, d), lambda i: (i, 0)),
        out_shape=jax.ShapeDtypeStruct((b * s, d), F32),
        scratch_shapes=[pltpu.VMEM((TOP_K, tm, d), F32), pltpu.SemaphoreType.DMA(())],
        compiler_params=_params("arbitrary"),
        name="moe_combine",
    )(dest, x.reshape(b * s, d), g, final_norm.reshape(1, d), wts.reshape(b * s, LANES), ys).reshape(b, s, d)


def kernel(x, c, w_ada, b_ada, na_w_qkv, na_rpb, na_w_o, ffn_w_gu, ffn_w_down, mla_w_down, mla_q_norm,
           mla_w_uq, mla_kv_norm, mla_w_ukv, mla_w_o, moe_w_router, moe_w_gu, moe_w_down, final_norm):
    b, s, d = x.shape
    mods = _adaln(c, w_ada, b_ada)

    def split(layer):
        return [v.reshape(b, 1, d) for v in jnp.split(mods[layer], 6, axis=-1)]

    tm = min(1024, s)
    tmh = min(512, s)

    sh_a, sc_a, g_a, sh_f, sc_f, g_f = split(0)
    qkv = _ln_matmul(x, sh_a, sc_a, na_w_qkv[0].astype(BF16), tm, 1024)
    att = _neighborhood_attention(qkv, na_rpb[0])
    x = _proj_res(att, na_w_o[0].astype(BF16), x, g_a, tm, feature_major=False)
    x = _ffn(x, sh_f, sc_f, g_f, ffn_w_gu[0].astype(BF16), ffn_w_down[0].astype(BF16), tmh)

    sh_a, sc_a, g_a, sh_f, sc_f, g_f = split(1)
    q_t, k, v_t = _mla_proj(x, sh_a, sc_a, mla_w_down[0], mla_q_norm[0], mla_w_uq[0], mla_kv_norm[0],
                            mla_w_ukv[0], tmh)
    o_t = _mla_attention(q_t, k, v_t, min(512, s), 256)
    x = _proj_res(o_t, mla_w_o[0].astype(BF16), x, g_a, tm, feature_major=True)
    h, idx, wts = _router(x, sh_f, sc_f, moe_w_router[0], tm)
    n_exp = moe_w_router.shape[-1]
    dest, src_tok, tile_expert, tile_live = _route_plan(idx[..., :TOP_K].reshape(b * s, TOP_K), n_exp, tmh)
    xs = _row_gather(h.reshape(b * s, d), src_tok, tile_live, tmh)
    ys = _experts(xs, tile_expert, tile_live, moe_w_gu[0].astype(BF16), moe_w_down[0].astype(BF16), tmh)
    return _combine(x, g_f, final_norm, wts, ys, dest, tmh)
```

```python
import functools

import jax
import jax.numpy as jnp
from jax import lax
from jax.experimental import pallas as pl
from jax.experimental.pallas import tpu as pltpu

F32 = jnp.float32
BF16 = jnp.bfloat16
HIGHEST = lax.Precision.HIGHEST

GRID_W = 64
NA_HEADS = 16
NA_WIN_H = 8
NA_WIN_W = 16
NA_ROWS_PER_STEP = 4
MLA_HEADS = 16
MLA_NOPE = 64
MLA_ROPE = 32
MLA_V = 64
ROPE_THETA = 10000.0
TOP_K = 2
NORM_EPS = 1e-6

LANES = 128
MXU_DIM = 256
SUM_ROWS = 16
MLA_SCORE_AHEAD = 2
MLA_SCORE_SLOTS = 4
VMEM_LIMIT = 56 * 1024 * 1024
NEG_BIG = -1e30
LOG2E = 1.4426950408889634


def _params(*sem):
    return pltpu.CompilerParams(dimension_semantics=sem, vmem_limit_bytes=VMEM_LIMIT)


def _rms(x):
    return x * lax.rsqrt(jnp.mean(x * x, axis=-1, keepdims=True) + NORM_EPS)


def _modulate(x, sh, sc):
    return _rms(x) * (1.0 + sc) + sh


def _adaln_kernel(c_ref, w_ref, b_ref, o_ref):
    c = c_ref[...]
    ca = c * jax.nn.sigmoid(c)
    o_ref[0] = jnp.dot(ca, w_ref[0], precision=HIGHEST, preferred_element_type=F32) + b_ref[0]


def _adaln(c, w_ada, b_ada):
    depth, d, n = w_ada.shape
    b = c.shape[0]
    rows = 8
    cp = jnp.zeros((rows, d), F32).at[:b].set(c)
    tn = 1536
    out = pl.pallas_call(
        _adaln_kernel,
        grid=(depth, n // tn),
        in_specs=[
            pl.BlockSpec((rows, d), lambda l, j: (0, 0)),
            pl.BlockSpec((1, d, tn), lambda l, j: (l, 0, j)),
            pl.BlockSpec((1, 1, tn), lambda l, j: (l, 0, j)),
        ],
        out_specs=pl.BlockSpec((1, rows, tn), lambda l, j: (l, 0, j)),
        out_shape=jax.ShapeDtypeStruct((depth, rows, n), F32),
        compiler_params=_params("arbitrary", "arbitrary"),
        name="adaln",
    )(cp, w_ada, b_ada.reshape(depth, 1, n))
    return out[:, :b]


def _ln_matmul_kernel(x_ref, sh_ref, sc_ref, w_ref, o_ref):
    h = _modulate(x_ref[0], sh_ref[0], sc_ref[0]).astype(BF16)
    o_ref[0] = jnp.dot(h, w_ref[...], preferred_element_type=F32).astype(o_ref.dtype)


def _ln_matmul(x, sh, sc, w, tm):
    b, s, d = x.shape
    n = w.shape[1]
    vec = pl.BlockSpec((1, 1, d), lambda bi, i: (bi, 0, 0))
    return pl.pallas_call(
        _ln_matmul_kernel,
        grid=(b, s // tm),
        in_specs=[
            pl.BlockSpec((1, tm, d), lambda bi, i: (bi, i, 0)),
            vec, vec,
            pl.BlockSpec((d, n), lambda bi, i: (0, 0)),
        ],
        out_specs=pl.BlockSpec((1, tm, n), lambda bi, i: (bi, i, 0)),
        out_shape=jax.ShapeDtypeStruct((b, s, n), BF16),
        compiler_params=_params("arbitrary", "arbitrary"),
        name="ln_matmul",
    )(x, sh, sc, w)


def _na_kernel(q_ref, k_ref, v_ref, t_ref, o_ref, *, rows):
    win = NA_WIN_H * GRID_W
    lane = lax.broadcasted_iota(jnp.int32, (GRID_W, LANES), 1)
    first = lane < (LANES // 2)
    scale = (LANES // 2) ** -0.5 * LOG2E

    def bias(pat):
        d0 = NA_WIN_H - 1 - pat
        return jnp.concatenate([t_ref[0, d0 + 2 * j] for j in range(NA_WIN_H // 2)], axis=1)

    def body(i, carry):
        rr = [i * NA_ROWS_PER_STEP + j for j in range(NA_ROWS_PER_STEP)]
        rss = [jnp.clip(r - NA_WIN_H // 2, 0, rows - NA_WIN_H) for r in rr]
        ss = []
        for r, rs in zip(rr, rss):
            q = q_ref[0, pl.ds(pl.multiple_of(r * GRID_W, GRID_W), GRID_W), :]
            zero = jnp.zeros_like(q)
            q2 = jnp.concatenate([jnp.where(first, q, zero), jnp.where(first, zero, q)], axis=0)
            kw = k_ref[0, pl.ds(pl.multiple_of(rs * GRID_W, GRID_W), win), :]
            s = lax.dot_general(q2, kw, (((1,), (1,)), ((), ())), preferred_element_type=F32)
            ss.append(s * scale + bias(r - rs))
        ps, ls = [], []
        for s in ss:
            p = jnp.exp2(s - jnp.max(s, axis=-1, keepdims=True))
            ls.append(jnp.sum(p, axis=-1, keepdims=True))
            ps.append(p.astype(BF16))
        for r, rs, p, l in zip(rr, rss, ps, ls):
            vw = v_ref[0, pl.ds(pl.multiple_of(rs * GRID_W, GRID_W), win), :]
            o = jnp.dot(p, vw, preferred_element_type=F32) / l
            o = jnp.where(first, o[:GRID_W], o[GRID_W:])
            o_ref[0, pl.ds(pl.multiple_of(r * GRID_W, GRID_W), GRID_W), :] = o.astype(o_ref.dtype)
        return carry

    lax.fori_loop(0, rows // NA_ROWS_PER_STEP, body, 0)


def _na_bias_table(rpb):
    h, ndr, ndc = rpb.shape
    qc = jnp.arange(GRID_W)
    kc = jnp.arange(GRID_W)
    cs = jnp.clip(qc - NA_WIN_W // 2, 0, GRID_W - NA_WIN_W)
    inwin = (kc[None, :] >= cs[:, None]) & (kc[None, :] < cs[:, None] + NA_WIN_W)
    period = 2 * GRID_W
    u = jnp.concatenate([rpb[:, :, NA_WIN_W - 1:], jnp.zeros((h, ndr, period - ndc), F32),
                         rpb[:, :, :NA_WIN_W - 1]], axis=-1)
    flat = jnp.tile(u, (1, 1, GRID_W))[:, :, :GRID_W * (period - 1)]
    col = flat.reshape(h, ndr, GRID_W, period - 1)[:, :, :, :GRID_W]
    col = jnp.where(inwin[None, None], col * LOG2E, NEG_BIG)
    pair = jnp.concatenate([col[:, :-1], col[:, 1:]], axis=-1)
    pair = pair.reshape(h // 2, 2, ndr - 1, GRID_W, 2 * GRID_W).transpose(0, 2, 1, 3, 4)
    return pair.reshape(h // 2, ndr - 1, 2 * GRID_W, 2 * GRID_W)


def _neighborhood_attention(qkv, rpb):
    b, s, d3 = qkv.shape
    d = d3 // 3
    rows = s // GRID_W
    assert rows >= NA_WIN_H and d // NA_HEADS == LANES // 2 and rows % NA_ROWS_PER_STEP == 0
    nslab = d // LANES
    table = _na_bias_table(rpb.astype(F32))
    win = NA_WIN_H * GRID_W
    return pl.pallas_call(
        functools.partial(_na_kernel, rows=rows),
        grid=(b, nslab),
        in_specs=[
            pl.BlockSpec((1, s, LANES), lambda bi, hp: (bi, 0, hp)),
            pl.BlockSpec((1, s, LANES), lambda bi, hp: (bi, 0, nslab + hp)),
            pl.BlockSpec((1, s, LANES), lambda bi, hp: (bi, 0, 2 * nslab + hp)),
            pl.BlockSpec((1,) + table.shape[1:], lambda bi, hp: (hp, 0, 0, 0)),
        ],
        out_specs=pl.BlockSpec((1, s, LANES), lambda bi, hp: (bi, 0, hp)),
        out_shape=jax.ShapeDtypeStruct((b, s, d), BF16),
        compiler_params=_params("arbitrary", "arbitrary"),
        name="na_attn",
    )(qkv, qkv, qkv, table)


def _swiglu_tile(h, wgu, wd, f):
    acc = None
    for c in range(f // MXU_DIM):
        lo = c * MXU_DIM
        g = jnp.dot(h, wgu[:, lo:lo + MXU_DIM], preferred_element_type=F32)
        u = jnp.dot(h, wgu[:, f + lo:f + lo + MXU_DIM], preferred_element_type=F32)
        a = (g * jax.nn.sigmoid(g) * u).astype(BF16)
        y = jnp.dot(a, wd[lo:lo + MXU_DIM, :], preferred_element_type=F32)
        acc = y if acc is None else acc + y
    return acc


def _ffn_kernel(a_ref, wo_ref, x_ref, ga_ref, sh_ref, sc_ref, g_ref, wgu_ref, wd_ref, o_ref, *, f):
    x = x_ref[0] + ga_ref[0] * jnp.dot(a_ref[0], wo_ref[...], preferred_element_type=F32)
    h = _modulate(x, sh_ref[0], sc_ref[0]).astype(BF16)
    o_ref[0] = x + g_ref[0] * _swiglu_tile(h, wgu_ref, wd_ref, f)


def _proj_ffn(att, w_o, x, g_a, sh, sc, g, wgu, wd, tm):
    b, s, d = x.shape
    f = wd.shape[0]
    vec = pl.BlockSpec((1, 1, d), lambda bi, i: (bi, 0, 0))
    tile = pl.BlockSpec((1, tm, d), lambda bi, i: (bi, i, 0))
    return pl.pallas_call(
        functools.partial(_ffn_kernel, f=f),
        grid=(b, s // tm),
        in_specs=[
            tile,
            pl.BlockSpec((d, d), lambda bi, i: (0, 0)),
            tile,
            vec, vec, vec, vec,
            pl.BlockSpec((d, 2 * f), lambda bi, i: (0, 0)),
            pl.BlockSpec((f, d), lambda bi, i: (0, 0)),
        ],
        out_specs=tile,
        out_shape=jax.ShapeDtypeStruct((b, s, d), F32),
        compiler_params=_params("arbitrary", "arbitrary"),
        name="proj_ffn",
    )(att, w_o, x, g_a, sh, sc, g, wgu, wd)


def _mla_proj_kernel(x_ref, sh_ref, sc_ref, wd_ref, qn_ref, kvn_ref, wq_ref, wk_ref, wv_ref,
                     cs_ref, sn_ref, cst_ref, snt_ref, q_ref, k_ref, v_ref, *, q_lora, kv_lora, qscale):
    h = _modulate(x_ref[0], sh_ref[0], sc_ref[0]).astype(BF16)
    down = jnp.dot(h, wd_ref[...], preferred_element_type=F32)
    cq = (_rms(down[:, :q_lora]) * qn_ref[...]).astype(BF16)
    ckv = (_rms(down[:, q_lora:q_lora + kv_lora]) * kvn_ref[...]).astype(BF16)
    r0 = q_lora + kv_lora
    kr = down[:, r0:r0 + LANES] * cs_ref[...] + down[:, r0 + LANES:r0 + 2 * LANES] * sn_ref[...]
    z = jnp.concatenate([ckv, kr.astype(BF16)], axis=1)
    k_ref[0] = jnp.dot(z, wk_ref[...], preferred_element_type=F32).astype(BF16)
    nt = (((1,), (1,)), ((), ()))
    v_ref[0] = lax.dot_general(wv_ref[...], ckv, nt, preferred_element_type=F32).astype(BF16)
    qt = lax.dot_general(wq_ref[...], cq, nt, preferred_element_type=F32)
    cst = cst_ref[...]
    snt = snt_ref[...]
    for hd in range(MLA_HEADS):
        base = hd * LANES
        blk = qt[base:base + LANES]
        rot = blk[MLA_NOPE:MLA_NOPE + MLA_ROPE] * cst + blk[MLA_NOPE + MLA_ROPE:] * snt
        q_ref[0, base:base + MLA_NOPE, :] = (blk[:MLA_NOPE] * qscale).astype(BF16)
        q_ref[0, base + MLA_NOPE:base + MLA_NOPE + MLA_ROPE, :] = (rot * qscale).astype(BF16)
        q_ref[0, base + MLA_NOPE + MLA_ROPE:base + LANES, :] = jnp.zeros(
            (LANES - MLA_NOPE - MLA_ROPE, rot.shape[1]), BF16)


def _rope_perm_weights(w_rope):
    x1 = w_rope[..., 0::2]
    x2 = w_rope[..., 1::2]
    return jnp.concatenate([x1, x2], axis=-1), jnp.concatenate([-x2, x1], axis=-1)


def _mla_proj(x, sh, sc, w_down, q_norm, w_uq, kv_norm, w_ukv, tm):
    b, s, d = x.shape
    q_lora = q_norm.shape[0]
    kv_lora = kv_norm.shape[0]
    hd_q = MLA_NOPE + MLA_ROPE
    assert hd_q <= LANES and MLA_ROPE % 2 == 0 and q_lora % LANES == 0 and kv_lora == LANES

    r_a, r_b = _rope_perm_weights(w_down[:, q_lora + kv_lora:])
    pad = jnp.zeros((d, LANES - MLA_ROPE), F32)
    wd_ext = jnp.concatenate([w_down[:, :q_lora + kv_lora], r_a, pad, r_b, pad], axis=1).astype(BF16)

    wq = w_uq.reshape(q_lora, MLA_HEADS, hd_q)
    ra, rb = _rope_perm_weights(wq[..., MLA_NOPE:])
    wq_ext = jnp.concatenate([wq[..., :MLA_NOPE], ra, rb], axis=-1)
    wq_ext_t = wq_ext.reshape(q_lora, MLA_HEADS * LANES).T.astype(BF16)

    wkv = w_ukv.reshape(kv_lora, MLA_HEADS, MLA_NOPE + MLA_V)
    top = jnp.concatenate([wkv[..., :MLA_NOPE], jnp.zeros((kv_lora, MLA_HEADS, LANES - MLA_NOPE), F32)], axis=-1)
    copy = jnp.concatenate([jnp.zeros((LANES, MLA_NOPE), F32), jnp.eye(LANES, MLA_ROPE, dtype=F32),
                            jnp.zeros((LANES, LANES - MLA_NOPE - MLA_ROPE), F32)], axis=1)
    bot = jnp.broadcast_to(copy[:, None, :], (LANES, MLA_HEADS, LANES))
    wk_ext = jnp.concatenate([top, bot], axis=0).reshape(2 * LANES, MLA_HEADS * LANES).astype(BF16)
    wv_t = wkv[:, :, MLA_NOPE:].reshape(kv_lora, MLA_HEADS * MLA_V).T.astype(BF16)

    t = jnp.arange(s)
    row = (t // GRID_W).astype(F32)
    col = (t % GRID_W).astype(F32)
    nf = MLA_ROPE // 4
    inv = ROPE_THETA ** (-jnp.arange(nf, dtype=F32) / nf)
    ang = jnp.concatenate([row[:, None] * inv, col[:, None] * inv], axis=-1)
    cos2 = jnp.concatenate([jnp.cos(ang), jnp.cos(ang)], axis=1)
    sin2 = jnp.concatenate([jnp.sin(ang), jnp.sin(ang)], axis=1)
    lpad = jnp.zeros((s, LANES - MLA_ROPE), F32)
    cos_tok = jnp.concatenate([cos2, lpad], axis=1)
    sin_tok = jnp.concatenate([sin2, lpad], axis=1)
    cos_t = cos2.T
    sin_t = sin2.T

    qscale = float(hd_q ** -0.5 * LOG2E)
    nq = MLA_HEADS * LANES
    nv = MLA_HEADS * MLA_V
    nd = wd_ext.shape[1]
    full = lambda shape: pl.BlockSpec(shape, lambda bi, i: (0,) * len(shape))
    vec = pl.BlockSpec((1, 1, d), lambda bi, i: (bi, 0, 0))
    return pl.pallas_call(
        functools.partial(_mla_proj_kernel, q_lora=q_lora, kv_lora=kv_lora, qscale=qscale),
        grid=(b, s // tm),
        in_specs=[
            pl.BlockSpec((1, tm, d), lambda bi, i: (bi, i, 0)),
            vec, vec,
            full((d, nd)), full((1, q_lora)), full((1, kv_lora)),
            full((nq, q_lora)), full((2 * LANES, nq)), full((nv, kv_lora)),
            pl.BlockSpec((tm, LANES), lambda bi, i: (i, 0)),
            pl.BlockSpec((tm, LANES), lambda bi, i: (i, 0)),
            pl.BlockSpec((MLA_ROPE, tm), lambda bi, i: (0, i)),
            pl.BlockSpec((MLA_ROPE, tm), lambda bi, i: (0, i)),
        ],
        out_specs=[
            pl.BlockSpec((1, nq, tm), lambda bi, i: (bi, 0, i)),
            pl.BlockSpec((1, tm, nq), lambda bi, i: (bi, i, 0)),
            pl.BlockSpec((1, nv, tm), lambda bi, i: (bi, 0, i)),
        ],
        out_shape=[
            jax.ShapeDtypeStruct((b, nq, s), BF16),
            jax.ShapeDtypeStruct((b, s, nq), BF16),
            jax.ShapeDtypeStruct((b, nv, s), BF16),
        ],
        compiler_params=_params("arbitrary", "arbitrary"),
        name="mla_proj",
    )(x, sh, sc, wd_ext, q_norm.reshape(1, -1), kv_norm.reshape(1, -1), wq_ext_t, wk_ext, wv_t,
      cos_tok, sin_tok, cos_t, sin_t)


def _mla_attn_kernel(q_ref, k_ref, v_ref, o_ref, *s_refs, tq, tk, nkt, nq):
    ones = jnp.ones((SUM_ROWS, tk), BF16)

    def scores(qi, kt):
        qt = q_ref[0, :, pl.ds(pl.multiple_of(qi * tq, tq), tq)]
        s_refs[kt % MLA_SCORE_SLOTS][...] = jnp.dot(k_ref[0, kt * tk:(kt + 1) * tk, :], qt,
                                                    preferred_element_type=F32)

    def softmax_pv(kt, carry):
        m, acc = carry
        st = s_refs[kt % MLA_SCORE_SLOTS][...]
        m_new = jnp.maximum(m, jnp.max(st, axis=0, keepdims=True))
        p = jnp.exp2(st - m_new).astype(BF16)
        vb = jnp.concatenate([v_ref[0, :, kt * tk:(kt + 1) * tk], ones], axis=0)
        acc = jnp.exp2(m - m_new) * acc + jnp.dot(vb, p, preferred_element_type=F32)
        return m_new, acc

    for kt in range(MLA_SCORE_AHEAD):
        scores(0, kt)

    def body(qi, carry):
        nxt = jnp.minimum(qi + 1, nq - 1)
        carry = (jnp.full((1, tq), NEG_BIG, F32), jnp.zeros((MLA_V + SUM_ROWS, tq), F32))
        for kt in range(nkt):
            ahead = kt + MLA_SCORE_AHEAD
            if ahead < nkt:
                scores(qi, ahead)
            else:
                scores(nxt, ahead - nkt)
            carry = softmax_pv(kt, carry)
        m, acc = carry
        out = acc[:MLA_V] / acc[MLA_V:MLA_V + 1]
        o_ref[0, :, pl.ds(pl.multiple_of(qi * tq, tq), tq)] = out.astype(o_ref.dtype)
        return 0

    lax.fori_loop(0, nq, body, 0)


def _mla_attention(q_t, k, v_t, tq, tk):
    b, _, s = q_t.shape
    nv = v_t.shape[1]
    nkt = s // tk
    assert nkt % MLA_SCORE_SLOTS == 0 and nkt >= MLA_SCORE_AHEAD
    return pl.pallas_call(
        functools.partial(_mla_attn_kernel, tq=tq, tk=tk, nkt=nkt, nq=s // tq),
        grid=(b, MLA_HEADS),
        in_specs=[
            pl.BlockSpec((1, LANES, s), lambda bi, hd: (bi, hd, 0)),
            pl.BlockSpec((1, s, LANES), lambda bi, hd: (bi, 0, hd)),
            pl.BlockSpec((1, MLA_V, s), lambda bi, hd: (bi, hd, 0)),
        ],
        out_specs=pl.BlockSpec((1, MLA_V, s), lambda bi, hd: (bi, hd, 0)),
        out_shape=jax.ShapeDtypeStruct((b, nv, s), BF16),
        scratch_shapes=[pltpu.VMEM((tk, tq), F32)] * MLA_SCORE_SLOTS,
        compiler_params=_params("arbitrary", "arbitrary"),
        name="mla_attn",
    )(q_t, k, v_t)


def _router_kernel(a_ref, wo_ref, x_ref, ga_ref, sh_ref, sc_ref, wr_ref, xo_ref, h_ref, i_ref, w_ref, *, n_exp):
    y = lax.dot_general(a_ref[0], wo_ref[...], (((0,), (0,)), ((), ())), preferred_element_type=F32)
    x = x_ref[0] + ga_ref[0] * y
    xo_ref[0] = x
    h = _modulate(x, sh_ref[0], sc_ref[0])
    h_ref[0] = h
    w = wr_ref[...]
    h_hi = h.astype(BF16)
    h_lo = (h - h_hi.astype(F32)).astype(BF16)
    w_hi = w.astype(BF16)
    w_lo = (w - w_hi.astype(F32)).astype(BF16)
    logits = (jnp.dot(h_hi, w_hi, preferred_element_type=F32)
              + (jnp.dot(h_lo, w_hi, preferred_element_type=F32)
                 + jnp.dot(h_hi, w_lo, preferred_element_type=F32)))
    lane = lax.broadcasted_iota(jnp.int32, logits.shape, 1)
    valid = lane < n_exp
    logits = jnp.where(valid, logits, NEG_BIG)
    e = jnp.exp(logits - jnp.max(logits, axis=-1, keepdims=True))
    probs = e / jnp.sum(e, axis=-1, keepdims=True)
    probs = jnp.where(valid, probs, -1.0)
    vals, idxs = [], []
    rest = probs
    for _ in range(TOP_K):
        v = jnp.max(rest, axis=-1, keepdims=True)
        idx = jnp.min(jnp.where(rest == v, lane, LANES), axis=-1, keepdims=True)
        vals.append(v)
        idxs.append(idx)
        rest = jnp.where(lane == idx, -1.0, rest)
    tot = vals[0]
    for v in vals[1:]:
        tot = tot + v
    i_out = jnp.zeros(probs.shape, jnp.int32)
    w_out = jnp.zeros_like(probs)
    for k in range(TOP_K):
        i_out = jnp.where(lane == k, idxs[k], i_out)
        w_out = jnp.where(lane == k, vals[k] / tot, w_out)
    i_ref[0] = i_out
    w_ref[0] = w_out


def _proj_router(o_t, w_o, x, g_a, sh, sc, w_router, tm):
    b, s, d = x.shape
    k = w_o.shape[0]
    n_exp = w_router.shape[1]
    wr = jnp.zeros((d, LANES), F32).at[:, :n_exp].set(w_router)
    vec = pl.BlockSpec((1, 1, d), lambda bi, i: (bi, 0, 0))
    tile = pl.BlockSpec((1, tm, d), lambda bi, i: (bi, i, 0))
    lanes_spec = pl.BlockSpec((1, tm, LANES), lambda bi, i: (bi, i, 0))
    return pl.pallas_call(
        functools.partial(_router_kernel, n_exp=n_exp),
        grid=(b, s // tm),
        in_specs=[
            pl.BlockSpec((1, k, tm), lambda bi, i: (bi, 0, i)),
            pl.BlockSpec((k, d), lambda bi, i: (0, 0)),
            tile,
            vec, vec, vec,
            pl.BlockSpec((d, LANES), lambda bi, i: (0, 0)),
        ],
        out_specs=[tile, tile, lanes_spec, lanes_spec],
        out_shape=[jax.ShapeDtypeStruct((b, s, d), F32),
                   jax.ShapeDtypeStruct((b, s, d), F32),
                   jax.ShapeDtypeStruct((b, s, LANES), jnp.int32),
                   jax.ShapeDtypeStruct((b, s, LANES), F32)],
        compiler_params=_params("arbitrary", "arbitrary"),
        name="proj_router",
    )(o_t, w_o, x, g_a, sh, sc, wr)


def _route_plan(idx, n_exp, tile):
    t, k = idx.shape
    e_flat = idx.reshape(t * k)
    onehot = (e_flat[:, None] == jnp.arange(n_exp, dtype=jnp.int32)[None, :]).astype(jnp.int32)
    csum = jnp.cumsum(onehot, axis=0)
    counts = csum[-1]
    rank = jnp.sum(onehot * csum, axis=1) - 1
    padded = ((counts + tile - 1) // tile) * tile
    gend = jnp.cumsum(padded)
    gstart = gend - padded
    dest = jnp.sum(onehot * gstart[None, :], axis=1) + rank
    p_rows = t * k + n_exp * tile
    tile_start = jnp.arange(p_rows // tile, dtype=jnp.int32) * tile
    tile_expert = jnp.minimum(jnp.sum((tile_start[:, None] >= gend[None, :]).astype(jnp.int32), axis=1),
                              n_exp - 1)
    tile_live = (tile_start < gend[-1]).astype(jnp.int32)
    return dest.astype(jnp.int32), tile_expert, tile_live, p_rows


def _row_scatter_kernel(dest_ref, h_ref, xs_in_ref, xs_ref, sem, *, tm):
    del xs_in_ref

    def row_copy(r, k):
        return pltpu.make_async_copy(h_ref.at[pl.ds(r, 1)], xs_ref.at[pl.ds(dest_ref[TOP_K * r + k], 1)], sem)

    def issue(r, carry):
        for k in range(TOP_K):
            row_copy(r, k).start(priority=k % 2)
        return carry

    lax.fori_loop(0, tm, issue, 0, unroll=4)
    for k in range(TOP_K):
        pltpu.make_async_copy(h_ref, xs_ref.at[pl.ds(0, tm)], sem).wait()


def _row_scatter(h, dest, p_rows, tm):
    n, d = h.shape
    return pl.pallas_call(
        functools.partial(_row_scatter_kernel, tm=tm),
        grid=(n // tm,),
        in_specs=[pl.BlockSpec((TOP_K * tm,), lambda i: (i,), memory_space=pltpu.SMEM),
                  pl.BlockSpec((tm, d), lambda i: (i, 0)),
                  pl.BlockSpec(memory_space=pl.ANY)],
        out_specs=pl.BlockSpec(memory_space=pl.ANY),
        out_shape=jax.ShapeDtypeStruct((p_rows, d), h.dtype),
        input_output_aliases={2: 0},
        scratch_shapes=[pltpu.SemaphoreType.DMA(())],
        compiler_params=_params("arbitrary"),
        name="moe_dispatch",
    )(dest, h, jnp.zeros((p_rows, d), h.dtype))


def _experts_kernel(te_ref, live_ref, x_ref, wgu_ref, wd_ref, o_ref, *, f):
    i = pl.program_id(0)

    @pl.when(live_ref[i] != 0)
    def _():
        o_ref[...] = _swiglu_tile(x_ref[...].astype(BF16), wgu_ref.at[0], wd_ref.at[0], f)

    @pl.when(live_ref[i] == 0)
    def _():
        o_ref[...] = jnp.zeros_like(o_ref)


def _experts(xs, tile_expert, tile_live, wgu, wd, tile):
    p_rows, d = xs.shape
    n_exp, f, _ = wd.shape
    grid_spec = pltpu.PrefetchScalarGridSpec(
        num_scalar_prefetch=2,
        grid=(p_rows // tile,),
        in_specs=[
            pl.BlockSpec((tile, d), lambda i, te, lv: (i, 0)),
            pl.BlockSpec((1, d, 2 * f), lambda i, te, lv: (te[i], 0, 0)),
            pl.BlockSpec((1, f, d), lambda i, te, lv: (te[i], 0, 0)),
        ],
        out_specs=pl.BlockSpec((tile, d), lambda i, te, lv: (i, 0)),
    )
    return pl.pallas_call(
        functools.partial(_experts_kernel, f=f),
        grid_spec=grid_spec,
        out_shape=jax.ShapeDtypeStruct((p_rows, d), F32),
        compiler_params=_params("arbitrary"),
        name="moe_experts",
    )(tile_expert, tile_live, xs, wgu, wd)


def _combine_kernel(dest_ref, x_ref, g_ref, fn_ref, w_ref, ys_ref, o_ref, buf_ref, sem, *, tm):
    def row_copy(r, k):
        return pltpu.make_async_copy(ys_ref.at[pl.ds(dest_ref[TOP_K * r + k], 1)],
                                     buf_ref.at[k, pl.ds(r, 1)], sem)

    def issue(r, carry):
        for k in range(TOP_K):
            row_copy(r, k).start(priority=k % 2)
        return carry

    lax.fori_loop(0, tm, issue, 0, unroll=4)
    for k in range(TOP_K):
        pltpu.make_async_copy(ys_ref.at[pl.ds(0, tm)], buf_ref.at[k], sem).wait()
    w = w_ref[...]
    y = w[:, 0:1] * buf_ref[0]
    for k in range(1, TOP_K):
        y = y + w[:, k:k + 1] * buf_ref[k]
    o_ref[...] = _rms(x_ref[...] + g_ref[0] * y) * fn_ref[...]


def _combine(x, g, final_norm, wts, ys, dest, tm):
    b, s, d = x.shape
    tiles_per_batch = s // tm
    return pl.pallas_call(
        functools.partial(_combine_kernel, tm=tm),
        grid=(b * tiles_per_batch,),
        in_specs=[
            pl.BlockSpec((TOP_K * tm,), lambda i: (i,), memory_space=pltpu.SMEM),
            pl.BlockSpec((tm, d), lambda i: (i, 0)),
            pl.BlockSpec((1, 1, d), lambda i: (i // tiles_per_batch, 0, 0)),
            pl.BlockSpec((1, d), lambda i: (0, 0)),
            pl.BlockSpec((tm, LANES), lambda i: (i, 0)),
            pl.BlockSpec(memory_space=pl.ANY),
        ],
        out_specs=pl.BlockSpec((tm, d), lambda i: (i, 0)),
        out_shape=jax.ShapeDtypeStruct((b * s, d), F32),
        scratch_shapes=[pltpu.VMEM((TOP_K, tm, d), F32), pltpu.SemaphoreType.DMA(())],
        compiler_params=_params("arbitrary"),
        name="moe_combine",
    )(dest, x.reshape(b * s, d), g, final_norm.reshape(1, d), wts.reshape(b * s, LANES), ys).reshape(b, s, d)


def kernel(x, c, w_ada, b_ada, na_w_qkv, na_rpb, na_w_o, ffn_w_gu, ffn_w_down, mla_w_down, mla_q_norm,
           mla_w_uq, mla_kv_norm, mla_w_ukv, mla_w_o, moe_w_router, moe_w_gu, moe_w_down, final_norm):
    b, s, d = x.shape
    mods = _adaln(c, w_ada, b_ada)

    def split(layer):
        return [v.reshape(b, 1, d) for v in jnp.split(mods[layer], 6, axis=-1)]

    tm = min(1024, s)
    tmh = min(512, s)

    sh_a, sc_a, g_a, sh_f, sc_f, g_f = split(0)
    qkv = _ln_matmul(x, sh_a, sc_a, na_w_qkv[0].astype(BF16), tmh)
    att = _neighborhood_attention(qkv, na_rpb[0])
    x = _proj_ffn(att, na_w_o[0].astype(BF16), x, g_a, sh_f, sc_f, g_f,
                  ffn_w_gu[0].astype(BF16), ffn_w_down[0].astype(BF16), tmh)

    sh_a, sc_a, g_a, sh_f, sc_f, g_f = split(1)
    q_t, k, v_t = _mla_proj(x, sh_a, sc_a, mla_w_down[0], mla_q_norm[0], mla_w_uq[0], mla_kv_norm[0],
                            mla_w_ukv[0], tmh)
    o_t = _mla_attention(q_t, k, v_t, min(512, s), 256)
    x, h, idx, wts = _proj_router(o_t, mla_w_o[0].astype(BF16), x, g_a, sh_f, sc_f, moe_w_router[0], tm)
    n_exp = moe_w_router.shape[-1]
    dest, tile_expert, tile_live, p_rows = _route_plan(idx[..., :TOP_K].reshape(b * s, TOP_K), n_exp, tmh)
    xs = _row_scatter(h.reshape(b * s, d), dest, p_rows, tmh)
    ys = _experts(xs, tile_expert, tile_live, moe_w_gu[0].astype(BF16), moe_w_down[0].astype(BF16), tmh)
    return _combine(x, g_f, final_norm, wts, ys, dest, tmh)
```

```python
import functools

import jax
import jax.numpy as jnp
from jax import lax
from jax.experimental import pallas as pl
from jax.experimental.pallas import tpu as pltpu

F32 = jnp.float32
BF16 = jnp.bfloat16
HIGHEST = lax.Precision.HIGHEST

GRID_W = 64
NA_HEADS = 16
NA_WIN_H = 8
NA_WIN_W = 16
NA_ROWS_PER_STEP = 8
MLA_HEADS = 16
MLA_NOPE = 64
MLA_ROPE = 32
MLA_V = 64
ROPE_THETA = 10000.0
TOP_K = 2
NORM_EPS = 1e-6

LANES = 128
MXU_DIM = 256
SUM_ROWS = 16
ROUTER_CHUNK = 256
MLA_SCORE_AHEAD = 2
MLA_SCORE_SLOTS = 4
VMEM_LIMIT = 56 * 1024 * 1024
NEG_BIG = -1e30
LOG2E = 1.4426950408889634


def _params(*sem):
    return pltpu.CompilerParams(dimension_semantics=sem, vmem_limit_bytes=VMEM_LIMIT)


def _rms(x):
    return x * lax.rsqrt(jnp.mean(x * x, axis=-1, keepdims=True) + NORM_EPS)


def _modulate(x, sh, sc):
    return _rms(x) * (1.0 + sc) + sh


def _adaln_kernel(c_ref, w_ref, b_ref, o_ref):
    c = c_ref[...]
    ca = c * jax.nn.sigmoid(c)
    o_ref[0] = jnp.dot(ca, w_ref[0], precision=HIGHEST, preferred_element_type=F32) + b_ref[0]


def _adaln(c, w_ada, b_ada):
    depth, d, n = w_ada.shape
    b = c.shape[0]
    rows = 8
    cp = jnp.zeros((rows, d), F32).at[:b].set(c)
    tn = 1536
    out = pl.pallas_call(
        _adaln_kernel,
        grid=(depth, n // tn),
        in_specs=[
            pl.BlockSpec((rows, d), lambda l, j: (0, 0)),
            pl.BlockSpec((1, d, tn), lambda l, j: (l, 0, j)),
            pl.BlockSpec((1, 1, tn), lambda l, j: (l, 0, j)),
        ],
        out_specs=pl.BlockSpec((1, rows, tn), lambda l, j: (l, 0, j)),
        out_shape=jax.ShapeDtypeStruct((depth, rows, n), F32),
        compiler_params=_params("arbitrary", "arbitrary"),
        name="adaln",
    )(cp, w_ada, b_ada.reshape(depth, 1, n))
    return out[:, :b]


def _ln_matmul_kernel(x_ref, sh_ref, sc_ref, w_ref, o_ref):
    h = _modulate(x_ref[0], sh_ref[0], sc_ref[0]).astype(BF16)
    o_ref[0] = jnp.dot(h, w_ref[...], preferred_element_type=F32).astype(o_ref.dtype)


def _ln_matmul(x, sh, sc, w, tm):
    b, s, d = x.shape
    n = w.shape[1]
    vec = pl.BlockSpec((1, 1, d), lambda bi, i: (bi, 0, 0))
    return pl.pallas_call(
        _ln_matmul_kernel,
        grid=(b, s // tm),
        in_specs=[
            pl.BlockSpec((1, tm, d), lambda bi, i: (bi, i, 0)),
            vec, vec,
            pl.BlockSpec((d, n), lambda bi, i: (0, 0)),
        ],
        out_specs=pl.BlockSpec((1, tm, n), lambda bi, i: (bi, i, 0)),
        out_shape=jax.ShapeDtypeStruct((b, s, n), BF16),
        compiler_params=_params("arbitrary", "arbitrary"),
        name="ln_matmul",
    )(x, sh, sc, w)


def _na_kernel(q_ref, k_ref, v_ref, t_ref, o_ref, *, rows):
    win = NA_WIN_H * GRID_W
    lane = lax.broadcasted_iota(jnp.int32, (GRID_W, LANES), 1)
    first = lane < (LANES // 2)
    scale = (LANES // 2) ** -0.5 * LOG2E

    def bias(pat):
        d0 = NA_WIN_H - 1 - pat
        return jnp.concatenate([t_ref[0, d0 + 2 * j] for j in range(NA_WIN_H // 2)], axis=1)

    def body(i, carry):
        rr = [i * NA_ROWS_PER_STEP + j for j in range(NA_ROWS_PER_STEP)]
        rss = [jnp.clip(r - NA_WIN_H // 2, 0, rows - NA_WIN_H) for r in rr]
        ss = []
        for r, rs in zip(rr, rss):
            q = q_ref[0, pl.ds(pl.multiple_of(r * GRID_W, GRID_W), GRID_W), :]
            zero = jnp.zeros_like(q)
            q2 = jnp.concatenate([jnp.where(first, q, zero), jnp.where(first, zero, q)], axis=0)
            kw = k_ref[0, pl.ds(pl.multiple_of(rs * GRID_W, GRID_W), win), :]
            s = lax.dot_general(q2, kw, (((1,), (1,)), ((), ())), preferred_element_type=F32)
            ss.append(s * scale + bias(r - rs))
        ps, ls = [], []
        for s in ss:
            p = jnp.exp2(s - jnp.max(s, axis=-1, keepdims=True))
            ls.append(jnp.sum(p, axis=-1, keepdims=True))
            ps.append(p.astype(BF16))
        for r, rs, p, l in zip(rr, rss, ps, ls):
            vw = v_ref[0, pl.ds(pl.multiple_of(rs * GRID_W, GRID_W), win), :]
            o = jnp.dot(p, vw, preferred_element_type=F32) / l
            o = jnp.where(first, o[:GRID_W], o[GRID_W:])
            o_ref[0, pl.ds(pl.multiple_of(r * GRID_W, GRID_W), GRID_W), :] = o.astype(o_ref.dtype)
        return carry

    lax.fori_loop(0, rows // NA_ROWS_PER_STEP, body, 0)


def _na_bias_table(rpb):
    h, ndr, ndc = rpb.shape
    qc = jnp.arange(GRID_W)
    kc = jnp.arange(GRID_W)
    cs = jnp.clip(qc - NA_WIN_W // 2, 0, GRID_W - NA_WIN_W)
    inwin = (kc[None, :] >= cs[:, None]) & (kc[None, :] < cs[:, None] + NA_WIN_W)
    period = 2 * GRID_W
    u = jnp.concatenate([rpb[:, :, NA_WIN_W - 1:], jnp.zeros((h, ndr, period - ndc), F32),
                         rpb[:, :, :NA_WIN_W - 1]], axis=-1)
    flat = jnp.tile(u, (1, 1, GRID_W))[:, :, :GRID_W * (period - 1)]
    col = flat.reshape(h, ndr, GRID_W, period - 1)[:, :, :, :GRID_W]
    col = jnp.where(inwin[None, None], col * LOG2E, NEG_BIG)
    pair = jnp.concatenate([col[:, :-1], col[:, 1:]], axis=-1)
    pair = pair.reshape(h // 2, 2, ndr - 1, GRID_W, 2 * GRID_W).transpose(0, 2, 1, 3, 4)
    return pair.reshape(h // 2, ndr - 1, 2 * GRID_W, 2 * GRID_W)


def _neighborhood_attention(qkv, rpb):
    b, s, d3 = qkv.shape
    d = d3 // 3
    rows = s // GRID_W
    assert rows >= NA_WIN_H and d // NA_HEADS == LANES // 2 and rows % NA_ROWS_PER_STEP == 0
    nslab = d // LANES
    table = _na_bias_table(rpb.astype(F32))
    win = NA_WIN_H * GRID_W
    return pl.pallas_call(
        functools.partial(_na_kernel, rows=rows),
        grid=(b, nslab),
        in_specs=[
            pl.BlockSpec((1, s, LANES), lambda bi, hp: (bi, 0, hp)),
            pl.BlockSpec((1, s, LANES), lambda bi, hp: (bi, 0, nslab + hp)),
            pl.BlockSpec((1, s, LANES), lambda bi, hp: (bi, 0, 2 * nslab + hp)),
            pl.BlockSpec((1,) + table.shape[1:], lambda bi, hp: (hp, 0, 0, 0)),
        ],
        out_specs=pl.BlockSpec((1, s, LANES), lambda bi, hp: (bi, 0, hp)),
        out_shape=jax.ShapeDtypeStruct((b, s, d), BF16),
        compiler_params=_params("arbitrary", "arbitrary"),
        name="na_attn",
    )(qkv, qkv, qkv, table)


def _swiglu_tile(h, wgu, wd, f):
    acc = None
    for c in range(f // MXU_DIM):
        lo = c * MXU_DIM
        g = jnp.dot(h, wgu[:, lo:lo + MXU_DIM], preferred_element_type=F32)
        u = jnp.dot(h, wgu[:, f + lo:f + lo + MXU_DIM], preferred_element_type=F32)
        a = (g * jax.nn.sigmoid(g) * u).astype(BF16)
        y = jnp.dot(a, wd[lo:lo + MXU_DIM, :], preferred_element_type=F32)
        acc = y if acc is None else acc + y
    return acc


def _ffn_kernel(a_ref, wo_ref, x_ref, ga_ref, sh_ref, sc_ref, g_ref, wgu_ref, wd_ref, o_ref, *, f):
    x = x_ref[0] + ga_ref[0] * jnp.dot(a_ref[0], wo_ref[...], preferred_element_type=F32)
    h = _modulate(x, sh_ref[0], sc_ref[0]).astype(BF16)
    o_ref[0] = x + g_ref[0] * _swiglu_tile(h, wgu_ref, wd_ref, f)


def _proj_ffn(att, w_o, x, g_a, sh, sc, g, wgu, wd, tm):
    b, s, d = x.shape
    f = wd.shape[0]
    vec = pl.BlockSpec((1, 1, d), lambda bi, i: (bi, 0, 0))
    tile = pl.BlockSpec((1, tm, d), lambda bi, i: (bi, i, 0))
    return pl.pallas_call(
        functools.partial(_ffn_kernel, f=f),
        grid=(b, s // tm),
        in_specs=[
            tile,
            pl.BlockSpec((d, d), lambda bi, i: (0, 0)),
            tile,
            vec, vec, vec, vec,
            pl.BlockSpec((d, 2 * f), lambda bi, i: (0, 0)),
            pl.BlockSpec((f, d), lambda bi, i: (0, 0)),
        ],
        out_specs=tile,
        out_shape=jax.ShapeDtypeStruct((b, s, d), F32),
        compiler_params=_params("arbitrary", "arbitrary"),
        name="proj_ffn",
    )(att, w_o, x, g_a, sh, sc, g, wgu, wd)


def _mla_proj_kernel(x_ref, sh_ref, sc_ref, wd_ref, qn_ref, kvn_ref, wq_ref, wk_ref, wv_ref,
                     cs_ref, sn_ref, cst_ref, snt_ref, q_ref, k_ref, v_ref, *, q_lora, kv_lora, qscale):
    h = _modulate(x_ref[0], sh_ref[0], sc_ref[0]).astype(BF16)
    down = jnp.dot(h, wd_ref[...], preferred_element_type=F32)
    cq = (_rms(down[:, :q_lora]) * qn_ref[...]).astype(BF16)
    ckv = (_rms(down[:, q_lora:q_lora + kv_lora]) * kvn_ref[...]).astype(BF16)
    r0 = q_lora + kv_lora
    kr = down[:, r0:r0 + LANES] * cs_ref[...] + down[:, r0 + LANES:r0 + 2 * LANES] * sn_ref[...]
    z = jnp.concatenate([ckv, kr.astype(BF16)], axis=1)
    k_ref[0] = jnp.dot(z, wk_ref[...], preferred_element_type=F32).astype(BF16)
    nt = (((1,), (1,)), ((), ()))
    v_ref[0] = lax.dot_general(wv_ref[...], ckv, nt, preferred_element_type=F32).astype(BF16)
    qt = lax.dot_general(wq_ref[...], cq, nt, preferred_element_type=F32)
    cst = cst_ref[...]
    snt = snt_ref[...]
    for hd in range(MLA_HEADS):
        base = hd * LANES
        blk = qt[base:base + LANES]
        rot = blk[MLA_NOPE:MLA_NOPE + MLA_ROPE] * cst + blk[MLA_NOPE + MLA_ROPE:] * snt
        q_ref[0, base:base + MLA_NOPE, :] = (blk[:MLA_NOPE] * qscale).astype(BF16)
        q_ref[0, base + MLA_NOPE:base + MLA_NOPE + MLA_ROPE, :] = (rot * qscale).astype(BF16)
        q_ref[0, base + MLA_NOPE + MLA_ROPE:base + LANES, :] = jnp.zeros(
            (LANES - MLA_NOPE - MLA_ROPE, rot.shape[1]), BF16)


def _rope_perm_weights(w_rope):
    x1 = w_rope[..., 0::2]
    x2 = w_rope[..., 1::2]
    return jnp.concatenate([x1, x2], axis=-1), jnp.concatenate([-x2, x1], axis=-1)


def _mla_proj(x, sh, sc, w_down, q_norm, w_uq, kv_norm, w_ukv, tm):
    b, s, d = x.shape
    q_lora = q_norm.shape[0]
    kv_lora = kv_norm.shape[0]
    hd_q = MLA_NOPE + MLA_ROPE
    assert hd_q <= LANES and MLA_ROPE % 2 == 0 and q_lora % LANES == 0 and kv_lora == LANES

    r_a, r_b = _rope_perm_weights(w_down[:, q_lora + kv_lora:])
    pad = jnp.zeros((d, LANES - MLA_ROPE), F32)
    wd_ext = jnp.concatenate([w_down[:, :q_lora + kv_lora], r_a, pad, r_b, pad], axis=1).astype(BF16)

    wq = w_uq.reshape(q_lora, MLA_HEADS, hd_q)
    ra, rb = _rope_perm_weights(wq[..., MLA_NOPE:])
    wq_ext = jnp.concatenate([wq[..., :MLA_NOPE], ra, rb], axis=-1)
    wq_ext_t = wq_ext.reshape(q_lora, MLA_HEADS * LANES).T.astype(BF16)

    wkv = w_ukv.reshape(kv_lora, MLA_HEADS, MLA_NOPE + MLA_V)
    top = jnp.concatenate([wkv[..., :MLA_NOPE], jnp.zeros((kv_lora, MLA_HEADS, LANES - MLA_NOPE), F32)], axis=-1)
    copy = jnp.concatenate([jnp.zeros((LANES, MLA_NOPE), F32), jnp.eye(LANES, MLA_ROPE, dtype=F32),
                            jnp.zeros((LANES, LANES - MLA_NOPE - MLA_ROPE), F32)], axis=1)
    bot = jnp.broadcast_to(copy[:, None, :], (LANES, MLA_HEADS, LANES))
    wk_ext = jnp.concatenate([top, bot], axis=0).reshape(2 * LANES, MLA_HEADS * LANES).astype(BF16)
    wv_t = wkv[:, :, MLA_NOPE:].reshape(kv_lora, MLA_HEADS * MLA_V).T.astype(BF16)

    t = jnp.arange(s)
    row = (t // GRID_W).astype(F32)
    col = (t % GRID_W).astype(F32)
    nf = MLA_ROPE // 4
    inv = ROPE_THETA ** (-jnp.arange(nf, dtype=F32) / nf)
    ang = jnp.concatenate([row[:, None] * inv, col[:, None] * inv], axis=-1)
    cos2 = jnp.concatenate([jnp.cos(ang), jnp.cos(ang)], axis=1)
    sin2 = jnp.concatenate([jnp.sin(ang), jnp.sin(ang)], axis=1)
    lpad = jnp.zeros((s, LANES - MLA_ROPE), F32)
    cos_tok = jnp.concatenate([cos2, lpad], axis=1)
    sin_tok = jnp.concatenate([sin2, lpad], axis=1)
    cos_t = cos2.T
    sin_t = sin2.T

    qscale = float(hd_q ** -0.5 * LOG2E)
    nq = MLA_HEADS * LANES
    nv = MLA_HEADS * MLA_V
    nd = wd_ext.shape[1]
    full = lambda shape: pl.BlockSpec(shape, lambda bi, i: (0,) * len(shape))
    vec = pl.BlockSpec((1, 1, d), lambda bi, i: (bi, 0, 0))
    return pl.pallas_call(
        functools.partial(_mla_proj_kernel, q_lora=q_lora, kv_lora=kv_lora, qscale=qscale),
        grid=(b, s // tm),
        in_specs=[
            pl.BlockSpec((1, tm, d), lambda bi, i: (bi, i, 0)),
            vec, vec,
            full((d, nd)), full((1, q_lora)), full((1, kv_lora)),
            full((nq, q_lora)), full((2 * LANES, nq)), full((nv, kv_lora)),
            pl.BlockSpec((tm, LANES), lambda bi, i: (i, 0)),
            pl.BlockSpec((tm, LANES), lambda bi, i: (i, 0)),
            pl.BlockSpec((MLA_ROPE, tm), lambda bi, i: (0, i)),
            pl.BlockSpec((MLA_ROPE, tm), lambda bi, i: (0, i)),
        ],
        out_specs=[
            pl.BlockSpec((1, nq, tm), lambda bi, i: (bi, 0, i)),
            pl.BlockSpec((1, tm, nq), lambda bi, i: (bi, i, 0)),
            pl.BlockSpec((1, nv, tm), lambda bi, i: (bi, 0, i)),
        ],
        out_shape=[
            jax.ShapeDtypeStruct((b, nq, s), BF16),
            jax.ShapeDtypeStruct((b, s, nq), BF16),
            jax.ShapeDtypeStruct((b, nv, s), BF16),
        ],
        compiler_params=_params("arbitrary", "arbitrary"),
        name="mla_proj",
    )(x, sh, sc, wd_ext, q_norm.reshape(1, -1), kv_norm.reshape(1, -1), wq_ext_t, wk_ext, wv_t,
      cos_tok, sin_tok, cos_t, sin_t)


def _mla_attn_kernel(q_ref, k_ref, v_ref, o_ref, *s_refs, tq, tk, nkt, nq):
    ones = jnp.ones((SUM_ROWS, tk), BF16)

    def scores(qi, kt):
        qt = q_ref[0, :, pl.ds(pl.multiple_of(qi * tq, tq), tq)]
        s_refs[kt % MLA_SCORE_SLOTS][...] = jnp.dot(k_ref[0, kt * tk:(kt + 1) * tk, :], qt,
                                                    preferred_element_type=F32)

    def softmax_pv(kt, carry):
        m, acc = carry
        st = s_refs[kt % MLA_SCORE_SLOTS][...]
        m_new = jnp.maximum(m, jnp.max(st, axis=0, keepdims=True))
        p = jnp.exp2(st - m_new).astype(BF16)
        vb = jnp.concatenate([v_ref[0, :, kt * tk:(kt + 1) * tk], ones], axis=0)
        acc = jnp.exp2(m - m_new) * acc + jnp.dot(vb, p, preferred_element_type=F32)
        return m_new, acc

    for kt in range(MLA_SCORE_AHEAD):
        scores(0, kt)

    def body(qi, carry):
        nxt = jnp.minimum(qi + 1, nq - 1)
        carry = (jnp.full((1, tq), NEG_BIG, F32), jnp.zeros((MLA_V + SUM_ROWS, tq), F32))
        for kt in range(nkt):
            ahead = kt + MLA_SCORE_AHEAD
            if ahead < nkt:
                scores(qi, ahead)
            else:
                scores(nxt, ahead - nkt)
            carry = softmax_pv(kt, carry)
        m, acc = carry
        out = acc[:MLA_V] / acc[MLA_V:MLA_V + 1]
        o_ref[0, :, pl.ds(pl.multiple_of(qi * tq, tq), tq)] = out.astype(o_ref.dtype)
        return 0

    lax.fori_loop(0, nq, body, 0)


def _mla_attention(q_t, k, v_t, tq, tk):
    b, _, s = q_t.shape
    nv = v_t.shape[1]
    nkt = s // tk
    assert nkt % MLA_SCORE_SLOTS == 0 and nkt >= MLA_SCORE_AHEAD
    return pl.pallas_call(
        functools.partial(_mla_attn_kernel, tq=tq, tk=tk, nkt=nkt, nq=s // tq),
        grid=(b, MLA_HEADS),
        in_specs=[
            pl.BlockSpec((1, LANES, s), lambda bi, hd: (bi, hd, 0)),
            pl.BlockSpec((1, s, LANES), lambda bi, hd: (bi, 0, hd)),
            pl.BlockSpec((1, MLA_V, s), lambda bi, hd: (bi, hd, 0)),
        ],
        out_specs=pl.BlockSpec((1, MLA_V, s), lambda bi, hd: (bi, hd, 0)),
        out_shape=jax.ShapeDtypeStruct((b, nv, s), BF16),
        scratch_shapes=[pltpu.VMEM((tk, tq), F32)] * MLA_SCORE_SLOTS,
        compiler_params=_params("arbitrary", "arbitrary"),
        name="mla_attn",
    )(q_t, k, v_t)


def _router_kernel(a_ref, wo_ref, x_ref, ga_ref, sh_ref, sc_ref, wr_ref, xo_ref, h_ref, i_ref, w_ref, *, n_exp):
    tm = x_ref.shape[1]
    chunks = [slice(c, c + ROUTER_CHUNK) for c in range(0, tm, ROUTER_CHUNK)]
    w = wr_ref[...]
    w_hi = w.astype(BF16)
    w_lo = (w - w_hi.astype(F32)).astype(BF16)
    lane = lax.broadcasted_iota(jnp.int32, (ROUTER_CHUNK, LANES), 1)
    valid = lane < n_exp

    hs = []
    for rows in chunks:
        y = lax.dot_general(a_ref[0, :, rows], wo_ref[...], (((0,), (0,)), ((), ())),
                            preferred_element_type=F32)
        x = x_ref[0, rows] + ga_ref[0] * y
        xo_ref[0, rows] = x
        h = _modulate(x, sh_ref[0], sc_ref[0])
        h_ref[0, rows] = h
        hs.append(h)

    all_logits = []
    for h in hs:
        h_hi = h.astype(BF16)
        h_lo = (h - h_hi.astype(F32)).astype(BF16)
        all_logits.append(jnp.dot(h_hi, w_hi, preferred_element_type=F32)
                          + (jnp.dot(h_lo, w_hi, preferred_element_type=F32)
                             + jnp.dot(h_hi, w_lo, preferred_element_type=F32)))

    rests = [jnp.where(valid, logits, -jnp.inf) for logits in all_logits]
    vals = [[] for _ in chunks]
    idxs = [[] for _ in chunks]
    for _ in range(TOP_K):
        for c in range(len(chunks)):
            v = jnp.max(rests[c], axis=-1, keepdims=True)
            idx = jnp.min(jnp.where(rests[c] == v, lane, LANES), axis=-1, keepdims=True)
            vals[c].append(v)
            idxs[c].append(idx)
            rests[c] = jnp.where(lane == idx, -jnp.inf, rests[c])
    for c, rows in enumerate(chunks):
        ev = [jnp.exp(v - vals[c][0]) for v in vals[c]]
        tot = ev[0]
        for v in ev[1:]:
            tot = tot + v
        i_out = jnp.zeros((ROUTER_CHUNK, LANES), jnp.int32)
        w_out = jnp.zeros((ROUTER_CHUNK, LANES), F32)
        for k in range(TOP_K):
            i_out = jnp.where(lane == k, idxs[c][k], i_out)
            w_out = jnp.where(lane == k, ev[k] / tot, w_out)
        i_ref[0, rows] = i_out
        w_ref[0, rows] = w_out


def _proj_router(o_t, w_o, x, g_a, sh, sc, w_router, tm):
    b, s, d = x.shape
    k = w_o.shape[0]
    n_exp = w_router.shape[1]
    wr = jnp.zeros((d, LANES), F32).at[:, :n_exp].set(w_router)
    vec = pl.BlockSpec((1, 1, d), lambda bi, i: (bi, 0, 0))
    tile = pl.BlockSpec((1, tm, d), lambda bi, i: (bi, i, 0))
    lanes_spec = pl.BlockSpec((1, tm, LANES), lambda bi, i: (bi, i, 0))
    return pl.pallas_call(
        functools.partial(_router_kernel, n_exp=n_exp),
        grid=(b, s // tm),
        in_specs=[
            pl.BlockSpec((1, k, tm), lambda bi, i: (bi, 0, i)),
            pl.BlockSpec((k, d), lambda bi, i: (0, 0)),
            tile,
            vec, vec, vec,
            pl.BlockSpec((d, LANES), lambda bi, i: (0, 0)),
        ],
        out_specs=[tile, tile, lanes_spec, lanes_spec],
        out_shape=[jax.ShapeDtypeStruct((b, s, d), F32),
                   jax.ShapeDtypeStruct((b, s, d), F32),
                   jax.ShapeDtypeStruct((b, s, LANES), jnp.int32),
                   jax.ShapeDtypeStruct((b, s, LANES), F32)],
        compiler_params=_params("arbitrary", "arbitrary"),
        name="proj_router",
    )(o_t, w_o, x, g_a, sh, sc, wr)


def _route_plan(idx, n_exp, tile):
    t, k = idx.shape
    e_flat = idx.reshape(t * k)
    onehot = (e_flat[:, None] == jnp.arange(n_exp, dtype=jnp.int32)[None, :]).astype(jnp.int32)
    csum = jnp.cumsum(onehot, axis=0)
    counts = csum[-1]
    rank = jnp.sum(onehot * csum, axis=1) - 1
    padded = ((counts + tile - 1) // tile) * tile
    gend = jnp.cumsum(padded)
    gstart = gend - padded
    dest = jnp.sum(onehot * gstart[None, :], axis=1) + rank
    p_rows = t * k + n_exp * tile
    tile_start = jnp.arange(p_rows // tile, dtype=jnp.int32) * tile
    tile_expert = jnp.minimum(jnp.sum((tile_start[:, None] >= gend[None, :]).astype(jnp.int32), axis=1),
                              n_exp - 1)
    tile_live = (tile_start < gend[-1]).astype(jnp.int32)
    return dest.astype(jnp.int32), tile_expert, tile_live, p_rows


def _row_scatter_kernel(dest_ref, h_ref, xs_in_ref, xs_ref, sem, *, tm):
    del xs_in_ref

    def row_copy(r, k):
        return pltpu.make_async_copy(h_ref.at[pl.ds(r, 1)], xs_ref.at[pl.ds(dest_ref[TOP_K * r + k], 1)], sem)

    def issue(r, carry):
        for k in range(TOP_K):
            row_copy(r, k).start(priority=k % 2)
        return carry

    lax.fori_loop(0, tm, issue, 0, unroll=4)
    for k in range(TOP_K):
        pltpu.make_async_copy(h_ref, xs_ref.at[pl.ds(0, tm)], sem).wait()


def _row_scatter(h, dest, p_rows, tm):
    n, d = h.shape
    return pl.pallas_call(
        functools.partial(_row_scatter_kernel, tm=tm),
        grid=(n // tm,),
        in_specs=[pl.BlockSpec((TOP_K * tm,), lambda i: (i,), memory_space=pltpu.SMEM),
                  pl.BlockSpec((tm, d), lambda i: (i, 0)),
                  pl.BlockSpec(memory_space=pl.ANY)],
        out_specs=pl.BlockSpec(memory_space=pl.ANY),
        out_shape=jax.ShapeDtypeStruct((p_rows, d), h.dtype),
        input_output_aliases={2: 0},
        scratch_shapes=[pltpu.SemaphoreType.DMA(())],
        compiler_params=_params("arbitrary"),
        name="moe_dispatch",
    )(dest, h, jnp.zeros((p_rows, d), h.dtype))


def _experts_kernel(te_ref, live_ref, x_ref, wgu_ref, wd_ref, o_ref, *, f):
    i = pl.program_id(0)

    @pl.when(live_ref[i] != 0)
    def _():
        o_ref[...] = _swiglu_tile(x_ref[...].astype(BF16), wgu_ref.at[0], wd_ref.at[0], f)

    @pl.when(live_ref[i] == 0)
    def _():
        o_ref[...] = jnp.zeros_like(o_ref)


def _experts(xs, tile_expert, tile_live, wgu, wd, tile):
    p_rows, d = xs.shape
    n_exp, f, _ = wd.shape
    grid_spec = pltpu.PrefetchScalarGridSpec(
        num_scalar_prefetch=2,
        grid=(p_rows // tile,),
        in_specs=[
            pl.BlockSpec((tile, d), lambda i, te, lv: (i, 0)),
            pl.BlockSpec((1, d, 2 * f), lambda i, te, lv: (te[i], 0, 0)),
            pl.BlockSpec((1, f, d), lambda i, te, lv: (te[i], 0, 0)),
        ],
        out_specs=pl.BlockSpec((tile, d), lambda i, te, lv: (i, 0)),
    )
    return pl.pallas_call(
        functools.partial(_experts_kernel, f=f),
        grid_spec=grid_spec,
        out_shape=jax.ShapeDtypeStruct((p_rows, d), F32),
        compiler_params=_params("arbitrary"),
        name="moe_experts",
    )(tile_expert, tile_live, xs, wgu, wd)


def _combine_kernel(dest_ref, x_ref, g_ref, fn_ref, w_ref, ys_ref, o_ref, buf_ref, sem, *, tm):
    def row_copy(r, k):
        return pltpu.make_async_copy(ys_ref.at[pl.ds(dest_ref[TOP_K * r + k], 1)],
                                     buf_ref.at[k, pl.ds(r, 1)], sem)

    def issue(r, carry):
        for k in range(TOP_K):
            row_copy(r, k).start(priority=k % 2)
        return carry

    lax.fori_loop(0, tm, issue, 0, unroll=4)
    for k in range(TOP_K):
        pltpu.make_async_copy(ys_ref.at[pl.ds(0, tm)], buf_ref.at[k], sem).wait()
    w = w_ref[...]
    y = w[:, 0:1] * buf_ref[0]
    for k in range(1, TOP_K):
        y = y + w[:, k:k + 1] * buf_ref[k]
    o_ref[...] = _rms(x_ref[...] + g_ref[0] * y) * fn_ref[...]


def _combine(x, g, final_norm, wts, ys, dest, tm):
    b, s, d = x.shape
    tiles_per_batch = s // tm
    return pl.pallas_call(
        functools.partial(_combine_kernel, tm=tm),
        grid=(b * tiles_per_batch,),
        in_specs=[
            pl.BlockSpec((TOP_K * tm,), lambda i: (i,), memory_space=pltpu.SMEM),
            pl.BlockSpec((tm, d), lambda i: (i, 0)),
            pl.BlockSpec((1, 1, d), lambda i: (i // tiles_per_batch, 0, 0)),
            pl.BlockSpec((1, d), lambda i: (0, 0)),
            pl.BlockSpec((tm, LANES), lambda i: (i, 0)),
            pl.BlockSpec(memory_space=pl.ANY),
        ],
        out_specs=pl.BlockSpec((tm, d), lambda i: (i, 0)),
        out_shape=jax.ShapeDtypeStruct((b * s, d), F32),
        scratch_shapes=[pltpu.VMEM((TOP_K, tm, d), F32), pltpu.SemaphoreType.DMA(())],
        compiler_params=_params("arbitrary"),
        name="moe_combine",
    )(dest, x.reshape(b * s, d), g, final_norm.reshape(1, d), wts.reshape(b * s, LANES), ys).reshape(b, s, d)


def kernel(x, c, w_ada, b_ada, na_w_qkv, na_rpb, na_w_o, ffn_w_gu, ffn_w_down, mla_w_down, mla_q_norm,
           mla_w_uq, mla_kv_norm, mla_w_ukv, mla_w_o, moe_w_router, moe_w_gu, moe_w_down, final_norm):
    b, s, d = x.shape
    mods = _adaln(c, w_ada, b_ada)

    def split(layer):
        return [v.reshape(b, 1, d) for v in jnp.split(mods[layer], 6, axis=-1)]

    tm = min(1024, s)
    tmh = min(512, s)

    sh_a, sc_a, g_a, sh_f, sc_f, g_f = split(0)
    qkv = _ln_matmul(x, sh_a, sc_a, na_w_qkv[0].astype(BF16), tmh)
    att = _neighborhood_attention(qkv, na_rpb[0])
    x = _proj_ffn(att, na_w_o[0].astype(BF16), x, g_a, sh_f, sc_f, g_f,
                  ffn_w_gu[0].astype(BF16), ffn_w_down[0].astype(BF16), tmh)

    sh_a, sc_a, g_a, sh_f, sc_f, g_f = split(1)
    q_t, k, v_t = _mla_proj(x, sh_a, sc_a, mla_w_down[0], mla_q_norm[0], mla_w_uq[0], mla_kv_norm[0],
                            mla_w_ukv[0], tmh)
    o_t = _mla_attention(q_t, k, v_t, min(512, s), 256)
    x, h, idx, wts = _proj_router(o_t, mla_w_o[0].astype(BF16), x, g_a, sh_f, sc_f, moe_w_router[0], tm)
    n_exp = moe_w_router.shape[-1]
    dest, tile_expert, tile_live, p_rows = _route_plan(idx[..., :TOP_K].reshape(b * s, TOP_K), n_exp, tmh)
    xs = _row_scatter(h.reshape(b * s, d), dest, p_rows, tmh)
    ys = _experts(xs, tile_expert, tile_live, moe_w_gu[0].astype(BF16), moe_w_down[0].astype(BF16), tmh)
    return _combine(x, g_f, final_norm, wts, ys, dest, tmh)
```

```python
import functools
from typing import NamedTuple

import jax
import jax.numpy as jnp
from jax import lax
from jax.experimental import pallas as pl
from jax.experimental.pallas import tpu as pltpu

F32 = jnp.float32
BF16 = jnp.bfloat16

GRID_W = 64
NA_HEADS = 16
NA_WIN_H = 8
NA_WIN_W = 16
MLA_HEADS = 16
MLA_NOPE = 64
MLA_ROPE = 32
MLA_V = 64
ROPE_THETA = 10000.0
TOP_K = 2
NORM_EPS = 1e-6

LANES = 128
SUBLANES = 8
MXU_DIM = 256
SUM_ROWS = 2 * SUBLANES
VMEM_LIMIT = 56 * 1024 * 1024

NA_ROWS_PER_STEP = 8
ROUTER_CHUNK = 256
MLA_SCORE_AHEAD = 2
MLA_SCORE_SLOTS = 4
ADALN_COLS = 1536
NEG_BIG = -1e30
LOG2E = 1.4426950408889634


def _params(*sem):
    return pltpu.CompilerParams(dimension_semantics=sem, vmem_limit_bytes=VMEM_LIMIT)


class _Tiles(NamedTuple):
    token: int
    router: int
    mla_q: int
    mla_k: int


def _tiles(s):
    return _Tiles(token=min(512, s), router=min(1024, s), mla_q=min(512, s), mla_k=256)


def _rms(x):
    return x * lax.rsqrt(jnp.mean(x * x, axis=-1, keepdims=True) + NORM_EPS)


def _modulate(x, sh, sc):
    return _rms(x) * (1.0 + sc) + sh


def _split_bf16(a):
    hi = a.astype(BF16)
    return hi, (a - hi.astype(F32)).astype(BF16)


def _dot_3pass(a, b):
    a_hi, a_lo = _split_bf16(a)
    b_hi, b_lo = _split_bf16(b)
    return (jnp.dot(a_hi, b_hi, preferred_element_type=F32)
            + (jnp.dot(a_lo, b_hi, preferred_element_type=F32)
               + jnp.dot(a_hi, b_lo, preferred_element_type=F32)))


def _adaln_kernel(c_ref, w_ref, b_ref, o_ref):
    c = c_ref[...]
    ca = c * jax.nn.sigmoid(c)
    o_ref[0] = _dot_3pass(ca, w_ref[0]) + b_ref[0]


def _adaln(c, w_ada, b_ada):
    depth, d, n = w_ada.shape
    b = c.shape[0]
    rows = SUBLANES
    assert b <= rows
    cp = jnp.zeros((rows, d), F32).at[:b].set(c)
    tn = ADALN_COLS
    out = pl.pallas_call(
        _adaln_kernel,
        grid=(depth, n // tn),
        in_specs=[
            pl.BlockSpec((rows, d), lambda l, j: (0, 0)),
            pl.BlockSpec((1, d, tn), lambda l, j: (l, 0, j)),
            pl.BlockSpec((1, 1, tn), lambda l, j: (l, 0, j)),
        ],
        out_specs=pl.BlockSpec((1, rows, tn), lambda l, j: (l, 0, j)),
        out_shape=jax.ShapeDtypeStruct((depth, rows, n), F32),
        compiler_params=_params("arbitrary", "arbitrary"),
        name="adaln",
    )(cp, w_ada, b_ada.reshape(depth, 1, n))
    return out[:, :b]


def _ln_matmul_kernel(x_ref, sh_ref, sc_ref, w_ref, o_ref):
    h = _modulate(x_ref[0], sh_ref[0], sc_ref[0]).astype(BF16)
    o_ref[0] = jnp.dot(h, w_ref[...], preferred_element_type=F32).astype(o_ref.dtype)


def _ln_matmul(x, sh, sc, w, tm):
    b, s, d = x.shape
    n = w.shape[1]
    vec = pl.BlockSpec((1, 1, d), lambda bi, i: (bi, 0, 0))
    return pl.pallas_call(
        _ln_matmul_kernel,
        grid=(b, s // tm),
        in_specs=[
            pl.BlockSpec((1, tm, d), lambda bi, i: (bi, i, 0)),
            vec, vec,
            pl.BlockSpec((d, n), lambda bi, i: (0, 0)),
        ],
        out_specs=pl.BlockSpec((1, tm, n), lambda bi, i: (bi, i, 0)),
        out_shape=jax.ShapeDtypeStruct((b, s, n), BF16),
        compiler_params=_params("arbitrary", "arbitrary"),
        name="ln_matmul",
    )(x, sh, sc, w)


def _na_kernel(q_ref, k_ref, v_ref, t_ref, o_ref, *, rows):
    win = NA_WIN_H * GRID_W
    lane = lax.broadcasted_iota(jnp.int32, (GRID_W, LANES), 1)
    first = lane < (LANES // 2)
    scale = (LANES // 2) ** -0.5 * LOG2E

    def bias(pat):
        d0 = NA_WIN_H - 1 - pat
        return jnp.concatenate(
            [jnp.concatenate([t_ref[0, d0 + 2 * j], t_ref[1, d0 + 2 * j]], axis=0)
             for j in range(NA_WIN_H // 2)], axis=1)

    def body(i, carry):
        rr = [i * NA_ROWS_PER_STEP + j for j in range(NA_ROWS_PER_STEP)]
        rss = [jnp.clip(r - NA_WIN_H // 2, 0, rows - NA_WIN_H) for r in rr]
        ss = []
        for r, rs in zip(rr, rss):
            q = q_ref[0, pl.ds(pl.multiple_of(r * GRID_W, GRID_W), GRID_W), :]
            zero = jnp.zeros_like(q)
            q2 = jnp.concatenate([jnp.where(first, q, zero), jnp.where(first, zero, q)], axis=0)
            kw = k_ref[0, pl.ds(pl.multiple_of(rs * GRID_W, GRID_W), win), :]
            s = lax.dot_general(q2, kw, (((1,), (1,)), ((), ())), preferred_element_type=F32)
            ss.append(s * scale + bias(r - rs))
        ps, ls = [], []
        for s in ss:
            p = jnp.exp2(s - jnp.max(s, axis=-1, keepdims=True))
            ls.append(jnp.sum(p, axis=-1, keepdims=True))
            ps.append(p.astype(BF16))
        for r, rs, p, l in zip(rr, rss, ps, ls):
            vw = v_ref[0, pl.ds(pl.multiple_of(rs * GRID_W, GRID_W), win), :]
            o = jnp.dot(p, vw, preferred_element_type=F32) / l
            o = jnp.where(first, o[:GRID_W], o[GRID_W:])
            o_ref[0, pl.ds(pl.multiple_of(r * GRID_W, GRID_W), GRID_W), :] = o.astype(o_ref.dtype)
        return carry

    lax.fori_loop(0, rows // NA_ROWS_PER_STEP, body, 0)


def _na_bias_table(rpb):
    h, ndr, ndc = rpb.shape
    qc = jnp.arange(GRID_W)
    kc = jnp.arange(GRID_W)
    cs = jnp.clip(qc - NA_WIN_W // 2, 0, GRID_W - NA_WIN_W)
    inwin = (kc[None, :] >= cs[:, None]) & (kc[None, :] < cs[:, None] + NA_WIN_W)
    period = 2 * GRID_W
    u = jnp.concatenate([rpb[:, :, NA_WIN_W - 1:], jnp.zeros((h, ndr, period - ndc), F32),
                         rpb[:, :, :NA_WIN_W - 1]], axis=-1)
    flat = jnp.tile(u, (1, 1, GRID_W))[:, :, :GRID_W * (period - 1)]
    col = flat.reshape(h, ndr, GRID_W, period - 1)[:, :, :, :GRID_W]
    col = jnp.where(inwin[None, None], col * LOG2E, NEG_BIG)
    return jnp.concatenate([col[:, :-1], col[:, 1:]], axis=-1)


def _neighborhood_attention(qkv, rpb):
    b, s, d3 = qkv.shape
    d = d3 // 3
    rows = s // GRID_W
    assert rows >= NA_WIN_H and d // NA_HEADS == LANES // 2 and rows % NA_ROWS_PER_STEP == 0
    nslab = d // LANES
    table = _na_bias_table(rpb.astype(F32))
    win = NA_WIN_H * GRID_W
    return pl.pallas_call(
        functools.partial(_na_kernel, rows=rows),
        grid=(b, nslab),
        in_specs=[
            pl.BlockSpec((1, s, LANES), lambda bi, hp: (bi, 0, hp)),
            pl.BlockSpec((1, s, LANES), lambda bi, hp: (bi, 0, nslab + hp)),
            pl.BlockSpec((1, s, LANES), lambda bi, hp: (bi, 0, 2 * nslab + hp)),
            pl.BlockSpec((2,) + table.shape[1:], lambda bi, hp: (hp, 0, 0, 0)),
        ],
        out_specs=pl.BlockSpec((1, s, LANES), lambda bi, hp: (bi, 0, hp)),
        out_shape=jax.ShapeDtypeStruct((b, s, d), BF16),
        compiler_params=_params("arbitrary", "arbitrary"),
        name="na_attn",
    )(qkv, qkv, qkv, table)


def _swiglu_tile(h, wgu, wd, f):
    acc = None
    for c in range(f // MXU_DIM):
        lo = c * MXU_DIM
        g = jnp.dot(h, wgu[:, lo:lo + MXU_DIM], preferred_element_type=F32)
        u = jnp.dot(h, wgu[:, f + lo:f + lo + MXU_DIM], preferred_element_type=F32)
        a = (g * jax.nn.sigmoid(g) * u).astype(BF16)
        y = jnp.dot(a, wd[lo:lo + MXU_DIM, :], preferred_element_type=F32)
        acc = y if acc is None else acc + y
    return acc


def _ffn_kernel(a_ref, wo_ref, x_ref, ga_ref, sh_ref, sc_ref, g_ref, wgu_ref, wd_ref, o_ref, *, f):
    x = x_ref[0] + ga_ref[0] * jnp.dot(a_ref[0], wo_ref[...], preferred_element_type=F32)
    h = _modulate(x, sh_ref[0], sc_ref[0]).astype(BF16)
    o_ref[0] = x + g_ref[0] * _swiglu_tile(h, wgu_ref, wd_ref, f)


def _proj_ffn(att, w_o, x, g_a, sh, sc, g, wgu, wd, tm):
    b, s, d = x.shape
    f = wd.shape[0]
    vec = pl.BlockSpec((1, 1, d), lambda bi, i: (bi, 0, 0))
    tile = pl.BlockSpec((1, tm, d), lambda bi, i: (bi, i, 0))
    return pl.pallas_call(
        functools.partial(_ffn_kernel, f=f),
        grid=(b, s // tm),
        in_specs=[
            tile,
            pl.BlockSpec((d, d), lambda bi, i: (0, 0)),
            tile,
            vec, vec, vec, vec,
            pl.BlockSpec((d, 2 * f), lambda bi, i: (0, 0)),
            pl.BlockSpec((f, d), lambda bi, i: (0, 0)),
        ],
        out_specs=tile,
        out_shape=jax.ShapeDtypeStruct((b, s, d), F32),
        compiler_params=_params("arbitrary", "arbitrary"),
        name="proj_ffn",
    )(att, w_o, x, g_a, sh, sc, g, wgu, wd)


def _mla_proj_kernel(x_ref, sh_ref, sc_ref, wd_ref, qn_ref, kvn_ref, wq_ref, wk_ref, wv_ref,
                     cs_ref, sn_ref, cst_ref, snt_ref, q_ref, k_ref, v_ref, *, q_lora, kv_lora, qscale):
    h = _modulate(x_ref[0], sh_ref[0], sc_ref[0]).astype(BF16)
    down = jnp.dot(h, wd_ref[...], preferred_element_type=F32)
    cq = (_rms(down[:, :q_lora]) * qn_ref[...]).astype(BF16)
    ckv = (_rms(down[:, q_lora:q_lora + kv_lora]) * kvn_ref[...]).astype(BF16)
    r0 = q_lora + kv_lora
    kr = down[:, r0:r0 + LANES] * cs_ref[...] + down[:, r0 + LANES:r0 + 2 * LANES] * sn_ref[...]
    z = jnp.concatenate([ckv, kr.astype(BF16)], axis=1)
    k_ref[0] = jnp.dot(z, wk_ref[...], preferred_element_type=F32).astype(BF16)
    nt = (((1,), (1,)), ((), ()))
    v_ref[0] = lax.dot_general(wv_ref[...], ckv, nt, preferred_element_type=F32).astype(BF16)
    qt = lax.dot_general(wq_ref[...], cq, nt, preferred_element_type=F32)
    cst = cst_ref[...]
    snt = snt_ref[...]
    for hd in range(MLA_HEADS):
        base = hd * LANES
        blk = qt[base:base + LANES]
        rot = blk[MLA_NOPE:MLA_NOPE + MLA_ROPE] * cst + blk[MLA_NOPE + MLA_ROPE:] * snt
        q_ref[0, base:base + MLA_NOPE, :] = (blk[:MLA_NOPE] * qscale).astype(BF16)
        q_ref[0, base + MLA_NOPE:base + MLA_NOPE + MLA_ROPE, :] = (rot * qscale).astype(BF16)
        q_ref[0, base + MLA_NOPE + MLA_ROPE:base + LANES, :] = jnp.zeros(
            (LANES - MLA_NOPE - MLA_ROPE, rot.shape[1]), BF16)


def _rope_perm_weights(w_rope):
    x1 = w_rope[..., 0::2]
    x2 = w_rope[..., 1::2]
    return jnp.concatenate([x1, x2], axis=-1), jnp.concatenate([-x2, x1], axis=-1)


def _mla_proj(x, sh, sc, w_down, q_norm, w_uq, kv_norm, w_ukv, tm):
    b, s, d = x.shape
    q_lora = q_norm.shape[0]
    kv_lora = kv_norm.shape[0]
    hd_q = MLA_NOPE + MLA_ROPE
    assert hd_q <= LANES and MLA_ROPE % 2 == 0 and q_lora % LANES == 0 and kv_lora == LANES

    r_a, r_b = _rope_perm_weights(w_down[:, q_lora + kv_lora:])
    pad = jnp.zeros((d, LANES - MLA_ROPE), F32)
    wd_ext = jnp.concatenate([w_down[:, :q_lora + kv_lora], r_a, pad, r_b, pad], axis=1).astype(BF16)

    wq = w_uq.reshape(q_lora, MLA_HEADS, hd_q)
    ra, rb = _rope_perm_weights(wq[..., MLA_NOPE:])
    wq_ext = jnp.concatenate([wq[..., :MLA_NOPE], ra, rb], axis=-1)
    wq_ext_t = wq_ext.reshape(q_lora, MLA_HEADS * LANES).T.astype(BF16)

    wkv = w_ukv.reshape(kv_lora, MLA_HEADS, MLA_NOPE + MLA_V)
    top = jnp.concatenate([wkv[..., :MLA_NOPE], jnp.zeros((kv_lora, MLA_HEADS, LANES - MLA_NOPE), F32)], axis=-1)
    copy = jnp.concatenate([jnp.zeros((LANES, MLA_NOPE), F32), jnp.eye(LANES, MLA_ROPE, dtype=F32),
                            jnp.zeros((LANES, LANES - MLA_NOPE - MLA_ROPE), F32)], axis=1)
    bot = jnp.broadcast_to(copy[:, None, :], (LANES, MLA_HEADS, LANES))
    wk_ext = jnp.concatenate([top, bot], axis=0).reshape(2 * LANES, MLA_HEADS * LANES).astype(BF16)
    wv_t = wkv[:, :, MLA_NOPE:].reshape(kv_lora, MLA_HEADS * MLA_V).T.astype(BF16)

    t = jnp.arange(s)
    row = (t // GRID_W).astype(F32)
    col = (t % GRID_W).astype(F32)
    nf = MLA_ROPE // 4
    inv = ROPE_THETA ** (-jnp.arange(nf, dtype=F32) / nf)
    ang = jnp.concatenate([row[:, None] * inv, col[:, None] * inv], axis=-1)
    cos2 = jnp.concatenate([jnp.cos(ang), jnp.cos(ang)], axis=1)
    sin2 = jnp.concatenate([jnp.sin(ang), jnp.sin(ang)], axis=1)
    lpad = jnp.zeros((s, LANES - MLA_ROPE), F32)
    cos_tok = jnp.concatenate([cos2, lpad], axis=1)
    sin_tok = jnp.concatenate([sin2, lpad], axis=1)
    cos_t = cos2.T
    sin_t = sin2.T

    qscale = float(hd_q ** -0.5 * LOG2E)
    nq = MLA_HEADS * LANES
    nv = MLA_HEADS * MLA_V
    nd = wd_ext.shape[1]
    full = lambda shape: pl.BlockSpec(shape, lambda bi, i: (0,) * len(shape))
    vec = pl.BlockSpec((1, 1, d), lambda bi, i: (bi, 0, 0))
    return pl.pallas_call(
        functools.partial(_mla_proj_kernel, q_lora=q_lora, kv_lora=kv_lora, qscale=qscale),
        grid=(b, s // tm),
        in_specs=[
            pl.BlockSpec((1, tm, d), lambda bi, i: (bi, i, 0)),
            vec, vec,
            full((d, nd)), full((1, q_lora)), full((1, kv_lora)),
            full((nq, q_lora)), full((2 * LANES, nq)), full((nv, kv_lora)),
            pl.BlockSpec((tm, LANES), lambda bi, i: (i, 0)),
            pl.BlockSpec((tm, LANES), lambda bi, i: (i, 0)),
            pl.BlockSpec((MLA_ROPE, tm), lambda bi, i: (0, i)),
            pl.BlockSpec((MLA_ROPE, tm), lambda bi, i: (0, i)),
        ],
        out_specs=[
            pl.BlockSpec((1, nq, tm), lambda bi, i: (bi, 0, i)),
            pl.BlockSpec((1, tm, nq), lambda bi, i: (bi, i, 0)),
            pl.BlockSpec((1, nv, tm), lambda bi, i: (bi, 0, i)),
        ],
        out_shape=[
            jax.ShapeDtypeStruct((b, nq, s), BF16),
            jax.ShapeDtypeStruct((b, s, nq), BF16),
            jax.ShapeDtypeStruct((b, nv, s), BF16),
        ],
        compiler_params=_params("arbitrary", "arbitrary"),
        name="mla_proj",
    )(x, sh, sc, wd_ext, q_norm.reshape(1, -1), kv_norm.reshape(1, -1), wq_ext_t, wk_ext, wv_t,
      cos_tok, sin_tok, cos_t, sin_t)


def _mla_attn_kernel(q_ref, k_ref, v_ref, o_ref, *s_refs, tq, tk, nkt, nq):
    ones = jnp.ones((SUM_ROWS, tk), BF16)

    def scores(qi, kt):
        qt = q_ref[0, :, pl.ds(pl.multiple_of(qi * tq, tq), tq)]
        s_refs[kt % MLA_SCORE_SLOTS][...] = jnp.dot(k_ref[0, kt * tk:(kt + 1) * tk, :], qt,
                                                    preferred_element_type=F32)

    def softmax_pv(kt, carry):
        m, acc = carry
        st = s_refs[kt % MLA_SCORE_SLOTS][...]
        m_new = jnp.maximum(m, jnp.max(st, axis=0, keepdims=True))
        p = jnp.exp2(st - m_new).astype(BF16)
        vb = jnp.concatenate([v_ref[0, :, kt * tk:(kt + 1) * tk], ones], axis=0)
        acc = jnp.exp2(m - m_new) * acc + jnp.dot(vb, p, preferred_element_type=F32)
        return m_new, acc

    for kt in range(MLA_SCORE_AHEAD):
        scores(0, kt)

    def body(qi, carry):
        nxt = jnp.minimum(qi + 1, nq - 1)
        carry = (jnp.full((1, tq), NEG_BIG, F32), jnp.zeros((MLA_V + SUM_ROWS, tq), F32))
        for kt in range(nkt):
            ahead = kt + MLA_SCORE_AHEAD
            if ahead < nkt:
                scores(qi, ahead)
            else:
                scores(nxt, ahead - nkt)
            carry = softmax_pv(kt, carry)
        m, acc = carry
        out = acc[:MLA_V] / acc[MLA_V:MLA_V + 1]
        o_ref[0, :, pl.ds(pl.multiple_of(qi * tq, tq), tq)] = out.astype(o_ref.dtype)
        return 0

    lax.fori_loop(0, nq, body, 0)


def _mla_attention(q_t, k, v_t, tq, tk):
    b, _, s = q_t.shape
    nv = v_t.shape[1]
    nkt = s // tk
    assert nkt % MLA_SCORE_SLOTS == 0 and nkt >= MLA_SCORE_AHEAD
    return pl.pallas_call(
        functools.partial(_mla_attn_kernel, tq=tq, tk=tk, nkt=nkt, nq=s // tq),
        grid=(b, MLA_HEADS),
        in_specs=[
            pl.BlockSpec((1, LANES, s), lambda bi, hd: (bi, hd, 0)),
            pl.BlockSpec((1, s, LANES), lambda bi, hd: (bi, 0, hd)),
            pl.BlockSpec((1, MLA_V, s), lambda bi, hd: (bi, hd, 0)),
        ],
        out_specs=pl.BlockSpec((1, MLA_V, s), lambda bi, hd: (bi, hd, 0)),
        out_shape=jax.ShapeDtypeStruct((b, nv, s), BF16),
        scratch_shapes=[pltpu.VMEM((tk, tq), F32)] * MLA_SCORE_SLOTS,
        compiler_params=_params("arbitrary", "arbitrary"),
        name="mla_attn",
    )(q_t, k, v_t)


def _router_kernel(a_ref, wo_ref, x_ref, ga_ref, sh_ref, sc_ref, wr_ref, xo_ref, h_ref, i_ref, w_ref, *, n_exp):
    tm = x_ref.shape[1]
    chunks = [slice(c, c + ROUTER_CHUNK) for c in range(0, tm, ROUTER_CHUNK)]
    w = wr_ref[...]
    w_hi, w_lo = _split_bf16(w)
    lane = lax.broadcasted_iota(jnp.int32, (ROUTER_CHUNK, LANES), 1)
    valid = lane < n_exp

    hs = []
    for rows in chunks:
        y = lax.dot_general(a_ref[0, :, rows], wo_ref[...], (((0,), (0,)), ((), ())),
                            preferred_element_type=F32)
        x = x_ref[0, rows] + ga_ref[0] * y
        xo_ref[0, rows] = x
        h = _modulate(x, sh_ref[0], sc_ref[0])
        h_ref[0, rows] = h
        hs.append(h)

    all_logits = []
    for h in hs:
        h_hi, h_lo = _split_bf16(h)
        all_logits.append(jnp.dot(h_hi, w_hi, preferred_element_type=F32)
                          + (jnp.dot(h_lo, w_hi, preferred_element_type=F32)
                             + jnp.dot(h_hi, w_lo, preferred_element_type=F32)))

    rests = [jnp.where(valid, logits, -jnp.inf) for logits in all_logits]
    vals = [[] for _ in chunks]
    idxs = [[] for _ in chunks]
    for _ in range(TOP_K):
        for c in range(len(chunks)):
            v = jnp.max(rests[c], axis=-1, keepdims=True)
            idx = jnp.min(jnp.where(rests[c] == v, lane, LANES), axis=-1, keepdims=True)
            vals[c].append(v)
            idxs[c].append(idx)
            rests[c] = jnp.where(lane == idx, -jnp.inf, rests[c])
    for c, rows in enumerate(chunks):
        ev = [jnp.exp(v - vals[c][0]) for v in vals[c]]
        tot = ev[0]
        for v in ev[1:]:
            tot = tot + v
        i_out = jnp.zeros((ROUTER_CHUNK, LANES), jnp.int32)
        w_out = jnp.zeros((ROUTER_CHUNK, LANES), F32)
        for k in range(TOP_K):
            i_out = jnp.where(lane == k, idxs[c][k], i_out)
            w_out = jnp.where(lane == k, ev[k] / tot, w_out)
        i_ref[0, rows] = i_out
        w_ref[0, rows] = w_out


def _proj_router(o_t, w_o, x, g_a, sh, sc, w_router, tm):
    b, s, d = x.shape
    k = w_o.shape[0]
    n_exp = w_router.shape[1]
    wr = jnp.zeros((d, LANES), F32).at[:, :n_exp].set(w_router)
    vec = pl.BlockSpec((1, 1, d), lambda bi, i: (bi, 0, 0))
    tile = pl.BlockSpec((1, tm, d), lambda bi, i: (bi, i, 0))
    lanes_spec = pl.BlockSpec((1, tm, LANES), lambda bi, i: (bi, i, 0))
    return pl.pallas_call(
        functools.partial(_router_kernel, n_exp=n_exp),
        grid=(b, s // tm),
        in_specs=[
            pl.BlockSpec((1, k, tm), lambda bi, i: (bi, 0, i)),
            pl.BlockSpec((k, d), lambda bi, i: (0, 0)),
            tile,
            vec, vec, vec,
            pl.BlockSpec((d, LANES), lambda bi, i: (0, 0)),
        ],
        out_specs=[tile, tile, lanes_spec, lanes_spec],
        out_shape=[jax.ShapeDtypeStruct((b, s, d), F32),
                   jax.ShapeDtypeStruct((b, s, d), F32),
                   jax.ShapeDtypeStruct((b, s, LANES), jnp.int32),
                   jax.ShapeDtypeStruct((b, s, LANES), F32)],
        compiler_params=_params("arbitrary", "arbitrary"),
        name="proj_router",
    )(o_t, w_o, x, g_a, sh, sc, wr)


def _route_plan(idx, n_exp, tile):
    t, k = idx.shape
    e_flat = idx.reshape(t * k)
    onehot = (e_flat[:, None] == jnp.arange(n_exp, dtype=jnp.int32)[None, :]).astype(jnp.int32)
    csum = jnp.cumsum(onehot, axis=0)
    counts = csum[-1]
    rank = jnp.sum(onehot * csum, axis=1) - 1
    padded = ((counts + tile - 1) // tile) * tile
    gend = jnp.cumsum(padded)
    gstart = gend - padded
    dest = jnp.sum(onehot * gstart[None, :], axis=1) + rank
    p_rows = t * k + n_exp * tile
    tile_start = jnp.arange(p_rows // tile, dtype=jnp.int32) * tile
    tile_expert = jnp.minimum(jnp.sum((tile_start[:, None] >= gend[None, :]).astype(jnp.int32), axis=1),
                              n_exp - 1)
    tile_live = (tile_start < gend[-1]).astype(jnp.int32)
    return dest.astype(jnp.int32), tile_expert, tile_live, p_rows


def _row_scatter_kernel(dest_ref, h_ref, xs_in_ref, xs_ref, sem, *, tm):
    del xs_in_ref

    def row_copy(r, k):
        return pltpu.make_async_copy(h_ref.at[pl.ds(r, 1)], xs_ref.at[pl.ds(dest_ref[TOP_K * r + k], 1)], sem)

    def issue(r, carry):
        for k in range(TOP_K):
            row_copy(r, k).start(priority=k % 2)
        return carry

    lax.fori_loop(0, tm, issue, 0, unroll=4)
    for k in range(TOP_K):
        pltpu.make_async_copy(h_ref, xs_ref.at[pl.ds(0, tm)], sem).wait()


def _row_scatter(h, dest, p_rows, tm):
    n, d = h.shape
    return pl.pallas_call(
        functools.partial(_row_scatter_kernel, tm=tm),
        grid=(n // tm,),
        in_specs=[pl.BlockSpec((TOP_K * tm,), lambda i: (i,), memory_space=pltpu.SMEM),
                  pl.BlockSpec((tm, d), lambda i: (i, 0)),
                  pl.BlockSpec(memory_space=pl.ANY)],
        out_specs=pl.BlockSpec(memory_space=pl.ANY),
        out_shape=jax.ShapeDtypeStruct((p_rows, d), h.dtype),
        input_output_aliases={2: 0},
        scratch_shapes=[pltpu.SemaphoreType.DMA(())],
        compiler_params=_params("arbitrary"),
        name="moe_dispatch",
    )(dest, h, jnp.zeros((p_rows, d), h.dtype))


def _experts_kernel(te_ref, live_ref, x_ref, wgu_ref, wd_ref, o_ref, *, f):
    i = pl.program_id(0)

    @pl.when(live_ref[i] != 0)
    def _():
        o_ref[...] = _swiglu_tile(x_ref[...].astype(BF16), wgu_ref.at[0], wd_ref.at[0], f)

    @pl.when(live_ref[i] == 0)
    def _():
        o_ref[...] = jnp.zeros_like(o_ref)


def _experts(xs, tile_expert, tile_live, wgu, wd, tile):
    p_rows, d = xs.shape
    n_exp, f, _ = wd.shape
    grid_spec = pltpu.PrefetchScalarGridSpec(
        num_scalar_prefetch=2,
        grid=(p_rows // tile,),
        in_specs=[
            pl.BlockSpec((tile, d), lambda i, te, lv: (i, 0)),
            pl.BlockSpec((1, d, 2 * f), lambda i, te, lv: (te[i], 0, 0)),
            pl.BlockSpec((1, f, d), lambda i, te, lv: (te[i], 0, 0)),
        ],
        out_specs=pl.BlockSpec((tile, d), lambda i, te, lv: (i, 0)),
    )
    return pl.pallas_call(
        functools.partial(_experts_kernel, f=f),
        grid_spec=grid_spec,
        out_shape=jax.ShapeDtypeStruct((p_rows, d), F32),
        compiler_params=_params("arbitrary"),
        name="moe_experts",
    )(tile_expert, tile_live, xs, wgu, wd)


def _combine_kernel(dest_ref, x_ref, g_ref, fn_ref, w_ref, ys_ref, o_ref, buf_ref, sem, *, tm):
    def row_copy(r, k):
        return pltpu.make_async_copy(ys_ref.at[pl.ds(dest_ref[TOP_K * r + k], 1)],
                                     buf_ref.at[k, pl.ds(r, 1)], sem)

    def issue(r, carry):
        for k in range(TOP_K):
            row_copy(r, k).start(priority=k % 2)
        return carry

    lax.fori_loop(0, tm, issue, 0, unroll=4)
    for k in range(TOP_K):
        pltpu.make_async_copy(ys_ref.at[pl.ds(0, tm)], buf_ref.at[k], sem).wait()
    w = w_ref[...]
    y = w[:, 0:1] * buf_ref[0]
    for k in range(1, TOP_K):
        y = y + w[:, k:k + 1] * buf_ref[k]
    o_ref[...] = _rms(x_ref[...] + g_ref[0] * y) * fn_ref[...]


def _combine(x, g, final_norm, wts, ys, dest, tm):
    b, s, d = x.shape
    tiles_per_batch = s // tm
    return pl.pallas_call(
        functools.partial(_combine_kernel, tm=tm),
        grid=(b * tiles_per_batch,),
        in_specs=[
            pl.BlockSpec((TOP_K * tm,), lambda i: (i,), memory_space=pltpu.SMEM),
            pl.BlockSpec((tm, d), lambda i: (i, 0)),
            pl.BlockSpec((1, 1, d), lambda i: (i // tiles_per_batch, 0, 0)),
            pl.BlockSpec((1, d), lambda i: (0, 0)),
            pl.BlockSpec((tm, LANES), lambda i: (i, 0)),
            pl.BlockSpec(memory_space=pl.ANY),
        ],
        out_specs=pl.BlockSpec((tm, d), lambda i: (i, 0)),
        out_shape=jax.ShapeDtypeStruct((b * s, d), F32),
        scratch_shapes=[pltpu.VMEM((TOP_K, tm, d), F32), pltpu.SemaphoreType.DMA(())],
        compiler_params=_params("arbitrary"),
        name="moe_combine",
    )(dest, x.reshape(b * s, d), g, final_norm.reshape(1, d), wts.reshape(b * s, LANES), ys).reshape(b, s, d)


def kernel(x, c, w_ada, b_ada, na_w_qkv, na_rpb, na_w_o, ffn_w_gu, ffn_w_down, mla_w_down, mla_q_norm,
           mla_w_uq, mla_kv_norm, mla_w_ukv, mla_w_o, moe_w_router, moe_w_gu, moe_w_down, final_norm):
    b, s, d = x.shape
    assert w_ada.shape[0] == 2, "one neighbourhood-attention layer followed by one latent-attention layer"
    t = _tiles(s)
    mods = _adaln(c, w_ada, b_ada)

    def split(layer):
        return [v.reshape(b, 1, d) for v in jnp.split(mods[layer], 6, axis=-1)]

    sh_a, sc_a, g_a, sh_f, sc_f, g_f = split(0)
    qkv = _ln_matmul(x, sh_a, sc_a, na_w_qkv[0].astype(BF16), t.token)
    att = _neighborhood_attention(qkv, na_rpb[0])
    x = _proj_ffn(att, na_w_o[0].astype(BF16), x, g_a, sh_f, sc_f, g_f,
                  ffn_w_gu[0].astype(BF16), ffn_w_down[0].astype(BF16), t.token)

    sh_a, sc_a, g_a, sh_f, sc_f, g_f = split(1)
    q_t, k, v_t = _mla_proj(x, sh_a, sc_a, mla_w_down[0], mla_q_norm[0], mla_w_uq[0], mla_kv_norm[0],
                            mla_w_ukv[0], t.token)
    o_t = _mla_attention(q_t, k, v_t, t.mla_q, t.mla_k)
    x, h, idx, wts = _proj_router(o_t, mla_w_o[0].astype(BF16), x, g_a, sh_f, sc_f, moe_w_router[0], t.router)
    n_exp = moe_w_router.shape[-1]
    dest, tile_expert, tile_live, p_rows = _route_plan(idx[..., :TOP_K].reshape(b * s, TOP_K), n_exp, t.token)
    xs = _row_scatter(h.reshape(b * s, d), dest, p_rows, t.token)
    ys = _experts(xs, tile_expert, tile_live, moe_w_gu[0].astype(BF16), moe_w_down[0].astype(BF16), t.token)
    return _combine(x, g_f, final_norm, wts, ys, dest, t.token)
```

```python
import functools
from typing import NamedTuple

import jax
import jax.numpy as jnp
from jax import lax
from jax.experimental import pallas as pl
from jax.experimental.pallas import tpu as pltpu

F32 = jnp.float32
BF16 = jnp.bfloat16

GRID_W = 64
NA_HEADS = 16
NA_WIN_H = 8
NA_WIN_W = 16
MLA_HEADS = 16
MLA_NOPE = 64
MLA_ROPE = 32
MLA_V = 64
ROPE_THETA = 10000.0
TOP_K = 2
NORM_EPS = 1e-6

LANES = 128
SUBLANES = 8
MXU_DIM = 256
SUM_ROWS = 2 * SUBLANES
VMEM_LIMIT = 56 * 1024 * 1024

NA_ROWS_PER_STEP = 8
ROUTER_CHUNK = 256
MLA_SCORE_AHEAD = 2
MLA_SCORE_SLOTS = 4
ADALN_COLS = 1536
NEG_BIG = -1e30
LOG2E = 1.4426950408889634


def _params(*sem):
    return pltpu.CompilerParams(dimension_semantics=sem, vmem_limit_bytes=VMEM_LIMIT)


class _Tiles(NamedTuple):
    token: int
    wide: int
    mla_q: int
    mla_k: int


def _tiles(s):
    return _Tiles(token=min(512, s), wide=min(1024, s), mla_q=min(512, s), mla_k=256)


def _rms(x):
    return x * lax.rsqrt(jnp.mean(x * x, axis=-1, keepdims=True) + NORM_EPS)


def _modulate(x, sh, sc):
    return _rms(x) * (1.0 + sc) + sh


def _split_bf16(a):
    hi = a.astype(BF16)
    return hi, (a - hi.astype(F32)).astype(BF16)


def _dot_3pass(a, b):
    a_hi, a_lo = _split_bf16(a)
    b_hi, b_lo = _split_bf16(b)
    return (jnp.dot(a_hi, b_hi, preferred_element_type=F32)
            + (jnp.dot(a_lo, b_hi, preferred_element_type=F32)
               + jnp.dot(a_hi, b_lo, preferred_element_type=F32)))


def _adaln_kernel(c_ref, w_ref, b_ref, o_ref):
    c = c_ref[...]
    ca = c * jax.nn.sigmoid(c)
    o_ref[0] = _dot_3pass(ca, w_ref[0]) + b_ref[0]


def _adaln(c, w_ada, b_ada):
    depth, d, n = w_ada.shape
    b = c.shape[0]
    rows = SUBLANES
    assert b <= rows
    cp = jnp.zeros((rows, d), F32).at[:b].set(c)
    tn = ADALN_COLS
    out = pl.pallas_call(
        _adaln_kernel,
        grid=(depth, n // tn),
        in_specs=[
            pl.BlockSpec((rows, d), lambda l, j: (0, 0)),
            pl.BlockSpec((1, d, tn), lambda l, j: (l, 0, j)),
            pl.BlockSpec((1, 1, tn), lambda l, j: (l, 0, j)),
        ],
        out_specs=pl.BlockSpec((1, rows, tn), lambda l, j: (l, 0, j)),
        out_shape=jax.ShapeDtypeStruct((depth, rows, n), F32),
        compiler_params=_params("arbitrary", "arbitrary"),
        name="adaln",
    )(cp, w_ada, b_ada.reshape(depth, 1, n))
    return out[:, :b]


def _ln_matmul_kernel(x_ref, sh_ref, sc_ref, w_ref, o_ref):
    h = _modulate(x_ref[0], sh_ref[0], sc_ref[0]).astype(BF16)
    o_ref[0] = jnp.dot(h, w_ref[...], preferred_element_type=F32).astype(o_ref.dtype)


def _ln_matmul(x, sh, sc, w, tm):
    b, s, d = x.shape
    n = w.shape[1]
    vec = pl.BlockSpec((1, 1, d), lambda bi, i: (bi, 0, 0))
    return pl.pallas_call(
        _ln_matmul_kernel,
        grid=(b, s // tm),
        in_specs=[
            pl.BlockSpec((1, tm, d), lambda bi, i: (bi, i, 0)),
            vec, vec,
            pl.BlockSpec((d, n), lambda bi, i: (0, 0)),
        ],
        out_specs=pl.BlockSpec((1, tm, n), lambda bi, i: (bi, i, 0)),
        out_shape=jax.ShapeDtypeStruct((b, s, n), BF16),
        compiler_params=_params("arbitrary", "arbitrary"),
        name="ln_matmul",
    )(x, sh, sc, w)


def _na_kernel(q_ref, k_ref, v_ref, t_ref, o_ref, *, rows):
    win = NA_WIN_H * GRID_W
    lane = lax.broadcasted_iota(jnp.int32, (GRID_W, LANES), 1)
    first = lane < (LANES // 2)
    scale = (LANES // 2) ** -0.5 * LOG2E

    def bias(pat):
        d0 = NA_WIN_H - 1 - pat
        return jnp.concatenate(
            [jnp.concatenate([t_ref[0, d0 + 2 * j], t_ref[1, d0 + 2 * j]], axis=0)
             for j in range(NA_WIN_H // 2)], axis=1)

    def body(i, carry):
        rr = [i * NA_ROWS_PER_STEP + j for j in range(NA_ROWS_PER_STEP)]
        rss = [jnp.clip(r - NA_WIN_H // 2, 0, rows - NA_WIN_H) for r in rr]
        ss = []
        for r, rs in zip(rr, rss):
            q = q_ref[0, pl.ds(pl.multiple_of(r * GRID_W, GRID_W), GRID_W), :]
            zero = jnp.zeros_like(q)
            q2 = jnp.concatenate([jnp.where(first, q, zero), jnp.where(first, zero, q)], axis=0)
            kw = k_ref[0, pl.ds(pl.multiple_of(rs * GRID_W, GRID_W), win), :]
            s = lax.dot_general(q2, kw, (((1,), (1,)), ((), ())), preferred_element_type=F32)
            ss.append(s * scale + bias(r - rs))
        ps, ls = [], []
        for s in ss:
            p = jnp.exp2(s - jnp.max(s, axis=-1, keepdims=True))
            ls.append(jnp.sum(p, axis=-1, keepdims=True))
            ps.append(p.astype(BF16))
        for r, rs, p, l in zip(rr, rss, ps, ls):
            vw = v_ref[0, pl.ds(pl.multiple_of(rs * GRID_W, GRID_W), win), :]
            o = jnp.dot(p, vw, preferred_element_type=F32) / l
            o = jnp.where(first, o[:GRID_W], o[GRID_W:])
            o_ref[0, pl.ds(pl.multiple_of(r * GRID_W, GRID_W), GRID_W), :] = o.astype(o_ref.dtype)
        return carry

    lax.fori_loop(0, rows // NA_ROWS_PER_STEP, body, 0)


def _na_bias_table(rpb):
    h, ndr, ndc = rpb.shape
    qc = jnp.arange(GRID_W)
    kc = jnp.arange(GRID_W)
    cs = jnp.clip(qc - NA_WIN_W // 2, 0, GRID_W - NA_WIN_W)
    inwin = (kc[None, :] >= cs[:, None]) & (kc[None, :] < cs[:, None] + NA_WIN_W)
    period = 2 * GRID_W
    u = jnp.concatenate([rpb[:, :, NA_WIN_W - 1:], jnp.zeros((h, ndr, period - ndc), F32),
                         rpb[:, :, :NA_WIN_W - 1]], axis=-1)
    flat = jnp.tile(u, (1, 1, GRID_W))[:, :, :GRID_W * (period - 1)]
    col = flat.reshape(h, ndr, GRID_W, period - 1)[:, :, :, :GRID_W]
    col = jnp.where(inwin[None, None], col * LOG2E, NEG_BIG)
    return jnp.concatenate([col[:, :-1], col[:, 1:]], axis=-1)


def _neighborhood_attention(qkv, rpb):
    b, s, d3 = qkv.shape
    d = d3 // 3
    rows = s // GRID_W
    assert rows >= NA_WIN_H and d // NA_HEADS == LANES // 2 and rows % NA_ROWS_PER_STEP == 0
    nslab = d // LANES
    table = _na_bias_table(rpb.astype(F32))
    win = NA_WIN_H * GRID_W
    return pl.pallas_call(
        functools.partial(_na_kernel, rows=rows),
        grid=(b, nslab),
        in_specs=[
            pl.BlockSpec((1, s, LANES), lambda bi, hp: (bi, 0, hp)),
            pl.BlockSpec((1, s, LANES), lambda bi, hp: (bi, 0, nslab + hp)),
            pl.BlockSpec((1, s, LANES), lambda bi, hp: (bi, 0, 2 * nslab + hp)),
            pl.BlockSpec((2,) + table.shape[1:], lambda bi, hp: (hp, 0, 0, 0)),
        ],
        out_specs=pl.BlockSpec((1, s, LANES), lambda bi, hp: (bi, 0, hp)),
        out_shape=jax.ShapeDtypeStruct((b, s, d), BF16),
        compiler_params=_params("arbitrary", "arbitrary"),
        name="na_attn",
    )(qkv, qkv, qkv, table)


def _swiglu_tile(h, wgu, wd, f):
    acc = None
    for c in range(f // MXU_DIM):
        lo = c * MXU_DIM
        g = jnp.dot(h, wgu[:, lo:lo + MXU_DIM], preferred_element_type=F32)
        u = jnp.dot(h, wgu[:, f + lo:f + lo + MXU_DIM], preferred_element_type=F32)
        a = (g * jax.nn.sigmoid(g) * u).astype(BF16)
        y = jnp.dot(a, wd[lo:lo + MXU_DIM, :], preferred_element_type=F32)
        acc = y if acc is None else acc + y
    return acc


def _ffn_kernel(a_ref, wo_ref, x_ref, ga_ref, sh_ref, sc_ref, g_ref, wgu_ref, wd_ref, o_ref, *, f):
    x = x_ref[0] + ga_ref[0] * jnp.dot(a_ref[0], wo_ref[...], preferred_element_type=F32)
    h = _modulate(x, sh_ref[0], sc_ref[0]).astype(BF16)
    o_ref[0] = x + g_ref[0] * _swiglu_tile(h, wgu_ref, wd_ref, f)


def _proj_ffn(att, w_o, x, g_a, sh, sc, g, wgu, wd, tm):
    b, s, d = x.shape
    f = wd.shape[0]
    vec = pl.BlockSpec((1, 1, d), lambda bi, i: (bi, 0, 0))
    tile = pl.BlockSpec((1, tm, d), lambda bi, i: (bi, i, 0))
    return pl.pallas_call(
        functools.partial(_ffn_kernel, f=f),
        grid=(b, s // tm),
        in_specs=[
            tile,
            pl.BlockSpec((d, d), lambda bi, i: (0, 0)),
            tile,
            vec, vec, vec, vec,
            pl.BlockSpec((d, 2 * f), lambda bi, i: (0, 0)),
            pl.BlockSpec((f, d), lambda bi, i: (0, 0)),
        ],
        out_specs=tile,
        out_shape=jax.ShapeDtypeStruct((b, s, d), F32),
        compiler_params=_params("arbitrary", "arbitrary"),
        name="proj_ffn",
    )(att, w_o, x, g_a, sh, sc, g, wgu, wd)


def _mla_proj_kernel(x_ref, sh_ref, sc_ref, wd_ref, qn_ref, kvn_ref, wq_ref, wk_ref, wv_ref,
                     cs_ref, sn_ref, cst_ref, snt_ref, q_ref, k_ref, v_ref, *, q_lora, kv_lora, qscale):
    h = _modulate(x_ref[0], sh_ref[0], sc_ref[0]).astype(BF16)
    down = jnp.dot(h, wd_ref[...], preferred_element_type=F32)
    cq = (_rms(down[:, :q_lora]) * qn_ref[...]).astype(BF16)
    ckv = (_rms(down[:, q_lora:q_lora + kv_lora]) * kvn_ref[...]).astype(BF16)
    r0 = q_lora + kv_lora
    kr = down[:, r0:r0 + LANES] * cs_ref[...] + down[:, r0 + LANES:r0 + 2 * LANES] * sn_ref[...]
    z = jnp.concatenate([ckv, kr.astype(BF16)], axis=1)
    k_ref[0] = jnp.dot(z, wk_ref[...], preferred_element_type=F32).astype(BF16)
    nt = (((1,), (1,)), ((), ()))
    v_ref[0] = lax.dot_general(wv_ref[...], ckv, nt, preferred_element_type=F32).astype(BF16)
    qt = lax.dot_general(wq_ref[...], cq, nt, preferred_element_type=F32)
    cst = cst_ref[...]
    snt = snt_ref[...]
    for hd in range(MLA_HEADS):
        base = hd * LANES
        blk = qt[base:base + LANES]
        rot = blk[MLA_NOPE:MLA_NOPE + MLA_ROPE] * cst + blk[MLA_NOPE + MLA_ROPE:] * snt
        q_ref[0, base:base + MLA_NOPE, :] = (blk[:MLA_NOPE] * qscale).astype(BF16)
        q_ref[0, base + MLA_NOPE:base + MLA_NOPE + MLA_ROPE, :] = (rot * qscale).astype(BF16)
        q_ref[0, base + MLA_NOPE + MLA_ROPE:base + LANES, :] = jnp.zeros(
            (LANES - MLA_NOPE - MLA_ROPE, rot.shape[1]), BF16)


def _rope_perm_weights(w_rope):
    x1 = w_rope[..., 0::2]
    x2 = w_rope[..., 1::2]
    return jnp.concatenate([x1, x2], axis=-1), jnp.concatenate([-x2, x1], axis=-1)


def _mla_proj(x, sh, sc, w_down, q_norm, w_uq, kv_norm, w_ukv, tm):
    b, s, d = x.shape
    q_lora = q_norm.shape[0]
    kv_lora = kv_norm.shape[0]
    hd_q = MLA_NOPE + MLA_ROPE
    assert hd_q <= LANES and MLA_ROPE % 2 == 0 and q_lora % LANES == 0 and kv_lora == LANES

    r_a, r_b = _rope_perm_weights(w_down[:, q_lora + kv_lora:])
    pad = jnp.zeros((d, LANES - MLA_ROPE), F32)
    wd_ext = jnp.concatenate([w_down[:, :q_lora + kv_lora], r_a, pad, r_b, pad], axis=1).astype(BF16)

    wq = w_uq.reshape(q_lora, MLA_HEADS, hd_q)
    ra, rb = _rope_perm_weights(wq[..., MLA_NOPE:])
    wq_ext = jnp.concatenate([wq[..., :MLA_NOPE], ra, rb], axis=-1)
    wq_ext_t = wq_ext.reshape(q_lora, MLA_HEADS * LANES).T.astype(BF16)

    wkv = w_ukv.reshape(kv_lora, MLA_HEADS, MLA_NOPE + MLA_V)
    top = jnp.concatenate([wkv[..., :MLA_NOPE], jnp.zeros((kv_lora, MLA_HEADS, LANES - MLA_NOPE), F32)], axis=-1)
    copy = jnp.concatenate([jnp.zeros((LANES, MLA_NOPE), F32), jnp.eye(LANES, MLA_ROPE, dtype=F32),
                            jnp.zeros((LANES, LANES - MLA_NOPE - MLA_ROPE), F32)], axis=1)
    bot = jnp.broadcast_to(copy[:, None, :], (LANES, MLA_HEADS, LANES))
    wk_ext = jnp.concatenate([top, bot], axis=0).reshape(2 * LANES, MLA_HEADS * LANES).astype(BF16)
    wv_t = wkv[:, :, MLA_NOPE:].reshape(kv_lora, MLA_HEADS * MLA_V).T.astype(BF16)

    t = jnp.arange(s)
    row = (t // GRID_W).astype(F32)
    col = (t % GRID_W).astype(F32)
    nf = MLA_ROPE // 4
    inv = ROPE_THETA ** (-jnp.arange(nf, dtype=F32) / nf)
    ang = jnp.concatenate([row[:, None] * inv, col[:, None] * inv], axis=-1)
    cos2 = jnp.concatenate([jnp.cos(ang), jnp.cos(ang)], axis=1)
    sin2 = jnp.concatenate([jnp.sin(ang), jnp.sin(ang)], axis=1)
    lpad = jnp.zeros((s, LANES - MLA_ROPE), F32)
    cos_tok = jnp.concatenate([cos2, lpad], axis=1)
    sin_tok = jnp.concatenate([sin2, lpad], axis=1)
    cos_t = cos2.T
    sin_t = sin2.T

    qscale = float(hd_q ** -0.5 * LOG2E)
    nq = MLA_HEADS * LANES
    nv = MLA_HEADS * MLA_V
    nd = wd_ext.shape[1]
    full = lambda shape: pl.BlockSpec(shape, lambda bi, i: (0,) * len(shape))
    vec = pl.BlockSpec((1, 1, d), lambda bi, i: (bi, 0, 0))
    return pl.pallas_call(
        functools.partial(_mla_proj_kernel, q_lora=q_lora, kv_lora=kv_lora, qscale=qscale),
        grid=(b, s // tm),
        in_specs=[
            pl.BlockSpec((1, tm, d), lambda bi, i: (bi, i, 0)),
            vec, vec,
            full((d, nd)), full((1, q_lora)), full((1, kv_lora)),
            full((nq, q_lora)), full((2 * LANES, nq)), full((nv, kv_lora)),
            pl.BlockSpec((tm, LANES), lambda bi, i: (i, 0)),
            pl.BlockSpec((tm, LANES), lambda bi, i: (i, 0)),
            pl.BlockSpec((MLA_ROPE, tm), lambda bi, i: (0, i)),
            pl.BlockSpec((MLA_ROPE, tm), lambda bi, i: (0, i)),
        ],
        out_specs=[
            pl.BlockSpec((1, nq, tm), lambda bi, i: (bi, 0, i)),
            pl.BlockSpec((1, tm, nq), lambda bi, i: (bi, i, 0)),
            pl.BlockSpec((1, nv, tm), lambda bi, i: (bi, 0, i)),
        ],
        out_shape=[
            jax.ShapeDtypeStruct((b, nq, s), BF16),
            jax.ShapeDtypeStruct((b, s, nq), BF16),
            jax.ShapeDtypeStruct((b, nv, s), BF16),
        ],
        compiler_params=_params("arbitrary", "arbitrary"),
        name="mla_proj",
    )(x, sh, sc, wd_ext, q_norm.reshape(1, -1), kv_norm.reshape(1, -1), wq_ext_t, wk_ext, wv_t,
      cos_tok, sin_tok, cos_t, sin_t)


def _mla_attn_kernel(q_ref, k_ref, v_ref, o_ref, *s_refs, tq, tk, nkt, nq):
    ones = jnp.ones((SUM_ROWS, tk), BF16)

    def scores(qi, kt):
        qt = q_ref[0, :, pl.ds(pl.multiple_of(qi * tq, tq), tq)]
        s_refs[kt % MLA_SCORE_SLOTS][...] = jnp.dot(k_ref[0, kt * tk:(kt + 1) * tk, :], qt,
                                                    preferred_element_type=F32)

    def softmax_pv(kt, carry):
        m, acc = carry
        st = s_refs[kt % MLA_SCORE_SLOTS][...]
        m_new = jnp.maximum(m, jnp.max(st, axis=0, keepdims=True))
        p = jnp.exp2(st - m_new).astype(BF16)
        vb = jnp.concatenate([v_ref[0, :, kt * tk:(kt + 1) * tk], ones], axis=0)
        acc = jnp.exp2(m - m_new) * acc + jnp.dot(vb, p, preferred_element_type=F32)
        return m_new, acc

    for kt in range(MLA_SCORE_AHEAD):
        scores(0, kt)

    def body(qi, carry):
        nxt = jnp.minimum(qi + 1, nq - 1)
        carry = (jnp.full((1, tq), NEG_BIG, F32), jnp.zeros((MLA_V + SUM_ROWS, tq), F32))
        for kt in range(nkt):
            ahead = kt + MLA_SCORE_AHEAD
            if ahead < nkt:
                scores(qi, ahead)
            else:
                scores(nxt, ahead - nkt)
            carry = softmax_pv(kt, carry)
        m, acc = carry
        out = acc[:MLA_V] / acc[MLA_V:MLA_V + 1]
        o_ref[0, :, pl.ds(pl.multiple_of(qi * tq, tq), tq)] = out.astype(o_ref.dtype)
        return 0

    lax.fori_loop(0, nq, body, 0)


def _mla_attention(q_t, k, v_t, tq, tk):
    b, _, s = q_t.shape
    nv = v_t.shape[1]
    nkt = s // tk
    assert nkt % MLA_SCORE_SLOTS == 0 and nkt >= MLA_SCORE_AHEAD
    return pl.pallas_call(
        functools.partial(_mla_attn_kernel, tq=tq, tk=tk, nkt=nkt, nq=s // tq),
        grid=(b, MLA_HEADS),
        in_specs=[
            pl.BlockSpec((1, LANES, s), lambda bi, hd: (bi, hd, 0)),
            pl.BlockSpec((1, s, LANES), lambda bi, hd: (bi, 0, hd)),
            pl.BlockSpec((1, MLA_V, s), lambda bi, hd: (bi, hd, 0)),
        ],
        out_specs=pl.BlockSpec((1, MLA_V, s), lambda bi, hd: (bi, hd, 0)),
        out_shape=jax.ShapeDtypeStruct((b, nv, s), BF16),
        scratch_shapes=[pltpu.VMEM((tk, tq), F32)] * MLA_SCORE_SLOTS,
        compiler_params=_params("arbitrary", "arbitrary"),
        name="mla_attn",
    )(q_t, k, v_t)


def _router_kernel(a_ref, wo_ref, x_ref, ga_ref, sh_ref, sc_ref, wr_ref, xo_ref, h_ref, i_ref, w_ref, *, n_exp):
    tm = x_ref.shape[1]
    chunks = [slice(c, c + ROUTER_CHUNK) for c in range(0, tm, ROUTER_CHUNK)]
    w = wr_ref[...]
    w_hi, w_lo = _split_bf16(w)
    lane = lax.broadcasted_iota(jnp.int32, (ROUTER_CHUNK, LANES), 1)
    valid = lane < n_exp

    hs = []
    for rows in chunks:
        y = lax.dot_general(a_ref[0, :, rows], wo_ref[...], (((0,), (0,)), ((), ())),
                            preferred_element_type=F32)
        x = x_ref[0, rows] + ga_ref[0] * y
        xo_ref[0, rows] = x
        h = _modulate(x, sh_ref[0], sc_ref[0])
        h_ref[0, rows] = h
        hs.append(h)

    all_logits = []
    for h in hs:
        h_hi, h_lo = _split_bf16(h)
        all_logits.append(jnp.dot(h_hi, w_hi, preferred_element_type=F32)
                          + (jnp.dot(h_lo, w_hi, preferred_element_type=F32)
                             + jnp.dot(h_hi, w_lo, preferred_element_type=F32)))

    rests = [jnp.where(valid, logits, -jnp.inf) for logits in all_logits]
    vals = [[] for _ in chunks]
    idxs = [[] for _ in chunks]
    for _ in range(TOP_K):
        for c in range(len(chunks)):
            v = jnp.max(rests[c], axis=-1, keepdims=True)
            idx = jnp.min(jnp.where(rests[c] == v, lane, LANES), axis=-1, keepdims=True)
            vals[c].append(v)
            idxs[c].append(idx)
            rests[c] = jnp.where(lane == idx, -jnp.inf, rests[c])
    for c, rows in enumerate(chunks):
        ev = [jnp.exp(v - vals[c][0]) for v in vals[c]]
        tot = ev[0]
        for v in ev[1:]:
            tot = tot + v
        i_out = jnp.zeros((ROUTER_CHUNK, LANES), jnp.int32)
        w_out = jnp.zeros((ROUTER_CHUNK, LANES), F32)
        for k in range(TOP_K):
            i_out = jnp.where(lane == k, idxs[c][k], i_out)
            w_out = jnp.where(lane == k, ev[k] / tot, w_out)
        i_ref[0, rows] = i_out
        w_ref[0, rows] = w_out


def _proj_router(o_t, w_o, x, g_a, sh, sc, w_router, tm):
    b, s, d = x.shape
    k = w_o.shape[0]
    n_exp = w_router.shape[1]
    wr = jnp.zeros((d, LANES), F32).at[:, :n_exp].set(w_router)
    vec = pl.BlockSpec((1, 1, d), lambda bi, i: (bi, 0, 0))
    tile = pl.BlockSpec((1, tm, d), lambda bi, i: (bi, i, 0))
    lanes_spec = pl.BlockSpec((1, tm, LANES), lambda bi, i: (bi, i, 0))
    return pl.pallas_call(
        functools.partial(_router_kernel, n_exp=n_exp),
        grid=(b, s // tm),
        in_specs=[
            pl.BlockSpec((1, k, tm), lambda bi, i: (bi, 0, i)),
            pl.BlockSpec((k, d), lambda bi, i: (0, 0)),
            tile,
            vec, vec, vec,
            pl.BlockSpec((d, LANES), lambda bi, i: (0, 0)),
        ],
        out_specs=[tile, tile, lanes_spec, lanes_spec],
        out_shape=[jax.ShapeDtypeStruct((b, s, d), F32),
                   jax.ShapeDtypeStruct((b, s, d), F32),
                   jax.ShapeDtypeStruct((b, s, LANES), jnp.int32),
                   jax.ShapeDtypeStruct((b, s, LANES), F32)],
        compiler_params=_params("arbitrary", "arbitrary"),
        name="proj_router",
    )(o_t, w_o, x, g_a, sh, sc, wr)


def _route_plan(idx, n_exp, tile):
    t, k = idx.shape
    e_flat = idx.reshape(t * k)
    onehot = (e_flat[:, None] == jnp.arange(n_exp, dtype=jnp.int32)[None, :]).astype(jnp.int32)
    csum = jnp.cumsum(onehot, axis=0)
    counts = csum[-1]
    rank = jnp.sum(onehot * csum, axis=1) - 1
    padded = ((counts + tile - 1) // tile) * tile
    gend = jnp.cumsum(padded)
    gstart = gend - padded
    dest = jnp.sum(onehot * gstart[None, :], axis=1) + rank
    p_rows = t * k + n_exp * tile
    tile_start = jnp.arange(p_rows // tile, dtype=jnp.int32) * tile
    tile_expert = jnp.minimum(jnp.sum((tile_start[:, None] >= gend[None, :]).astype(jnp.int32), axis=1),
                              n_exp - 1)
    tile_live = (tile_start < gend[-1]).astype(jnp.int32)
    return dest.astype(jnp.int32), tile_expert, tile_live, p_rows


def _row_scatter_kernel(dest_ref, h_ref, xs_in_ref, xs_ref, sem, *, tm):
    del xs_in_ref

    def row_copy(r, k):
        return pltpu.make_async_copy(h_ref.at[pl.ds(r, 1)], xs_ref.at[pl.ds(dest_ref[TOP_K * r + k], 1)], sem)

    def issue(r, carry):
        for k in range(TOP_K):
            row_copy(r, k).start(priority=k % 2)
        return carry

    lax.fori_loop(0, tm, issue, 0, unroll=4)
    for k in range(TOP_K):
        pltpu.make_async_copy(h_ref, xs_ref.at[pl.ds(0, tm)], sem).wait()


def _row_scatter(h, dest, p_rows, tm):
    n, d = h.shape
    return pl.pallas_call(
        functools.partial(_row_scatter_kernel, tm=tm),
        grid=(n // tm,),
        in_specs=[pl.BlockSpec((TOP_K * tm,), lambda i: (i,), memory_space=pltpu.SMEM),
                  pl.BlockSpec((tm, d), lambda i: (i, 0)),
                  pl.BlockSpec(memory_space=pl.ANY)],
        out_specs=pl.BlockSpec(memory_space=pl.ANY),
        out_shape=jax.ShapeDtypeStruct((p_rows, d), h.dtype),
        input_output_aliases={2: 0},
        scratch_shapes=[pltpu.SemaphoreType.DMA(())],
        compiler_params=_params("arbitrary"),
        name="moe_dispatch",
    )(dest, h, jnp.zeros((p_rows, d), h.dtype))


def _experts_kernel(te_ref, live_ref, x_ref, wgu_ref, wd_ref, o_ref, *, f):
    i = pl.program_id(0)

    @pl.when(live_ref[i] != 0)
    def _():
        o_ref[...] = _swiglu_tile(x_ref[...].astype(BF16), wgu_ref.at[0], wd_ref.at[0], f)

    @pl.when(live_ref[i] == 0)
    def _():
        o_ref[...] = jnp.zeros_like(o_ref)


def _experts(xs, tile_expert, tile_live, wgu, wd, tile):
    p_rows, d = xs.shape
    n_exp, f, _ = wd.shape
    grid_spec = pltpu.PrefetchScalarGridSpec(
        num_scalar_prefetch=2,
        grid=(p_rows // tile,),
        in_specs=[
            pl.BlockSpec((tile, d), lambda i, te, lv: (i, 0)),
            pl.BlockSpec((1, d, 2 * f), lambda i, te, lv: (te[i], 0, 0)),
            pl.BlockSpec((1, f, d), lambda i, te, lv: (te[i], 0, 0)),
        ],
        out_specs=pl.BlockSpec((tile, d), lambda i, te, lv: (i, 0)),
    )
    return pl.pallas_call(
        functools.partial(_experts_kernel, f=f),
        grid_spec=grid_spec,
        out_shape=jax.ShapeDtypeStruct((p_rows, d), F32),
        compiler_params=_params("arbitrary"),
        name="moe_experts",
    )(tile_expert, tile_live, xs, wgu, wd)


def _combine_kernel(dest_ref, x_ref, g_ref, fn_ref, w_ref, ys_ref, o_ref, buf_ref, sem, *, tm):
    def row_copy(r, k):
        return pltpu.make_async_copy(ys_ref.at[pl.ds(dest_ref[TOP_K * r + k], 1)],
                                     buf_ref.at[k, pl.ds(r, 1)], sem)

    def issue(r, carry):
        for k in range(TOP_K):
            row_copy(r, k).start(priority=k % 2)
        return carry

    lax.fori_loop(0, tm, issue, 0, unroll=4)
    for k in range(TOP_K):
        pltpu.make_async_copy(ys_ref.at[pl.ds(0, tm)], buf_ref.at[k], sem).wait()
    w = w_ref[...]
    y = w[:, 0:1] * buf_ref[0]
    for k in range(1, TOP_K):
        y = y + w[:, k:k + 1] * buf_ref[k]
    o_ref[...] = _rms(x_ref[...] + g_ref[0] * y) * fn_ref[...]


def _combine(x, g, final_norm, wts, ys, dest, tm):
    b, s, d = x.shape
    tiles_per_batch = s // tm
    return pl.pallas_call(
        functools.partial(_combine_kernel, tm=tm),
        grid=(b * tiles_per_batch,),
        in_specs=[
            pl.BlockSpec((TOP_K * tm,), lambda i: (i,), memory_space=pltpu.SMEM),
            pl.BlockSpec((tm, d), lambda i: (i, 0)),
            pl.BlockSpec((1, 1, d), lambda i: (i // tiles_per_batch, 0, 0)),
            pl.BlockSpec((1, d), lambda i: (0, 0)),
            pl.BlockSpec((tm, LANES), lambda i: (i, 0)),
            pl.BlockSpec(memory_space=pl.ANY),
        ],
        out_specs=pl.BlockSpec((tm, d), lambda i: (i, 0)),
        out_shape=jax.ShapeDtypeStruct((b * s, d), F32),
        scratch_shapes=[pltpu.VMEM((TOP_K, tm, d), F32), pltpu.SemaphoreType.DMA(())],
        compiler_params=_params("arbitrary"),
        name="moe_combine",
    )(dest, x.reshape(b * s, d), g, final_norm.reshape(1, d), wts.reshape(b * s, LANES), ys).reshape(b, s, d)


def kernel(x, c, w_ada, b_ada, na_w_qkv, na_rpb, na_w_o, ffn_w_gu, ffn_w_down, mla_w_down, mla_q_norm,
           mla_w_uq, mla_kv_norm, mla_w_ukv, mla_w_o, moe_w_router, moe_w_gu, moe_w_down, final_norm):
    b, s, d = x.shape
    assert w_ada.shape[0] == 2, "one neighbourhood-attention layer followed by one latent-attention layer"
    t = _tiles(s)
    mods = _adaln(c, w_ada, b_ada)

    def split(layer):
        return [v.reshape(b, 1, d) for v in jnp.split(mods[layer], 6, axis=-1)]

    sh_a, sc_a, g_a, sh_f, sc_f, g_f = split(0)
    qkv = _ln_matmul(x, sh_a, sc_a, na_w_qkv[0].astype(BF16), t.wide)
    att = _neighborhood_attention(qkv, na_rpb[0])
    x = _proj_ffn(att, na_w_o[0].astype(BF16), x, g_a, sh_f, sc_f, g_f,
                  ffn_w_gu[0].astype(BF16), ffn_w_down[0].astype(BF16), t.token)

    sh_a, sc_a, g_a, sh_f, sc_f, g_f = split(1)
    q_t, k, v_t = _mla_proj(x, sh_a, sc_a, mla_w_down[0], mla_q_norm[0], mla_w_uq[0], mla_kv_norm[0],
                            mla_w_ukv[0], t.token)
    o_t = _mla_attention(q_t, k, v_t, t.mla_q, t.mla_k)
    x, h, idx, wts = _proj_router(o_t, mla_w_o[0].astype(BF16), x, g_a, sh_f, sc_f, moe_w_router[0], t.wide)
    n_exp = moe_w_router.shape[-1]
    dest, tile_expert, tile_live, p_rows = _route_plan(idx[..., :TOP_K].reshape(b * s, TOP_K), n_exp, t.token)
    xs = _row_scatter(h.reshape(b * s, d), dest, p_rows, t.wide)
    ys = _experts(xs, tile_expert, tile_live, moe_w_gu[0].astype(BF16), moe_w_down[0].astype(BF16), t.token)
    return _combine(x, g_f, final_norm, wts, ys, dest, t.wide)
```

```python
import functools
from typing import NamedTuple

import jax
import jax.numpy as jnp
from jax import lax
from jax.experimental import pallas as pl
from jax.experimental.pallas import tpu as pltpu

F32 = jnp.float32
BF16 = jnp.bfloat16

GRID_W = 64
NA_HEADS = 16
NA_WIN_H = 8
NA_WIN_W = 16
MLA_HEADS = 16
MLA_NOPE = 64
MLA_ROPE = 32
MLA_V = 64
ROPE_THETA = 10000.0
TOP_K = 2
NORM_EPS = 1e-6

LANES = 128
SUBLANES = 8
MXU_DIM = 256
SUM_ROWS = 2 * SUBLANES
VMEM_LIMIT = 56 * 1024 * 1024

NA_ROWS_PER_STEP = 8
ROUTER_CHUNK = 256
MLA_SCORE_AHEAD = 2
MLA_SCORE_SLOTS = 4
ADALN_COLS = 1536
NEG_BIG = -1e30
LOG2E = 1.4426950408889634


def _params(*sem):
    return pltpu.CompilerParams(dimension_semantics=sem, vmem_limit_bytes=VMEM_LIMIT)


class _Tiles(NamedTuple):
    token: int
    wide: int
    mla_q: int
    mla_k: int


def _tiles(s):
    return _Tiles(token=min(512, s), wide=min(1024, s), mla_q=min(512, s), mla_k=256)


def _rms(x):
    return x * lax.rsqrt(jnp.mean(x * x, axis=-1, keepdims=True) + NORM_EPS)


def _modulate(x, sh, sc):
    return _rms(x) * (1.0 + sc) + sh


def _split_bf16(a):
    hi = a.astype(BF16)
    return hi, (a - hi.astype(F32)).astype(BF16)


def _dot_3pass(a, b):
    a_hi, a_lo = _split_bf16(a)
    b_hi, b_lo = _split_bf16(b)
    return (jnp.dot(a_hi, b_hi, preferred_element_type=F32)
            + (jnp.dot(a_lo, b_hi, preferred_element_type=F32)
               + jnp.dot(a_hi, b_lo, preferred_element_type=F32)))


def _adaln_kernel(c_ref, w_ref, b_ref, o_ref):
    c = c_ref[...]
    ca = c * jax.nn.sigmoid(c)
    o_ref[0] = _dot_3pass(ca, w_ref[0]) + b_ref[0]


def _adaln(c, w_ada, b_ada):
    depth, d, n = w_ada.shape
    b = c.shape[0]
    rows = SUBLANES
    assert b <= rows
    cp = jnp.zeros((rows, d), F32).at[:b].set(c)
    tn = ADALN_COLS
    out = pl.pallas_call(
        _adaln_kernel,
        grid=(depth, n // tn),
        in_specs=[
            pl.BlockSpec((rows, d), lambda l, j: (0, 0)),
            pl.BlockSpec((1, d, tn), lambda l, j: (l, 0, j)),
            pl.BlockSpec((1, 1, tn), lambda l, j: (l, 0, j)),
        ],
        out_specs=pl.BlockSpec((1, rows, tn), lambda l, j: (l, 0, j)),
        out_shape=jax.ShapeDtypeStruct((depth, rows, n), F32),
        compiler_params=_params("arbitrary", "arbitrary"),
        name="adaln",
    )(cp, w_ada, b_ada.reshape(depth, 1, n))
    return out[:, :b]


def _ln_matmul_kernel(x_ref, sh_ref, sc_ref, w_ref, o_ref):
    h = _modulate(x_ref[0], sh_ref[0], sc_ref[0]).astype(BF16)
    o_ref[0] = jnp.dot(h, w_ref[...], preferred_element_type=F32).astype(o_ref.dtype)


def _ln_matmul(x, sh, sc, w, tm):
    b, s, d = x.shape
    n = w.shape[1]
    vec = pl.BlockSpec((1, 1, d), lambda bi, i: (bi, 0, 0))
    return pl.pallas_call(
        _ln_matmul_kernel,
        grid=(b, s // tm),
        in_specs=[
            pl.BlockSpec((1, tm, d), lambda bi, i: (bi, i, 0)),
            vec, vec,
            pl.BlockSpec((d, n), lambda bi, i: (0, 0)),
        ],
        out_specs=pl.BlockSpec((1, tm, n), lambda bi, i: (bi, i, 0)),
        out_shape=jax.ShapeDtypeStruct((b, s, n), BF16),
        compiler_params=_params("arbitrary", "arbitrary"),
        name="ln_matmul",
    )(x, sh, sc, w)


def _na_kernel(q_ref, k_ref, v_ref, t_ref, o_ref, *, rows):
    win = NA_WIN_H * GRID_W
    lane = lax.broadcasted_iota(jnp.int32, (GRID_W, LANES), 1)
    first = lane < (LANES // 2)
    scale = (LANES // 2) ** -0.5 * LOG2E

    def bias(pat):
        d0 = NA_WIN_H - 1 - pat
        return jnp.concatenate(
            [jnp.concatenate([t_ref[0, d0 + 2 * j], t_ref[1, d0 + 2 * j]], axis=0)
             for j in range(NA_WIN_H // 2)], axis=1)

    def body(i, carry):
        rr = [i * NA_ROWS_PER_STEP + j for j in range(NA_ROWS_PER_STEP)]
        rss = [jnp.clip(r - NA_WIN_H // 2, 0, rows - NA_WIN_H) for r in rr]
        ss = []
        for r, rs in zip(rr, rss):
            q = q_ref[0, pl.ds(pl.multiple_of(r * GRID_W, GRID_W), GRID_W), :]
            zero = jnp.zeros_like(q)
            q2 = jnp.concatenate([jnp.where(first, q, zero), jnp.where(first, zero, q)], axis=0)
            kw = k_ref[0, pl.ds(pl.multiple_of(rs * GRID_W, GRID_W), win), :]
            s = lax.dot_general(q2, kw, (((1,), (1,)), ((), ())), preferred_element_type=F32)
            ss.append(s * scale + bias(r - rs))
        ps, ls = [], []
        for s in ss:
            p = jnp.exp2(s - jnp.max(s, axis=-1, keepdims=True))
            ls.append(jnp.sum(p, axis=-1, keepdims=True))
            ps.append(p.astype(BF16))
        for r, rs, p, l in zip(rr, rss, ps, ls):
            vw = v_ref[0, pl.ds(pl.multiple_of(rs * GRID_W, GRID_W), win), :]
            o = jnp.dot(p, vw, preferred_element_type=F32) / l
            o = jnp.where(first, o[:GRID_W], o[GRID_W:])
            o_ref[0, pl.ds(pl.multiple_of(r * GRID_W, GRID_W), GRID_W), :] = o.astype(o_ref.dtype)
        return carry

    lax.fori_loop(0, rows // NA_ROWS_PER_STEP, body, 0)


def _na_bias_table(rpb):
    h, ndr, ndc = rpb.shape
    qc = jnp.arange(GRID_W)
    kc = jnp.arange(GRID_W)
    cs = jnp.clip(qc - NA_WIN_W // 2, 0, GRID_W - NA_WIN_W)
    inwin = (kc[None, :] >= cs[:, None]) & (kc[None, :] < cs[:, None] + NA_WIN_W)
    period = 2 * GRID_W
    u = jnp.concatenate([rpb[:, :, NA_WIN_W - 1:], jnp.zeros((h, ndr, period - ndc), F32),
                         rpb[:, :, :NA_WIN_W - 1]], axis=-1)
    flat = jnp.tile(u, (1, 1, GRID_W))[:, :, :GRID_W * (period - 1)]
    col = flat.reshape(h, ndr, GRID_W, period - 1)[:, :, :, :GRID_W]
    col = jnp.where(inwin[None, None], col * LOG2E, NEG_BIG)
    return jnp.concatenate([col[:, :-1], col[:, 1:]], axis=-1)


def _neighborhood_attention(qkv, rpb):
    b, s, d3 = qkv.shape
    d = d3 // 3
    rows = s // GRID_W
    assert rows >= NA_WIN_H and d // NA_HEADS == LANES // 2 and rows % NA_ROWS_PER_STEP == 0
    nslab = d // LANES
    table = _na_bias_table(rpb.astype(F32))
    win = NA_WIN_H * GRID_W
    return pl.pallas_call(
        functools.partial(_na_kernel, rows=rows),
        grid=(b, nslab),
        in_specs=[
            pl.BlockSpec((1, s, LANES), lambda bi, hp: (bi, 0, hp)),
            pl.BlockSpec((1, s, LANES), lambda bi, hp: (bi, 0, nslab + hp)),
            pl.BlockSpec((1, s, LANES), lambda bi, hp: (bi, 0, 2 * nslab + hp)),
            pl.BlockSpec((2,) + table.shape[1:], lambda bi, hp: (hp, 0, 0, 0)),
        ],
        out_specs=pl.BlockSpec((1, s, LANES), lambda bi, hp: (bi, 0, hp)),
        out_shape=jax.ShapeDtypeStruct((b, s, d), BF16),
        compiler_params=_params("arbitrary", "arbitrary"),
        name="na_attn",
    )(qkv, qkv, qkv, table)


def _swiglu_tile(h, wgu, wd, f):
    acc = None
    for c in range(f // MXU_DIM):
        lo = c * MXU_DIM
        g = jnp.dot(h, wgu[:, lo:lo + MXU_DIM], preferred_element_type=F32)
        u = jnp.dot(h, wgu[:, f + lo:f + lo + MXU_DIM], preferred_element_type=F32)
        a = (g * jax.nn.sigmoid(g) * u).astype(BF16)
        y = jnp.dot(a, wd[lo:lo + MXU_DIM, :], preferred_element_type=F32)
        acc = y if acc is None else acc + y
    return acc


def _ffn_kernel(a_ref, wo_ref, x_ref, ga_ref, sh_ref, sc_ref, g_ref, wgu_ref, wd_ref, o_ref, *, f):
    x = x_ref[0] + ga_ref[0] * jnp.dot(a_ref[0], wo_ref[...], preferred_element_type=F32)
    h = _modulate(x, sh_ref[0], sc_ref[0]).astype(BF16)
    o_ref[0] = x + g_ref[0] * _swiglu_tile(h, wgu_ref, wd_ref, f)


def _proj_ffn(att, w_o, x, g_a, sh, sc, g, wgu, wd, tm):
    b, s, d = x.shape
    f = wd.shape[0]
    vec = pl.BlockSpec((1, 1, d), lambda bi, i: (bi, 0, 0))
    tile = pl.BlockSpec((1, tm, d), lambda bi, i: (bi, i, 0))
    return pl.pallas_call(
        functools.partial(_ffn_kernel, f=f),
        grid=(b, s // tm),
        in_specs=[
            tile,
            pl.BlockSpec((d, d), lambda bi, i: (0, 0)),
            tile,
            vec, vec, vec, vec,
            pl.BlockSpec((d, 2 * f), lambda bi, i: (0, 0)),
            pl.BlockSpec((f, d), lambda bi, i: (0, 0)),
        ],
        out_specs=tile,
        out_shape=jax.ShapeDtypeStruct((b, s, d), F32),
        compiler_params=_params("arbitrary", "arbitrary"),
        name="proj_ffn",
    )(att, w_o, x, g_a, sh, sc, g, wgu, wd)


def _mla_proj_kernel(x_ref, sh_ref, sc_ref, wd_ref, qn_ref, kvn_ref, wq_ref, wk_ref, wv_ref,
                     cs_ref, sn_ref, cst_ref, snt_ref, q_ref, k_ref, v_ref, *, q_lora, kv_lora, qscale):
    h = _modulate(x_ref[0], sh_ref[0], sc_ref[0]).astype(BF16)
    down = jnp.dot(h, wd_ref[...], preferred_element_type=F32)
    cq = (_rms(down[:, :q_lora]) * qn_ref[...]).astype(BF16)
    ckv = (_rms(down[:, q_lora:q_lora + kv_lora]) * kvn_ref[...]).astype(BF16)
    r0 = q_lora + kv_lora
    kr = down[:, r0:r0 + LANES] * cs_ref[...] + down[:, r0 + LANES:r0 + 2 * LANES] * sn_ref[...]
    z = jnp.concatenate([ckv, kr.astype(BF16)], axis=1)
    k_ref[0] = jnp.dot(z, wk_ref[...], preferred_element_type=F32).astype(BF16)
    nt = (((1,), (1,)), ((), ()))
    v_ref[0] = lax.dot_general(wv_ref[...], ckv, nt, preferred_element_type=F32).astype(BF16)
    qt = lax.dot_general(wq_ref[...], cq, nt, preferred_element_type=F32)
    cst = cst_ref[...]
    snt = snt_ref[...]
    for hd in range(MLA_HEADS):
        base = hd * LANES
        blk = qt[base:base + LANES]
        rot = blk[MLA_NOPE:MLA_NOPE + MLA_ROPE] * cst + blk[MLA_NOPE + MLA_ROPE:] * snt
        q_ref[0, base:base + MLA_NOPE, :] = (blk[:MLA_NOPE] * qscale).astype(BF16)
        q_ref[0, base + MLA_NOPE:base + MLA_NOPE + MLA_ROPE, :] = (rot * qscale).astype(BF16)
        q_ref[0, base + MLA_NOPE + MLA_ROPE:base + LANES, :] = jnp.zeros(
            (LANES - MLA_NOPE - MLA_ROPE, rot.shape[1]), BF16)


def _rope_perm_weights(w_rope):
    x1 = w_rope[..., 0::2]
    x2 = w_rope[..., 1::2]
    return jnp.concatenate([x1, x2], axis=-1), jnp.concatenate([-x2, x1], axis=-1)


def _mla_proj(x, sh, sc, w_down, q_norm, w_uq, kv_norm, w_ukv, tm):
    b, s, d = x.shape
    q_lora = q_norm.shape[0]
    kv_lora = kv_norm.shape[0]
    hd_q = MLA_NOPE + MLA_ROPE
    assert hd_q <= LANES and MLA_ROPE % 2 == 0 and q_lora % LANES == 0 and kv_lora == LANES

    r_a, r_b = _rope_perm_weights(w_down[:, q_lora + kv_lora:])
    pad = jnp.zeros((d, LANES - MLA_ROPE), F32)
    wd_ext = jnp.concatenate([w_down[:, :q_lora + kv_lora], r_a, pad, r_b, pad], axis=1).astype(BF16)

    wq = w_uq.reshape(q_lora, MLA_HEADS, hd_q)
    ra, rb = _rope_perm_weights(wq[..., MLA_NOPE:])
    wq_ext = jnp.concatenate([wq[..., :MLA_NOPE], ra, rb], axis=-1)
    wq_ext_t = wq_ext.reshape(q_lora, MLA_HEADS * LANES).T.astype(BF16)

    wkv = w_ukv.reshape(kv_lora, MLA_HEADS, MLA_NOPE + MLA_V)
    top = jnp.concatenate([wkv[..., :MLA_NOPE], jnp.zeros((kv_lora, MLA_HEADS, LANES - MLA_NOPE), F32)], axis=-1)
    copy = jnp.concatenate([jnp.zeros((LANES, MLA_NOPE), F32), jnp.eye(LANES, MLA_ROPE, dtype=F32),
                            jnp.zeros((LANES, LANES - MLA_NOPE - MLA_ROPE), F32)], axis=1)
    bot = jnp.broadcast_to(copy[:, None, :], (LANES, MLA_HEADS, LANES))
    wk_ext = jnp.concatenate([top, bot], axis=0).reshape(2 * LANES, MLA_HEADS * LANES).astype(BF16)
    wv_t = wkv[:, :, MLA_NOPE:].reshape(kv_lora, MLA_HEADS * MLA_V).T.astype(BF16)

    t = jnp.arange(s)
    row = (t // GRID_W).astype(F32)
    col = (t % GRID_W).astype(F32)
    nf = MLA_ROPE // 4
    inv = ROPE_THETA ** (-jnp.arange(nf, dtype=F32) / nf)
    ang = jnp.concatenate([row[:, None] * inv, col[:, None] * inv], axis=-1)
    cos2 = jnp.concatenate([jnp.cos(ang), jnp.cos(ang)], axis=1)
    sin2 = jnp.concatenate([jnp.sin(ang), jnp.sin(ang)], axis=1)
    lpad = jnp.zeros((s, LANES - MLA_ROPE), F32)
    cos_tok = jnp.concatenate([cos2, lpad], axis=1)
    sin_tok = jnp.concatenate([sin2, lpad], axis=1)
    cos_t = cos2.T
    sin_t = sin2.T

    qscale = float(hd_q ** -0.5 * LOG2E)
    nq = MLA_HEADS * LANES
    nv = MLA_HEADS * MLA_V
    nd = wd_ext.shape[1]
    full = lambda shape: pl.BlockSpec(shape, lambda bi, i: (0,) * len(shape))
    vec = pl.BlockSpec((1, 1, d), lambda bi, i: (bi, 0, 0))
    return pl.pallas_call(
        functools.partial(_mla_proj_kernel, q_lora=q_lora, kv_lora=kv_lora, qscale=qscale),
        grid=(b, s // tm),
        in_specs=[
            pl.BlockSpec((1, tm, d), lambda bi, i: (bi, i, 0)),
            vec, vec,
            full((d, nd)), full((1, q_lora)), full((1, kv_lora)),
            full((nq, q_lora)), full((2 * LANES, nq)), full((nv, kv_lora)),
            pl.BlockSpec((tm, LANES), lambda bi, i: (i, 0)),
            pl.BlockSpec((tm, LANES), lambda bi, i: (i, 0)),
            pl.BlockSpec((MLA_ROPE, tm), lambda bi, i: (0, i)),
            pl.BlockSpec((MLA_ROPE, tm), lambda bi, i: (0, i)),
        ],
        out_specs=[
            pl.BlockSpec((1, nq, tm), lambda bi, i: (bi, 0, i)),
            pl.BlockSpec((1, tm, nq), lambda bi, i: (bi, i, 0)),
            pl.BlockSpec((1, nv, tm), lambda bi, i: (bi, 0, i)),
        ],
        out_shape=[
            jax.ShapeDtypeStruct((b, nq, s), BF16),
            jax.ShapeDtypeStruct((b, s, nq), BF16),
            jax.ShapeDtypeStruct((b, nv, s), BF16),
        ],
        compiler_params=_params("arbitrary", "arbitrary"),
        name="mla_proj",
    )(x, sh, sc, wd_ext, q_norm.reshape(1, -1), kv_norm.reshape(1, -1), wq_ext_t, wk_ext, wv_t,
      cos_tok, sin_tok, cos_t, sin_t)


def _mla_attn_kernel(q_ref, k_ref, v_ref, o_ref, *s_refs, tq, tk, nkt, nq):
    ones = jnp.ones((SUM_ROWS, tk), BF16)

    def scores(qi, kt):
        qt = q_ref[0, :, pl.ds(pl.multiple_of(qi * tq, tq), tq)]
        s_refs[kt % MLA_SCORE_SLOTS][...] = jnp.dot(k_ref[0, kt * tk:(kt + 1) * tk, :], qt,
                                                    preferred_element_type=F32)

    def softmax_pv(kt, carry):
        m, acc = carry
        st = s_refs[kt % MLA_SCORE_SLOTS][...]
        m_new = jnp.maximum(m, jnp.max(st, axis=0, keepdims=True))
        p = jnp.exp2(st - m_new).astype(BF16)
        vb = jnp.concatenate([v_ref[0, :, kt * tk:(kt + 1) * tk], ones], axis=0)
        acc = jnp.exp2(m - m_new) * acc + jnp.dot(vb, p, preferred_element_type=F32)
        return m_new, acc

    for kt in range(MLA_SCORE_AHEAD):
        scores(0, kt)

    def body(qi, carry):
        nxt = jnp.minimum(qi + 1, nq - 1)
        carry = (jnp.full((1, tq), NEG_BIG, F32), jnp.zeros((MLA_V + SUM_ROWS, tq), F32))
        for kt in range(nkt):
            ahead = kt + MLA_SCORE_AHEAD
            if ahead < nkt:
                scores(qi, ahead)
            else:
                scores(nxt, ahead - nkt)
            carry = softmax_pv(kt, carry)
        m, acc = carry
        out = acc[:MLA_V] / acc[MLA_V:MLA_V + 1]
        o_ref[0, :, pl.ds(pl.multiple_of(qi * tq, tq), tq)] = out.astype(o_ref.dtype)
        return 0

    lax.fori_loop(0, nq, body, 0)


def _mla_attention(q_t, k, v_t, tq, tk):
    b, _, s = q_t.shape
    nv = v_t.shape[1]
    nkt = s // tk
    assert nkt % MLA_SCORE_SLOTS == 0 and nkt >= MLA_SCORE_AHEAD
    return pl.pallas_call(
        functools.partial(_mla_attn_kernel, tq=tq, tk=tk, nkt=nkt, nq=s // tq),
        grid=(b, MLA_HEADS),
        in_specs=[
            pl.BlockSpec((1, LANES, s), lambda bi, hd: (bi, hd, 0)),
            pl.BlockSpec((1, s, LANES), lambda bi, hd: (bi, 0, hd)),
            pl.BlockSpec((1, MLA_V, s), lambda bi, hd: (bi, hd, 0)),
        ],
        out_specs=pl.BlockSpec((1, MLA_V, s), lambda bi, hd: (bi, hd, 0)),
        out_shape=jax.ShapeDtypeStruct((b, nv, s), BF16),
        scratch_shapes=[pltpu.VMEM((tk, tq), F32)] * MLA_SCORE_SLOTS,
        compiler_params=_params("arbitrary", "arbitrary"),
        name="mla_attn",
    )(q_t, k, v_t)


def _router_kernel(a_ref, wo_ref, x_ref, ga_ref, sh_ref, sc_ref, wr_ref, xo_ref, i_ref, w_ref, *, n_exp):
    tm = x_ref.shape[1]
    chunks = [slice(c, c + ROUTER_CHUNK) for c in range(0, tm, ROUTER_CHUNK)]
    w = wr_ref[...]
    w_hi, w_lo = _split_bf16(w)
    lane = lax.broadcasted_iota(jnp.int32, (ROUTER_CHUNK, LANES), 1)
    valid = lane < n_exp

    hs = []
    for rows in chunks:
        y = lax.dot_general(a_ref[0, :, rows], wo_ref[...], (((0,), (0,)), ((), ())),
                            preferred_element_type=F32)
        x = x_ref[0, rows] + ga_ref[0] * y
        xo_ref[0, rows] = x
        hs.append(_modulate(x, sh_ref[0], sc_ref[0]))

    all_logits = []
    for h in hs:
        h_hi, h_lo = _split_bf16(h)
        all_logits.append(jnp.dot(h_hi, w_hi, preferred_element_type=F32)
                          + (jnp.dot(h_lo, w_hi, preferred_element_type=F32)
                             + jnp.dot(h_hi, w_lo, preferred_element_type=F32)))

    rests = [jnp.where(valid, logits, -jnp.inf) for logits in all_logits]
    vals = [[] for _ in chunks]
    idxs = [[] for _ in chunks]
    for _ in range(TOP_K):
        for c in range(len(chunks)):
            v = jnp.max(rests[c], axis=-1, keepdims=True)
            idx = jnp.min(jnp.where(rests[c] == v, lane, LANES), axis=-1, keepdims=True)
            vals[c].append(v)
            idxs[c].append(idx)
            rests[c] = jnp.where(lane == idx, -jnp.inf, rests[c])
    for c, rows in enumerate(chunks):
        ev = [jnp.exp(v - vals[c][0]) for v in vals[c]]
        tot = ev[0]
        for v in ev[1:]:
            tot = tot + v
        i_out = jnp.zeros((ROUTER_CHUNK, LANES), jnp.int32)
        w_out = jnp.zeros((ROUTER_CHUNK, LANES), F32)
        for k in range(TOP_K):
            i_out = jnp.where(lane == k, idxs[c][k], i_out)
            w_out = jnp.where(lane == k, ev[k] / tot, w_out)
        i_ref[0, rows] = i_out
        w_ref[0, rows] = w_out


def _proj_router(o_t, w_o, x, g_a, sh, sc, w_router, tm):
    b, s, d = x.shape
    k = w_o.shape[0]
    n_exp = w_router.shape[1]
    wr = jnp.zeros((d, LANES), F32).at[:, :n_exp].set(w_router)
    vec = pl.BlockSpec((1, 1, d), lambda bi, i: (bi, 0, 0))
    tile = pl.BlockSpec((1, tm, d), lambda bi, i: (bi, i, 0))
    lanes_spec = pl.BlockSpec((1, tm, LANES), lambda bi, i: (bi, i, 0))
    return pl.pallas_call(
        functools.partial(_router_kernel, n_exp=n_exp),
        grid=(b, s // tm),
        in_specs=[
            pl.BlockSpec((1, k, tm), lambda bi, i: (bi, 0, i)),
            pl.BlockSpec((k, d), lambda bi, i: (0, 0)),
            tile,
            vec, vec, vec,
            pl.BlockSpec((d, LANES), lambda bi, i: (0, 0)),
        ],
        out_specs=[tile, lanes_spec, lanes_spec],
        out_shape=[jax.ShapeDtypeStruct((b, s, d), F32),
                   jax.ShapeDtypeStruct((b, s, LANES), jnp.int32),
                   jax.ShapeDtypeStruct((b, s, LANES), F32)],
        compiler_params=_params("arbitrary", "arbitrary"),
        name="proj_router",
    )(o_t, w_o, x, g_a, sh, sc, wr)


def _route_plan(idx, n_exp, tile):
    t, k = idx.shape
    e_flat = idx.reshape(t * k)
    onehot = (e_flat[:, None] == jnp.arange(n_exp, dtype=jnp.int32)[None, :]).astype(jnp.int32)
    csum = jnp.cumsum(onehot, axis=0)
    counts = csum[-1]
    rank = jnp.sum(onehot * csum, axis=1) - 1
    padded = ((counts + tile - 1) // tile) * tile
    gend = jnp.cumsum(padded)
    gstart = gend - padded
    dest = jnp.sum(onehot * gstart[None, :], axis=1) + rank
    p_rows = t * k + n_exp * tile
    tile_start = jnp.arange(p_rows // tile, dtype=jnp.int32) * tile
    tile_expert = jnp.minimum(jnp.sum((tile_start[:, None] >= gend[None, :]).astype(jnp.int32), axis=1),
                              n_exp - 1)
    tile_live = (tile_start < gend[-1]).astype(jnp.int32)
    return dest.astype(jnp.int32), tile_expert, tile_live, p_rows


def _row_scatter_kernel(dest_ref, x_ref, sh_ref, sc_ref, xs_in_ref, xs_ref, h_ref, sem, *, tm):
    del xs_in_ref
    h_ref[...] = _modulate(x_ref[...], sh_ref[0], sc_ref[0])

    def row_copy(r, k):
        return pltpu.make_async_copy(h_ref.at[pl.ds(r, 1)], xs_ref.at[pl.ds(dest_ref[TOP_K * r + k], 1)], sem)

    def issue(r, carry):
        for k in range(TOP_K):
            row_copy(r, k).start(priority=k % 2)
        return carry

    lax.fori_loop(0, tm, issue, 0, unroll=4)
    for k in range(TOP_K):
        pltpu.make_async_copy(h_ref, xs_ref.at[pl.ds(0, tm)], sem).wait()


def _row_scatter(x, sh, sc, dest, p_rows, tm):
    b, s, d = x.shape
    tiles_per_batch = s // tm
    vec = pl.BlockSpec((1, 1, d), lambda i: (i // tiles_per_batch, 0, 0))
    return pl.pallas_call(
        functools.partial(_row_scatter_kernel, tm=tm),
        grid=(b * tiles_per_batch,),
        in_specs=[pl.BlockSpec((TOP_K * tm,), lambda i: (i,), memory_space=pltpu.SMEM),
                  pl.BlockSpec((tm, d), lambda i: (i, 0)),
                  vec, vec,
                  pl.BlockSpec(memory_space=pl.ANY)],
        out_specs=pl.BlockSpec(memory_space=pl.ANY),
        out_shape=jax.ShapeDtypeStruct((p_rows, d), F32),
        input_output_aliases={4: 0},
        scratch_shapes=[pltpu.VMEM((tm, d), F32), pltpu.SemaphoreType.DMA(())],
        compiler_params=_params("arbitrary"),
        name="moe_dispatch",
    )(dest, x.reshape(b * s, d), sh, sc, jnp.zeros((p_rows, d), F32))


def _experts_kernel(te_ref, live_ref, x_ref, wgu_ref, wd_ref, o_ref, *, f):
    i = pl.program_id(0)

    @pl.when(live_ref[i] != 0)
    def _():
        o_ref[...] = _swiglu_tile(x_ref[...].astype(BF16), wgu_ref.at[0], wd_ref.at[0], f)

    @pl.when(live_ref[i] == 0)
    def _():
        o_ref[...] = jnp.zeros_like(o_ref)


def _experts(xs, tile_expert, tile_live, wgu, wd, tile):
    p_rows, d = xs.shape
    n_exp, f, _ = wd.shape
    grid_spec = pltpu.PrefetchScalarGridSpec(
        num_scalar_prefetch=2,
        grid=(p_rows // tile,),
        in_specs=[
            pl.BlockSpec((tile, d), lambda i, te, lv: (i, 0)),
            pl.BlockSpec((1, d, 2 * f), lambda i, te, lv: (te[i], 0, 0)),
            pl.BlockSpec((1, f, d), lambda i, te, lv: (te[i], 0, 0)),
        ],
        out_specs=pl.BlockSpec((tile, d), lambda i, te, lv: (i, 0)),
    )
    return pl.pallas_call(
        functools.partial(_experts_kernel, f=f),
        grid_spec=grid_spec,
        out_shape=jax.ShapeDtypeStruct((p_rows, d), F32),
        compiler_params=_params("arbitrary"),
        name="moe_experts",
    )(tile_expert, tile_live, xs, wgu, wd)


def _combine_kernel(dest_ref, x_ref, g_ref, fn_ref, w_ref, ys_ref, o_ref, buf_ref, sem, *, tm):
    def row_copy(r, k):
        return pltpu.make_async_copy(ys_ref.at[pl.ds(dest_ref[TOP_K * r + k], 1)],
                                     buf_ref.at[k, pl.ds(r, 1)], sem)

    def issue(r, carry):
        for k in range(TOP_K):
            row_copy(r, k).start(priority=k % 2)
        return carry

    lax.fori_loop(0, tm, issue, 0, unroll=4)
    for k in range(TOP_K):
        pltpu.make_async_copy(ys_ref.at[pl.ds(0, tm)], buf_ref.at[k], sem).wait()
    w = w_ref[...]
    y = w[:, 0:1] * buf_ref[0]
    for k in range(1, TOP_K):
        y = y + w[:, k:k + 1] * buf_ref[k]
    o_ref[...] = _rms(x_ref[...] + g_ref[0] * y) * fn_ref[...]


def _combine(x, g, final_norm, wts, ys, dest, tm):
    b, s, d = x.shape
    tiles_per_batch = s // tm
    return pl.pallas_call(
        functools.partial(_combine_kernel, tm=tm),
        grid=(b * tiles_per_batch,),
        in_specs=[
            pl.BlockSpec((TOP_K * tm,), lambda i: (i,), memory_space=pltpu.SMEM),
            pl.BlockSpec((tm, d), lambda i: (i, 0)),
            pl.BlockSpec((1, 1, d), lambda i: (i // tiles_per_batch, 0, 0)),
            pl.BlockSpec((1, d), lambda i: (0, 0)),
            pl.BlockSpec((tm, LANES), lambda i: (i, 0)),
            pl.BlockSpec(memory_space=pl.ANY),
        ],
        out_specs=pl.BlockSpec((tm, d), lambda i: (i, 0)),
        out_shape=jax.ShapeDtypeStruct((b * s, d), F32),
        scratch_shapes=[pltpu.VMEM((TOP_K, tm, d), F32), pltpu.SemaphoreType.DMA(())],
        compiler_params=_params("arbitrary"),
        name="moe_combine",
    )(dest, x.reshape(b * s, d), g, final_norm.reshape(1, d), wts.reshape(b * s, LANES), ys).reshape(b, s, d)


def kernel(x, c, w_ada, b_ada, na_w_qkv, na_rpb, na_w_o, ffn_w_gu, ffn_w_down, mla_w_down, mla_q_norm,
           mla_w_uq, mla_kv_norm, mla_w_ukv, mla_w_o, moe_w_router, moe_w_gu, moe_w_down, final_norm):
    b, s, d = x.shape
    assert w_ada.shape[0] == 2, "one neighbourhood-attention layer followed by one latent-attention layer"
    t = _tiles(s)
    mods = _adaln(c, w_ada, b_ada)

    def split(layer):
        return [v.reshape(b, 1, d) for v in jnp.split(mods[layer], 6, axis=-1)]

    sh_a, sc_a, g_a, sh_f, sc_f, g_f = split(0)
    qkv = _ln_matmul(x, sh_a, sc_a, na_w_qkv[0].astype(BF16), t.wide)
    att = _neighborhood_attention(qkv, na_rpb[0])
    x = _proj_ffn(att, na_w_o[0].astype(BF16), x, g_a, sh_f, sc_f, g_f,
                  ffn_w_gu[0].astype(BF16), ffn_w_down[0].astype(BF16), t.token)

    sh_a, sc_a, g_a, sh_f, sc_f, g_f = split(1)
    q_t, k, v_t = _mla_proj(x, sh_a, sc_a, mla_w_down[0], mla_q_norm[0], mla_w_uq[0], mla_kv_norm[0],
                            mla_w_ukv[0], t.token)
    o_t = _mla_attention(q_t, k, v_t, t.mla_q, t.mla_k)
    x, idx, wts = _proj_router(o_t, mla_w_o[0].astype(BF16), x, g_a, sh_f, sc_f, moe_w_router[0], t.wide)
    n_exp = moe_w_router.shape[-1]
    dest, tile_expert, tile_live, p_rows = _route_plan(idx[..., :TOP_K].reshape(b * s, TOP_K), n_exp, t.token)
    xs = _row_scatter(x, sh_f, sc_f, dest, p_rows, t.wide)
    ys = _experts(xs, tile_expert, tile_live, moe_w_gu[0].astype(BF16), moe_w_down[0].astype(BF16), t.token)
    return _combine(x, g_f, final_norm, wts, ys, dest, t.wide)
```

```python
import functools
from typing import NamedTuple

import jax
import jax.numpy as jnp
from jax import lax
from jax.experimental import pallas as pl
from jax.experimental.pallas import tpu as pltpu

F32 = jnp.float32
BF16 = jnp.bfloat16

GRID_W = 64
NA_HEADS = 16
NA_WIN_H = 8
NA_WIN_W = 16
MLA_HEADS = 16
MLA_NOPE = 64
MLA_ROPE = 32
MLA_V = 64
ROPE_THETA = 10000.0
TOP_K = 2
NORM_EPS = 1e-6

LANES = 128
SUBLANES = 8
MXU_DIM = 256
SUM_ROWS = 2 * SUBLANES
VMEM_LIMIT = 56 * 1024 * 1024

NA_ROWS_PER_STEP = 8
ROUTER_CHUNK = 256
MLA_PROJ_CHUNK = 256
MLA_SCORE_AHEAD = 2
MLA_SCORE_SLOTS = 4
ADALN_COLS = 1536
NEG_BIG = -1e30
LOG2E = 1.4426950408889634


def _params(*sem):
    return pltpu.CompilerParams(dimension_semantics=sem, vmem_limit_bytes=VMEM_LIMIT)


class _Tiles(NamedTuple):
    token: int
    wide: int
    mla_q: int
    mla_k: int


def _tiles(s):
    return _Tiles(token=min(512, s), wide=min(1024, s), mla_q=min(512, s), mla_k=256)


def _rms(x):
    return x * lax.rsqrt(jnp.mean(x * x, axis=-1, keepdims=True) + NORM_EPS)


def _modulate(x, sh, sc):
    return _rms(x) * (1.0 + sc) + sh


def _split_bf16(a):
    hi = a.astype(BF16)
    return hi, (a - hi.astype(F32)).astype(BF16)


def _dot_3pass(a, b):
    a_hi, a_lo = _split_bf16(a)
    b_hi, b_lo = _split_bf16(b)
    return (jnp.dot(a_hi, b_hi, preferred_element_type=F32)
            + (jnp.dot(a_lo, b_hi, preferred_element_type=F32)
               + jnp.dot(a_hi, b_lo, preferred_element_type=F32)))


def _adaln_kernel(c_ref, w_ref, b_ref, o_ref):
    c = c_ref[...]
    ca = c * jax.nn.sigmoid(c)
    o_ref[0] = _dot_3pass(ca, w_ref[0]) + b_ref[0]


def _adaln(c, w_ada, b_ada):
    depth, d, n = w_ada.shape
    b = c.shape[0]
    rows = SUBLANES
    assert b <= rows
    cp = jnp.zeros((rows, d), F32).at[:b].set(c)
    tn = ADALN_COLS
    out = pl.pallas_call(
        _adaln_kernel,
        grid=(depth, n // tn),
        in_specs=[
            pl.BlockSpec((rows, d), lambda l, j: (0, 0)),
            pl.BlockSpec((1, d, tn), lambda l, j: (l, 0, j)),
            pl.BlockSpec((1, 1, tn), lambda l, j: (l, 0, j)),
        ],
        out_specs=pl.BlockSpec((1, rows, tn), lambda l, j: (l, 0, j)),
        out_shape=jax.ShapeDtypeStruct((depth, rows, n), F32),
        compiler_params=_params("arbitrary", "arbitrary"),
        name="adaln",
    )(cp, w_ada, b_ada.reshape(depth, 1, n))
    return out[:, :b]


def _ln_matmul_kernel(x_ref, sh_ref, sc_ref, w_ref, o_ref):
    h = _modulate(x_ref[0], sh_ref[0], sc_ref[0]).astype(BF16)
    o_ref[0] = jnp.dot(h, w_ref[...], preferred_element_type=F32).astype(o_ref.dtype)


def _ln_matmul(x, sh, sc, w, tm):
    b, s, d = x.shape
    n = w.shape[1]
    vec = pl.BlockSpec((1, 1, d), lambda bi, i: (bi, 0, 0))
    return pl.pallas_call(
        _ln_matmul_kernel,
        grid=(b, s // tm),
        in_specs=[
            pl.BlockSpec((1, tm, d), lambda bi, i: (bi, i, 0)),
            vec, vec,
            pl.BlockSpec((d, n), lambda bi, i: (0, 0)),
        ],
        out_specs=pl.BlockSpec((1, tm, n), lambda bi, i: (bi, i, 0)),
        out_shape=jax.ShapeDtypeStruct((b, s, n), BF16),
        compiler_params=_params("arbitrary", "arbitrary"),
        name="ln_matmul",
    )(x, sh, sc, w)


def _na_kernel(q_ref, k_ref, v_ref, t_ref, o_ref, *, rows):
    win = NA_WIN_H * GRID_W
    lane = lax.broadcasted_iota(jnp.int32, (GRID_W, LANES), 1)
    first = lane < (LANES // 2)
    scale = (LANES // 2) ** -0.5 * LOG2E

    def bias(pat):
        d0 = NA_WIN_H - 1 - pat
        return jnp.concatenate(
            [jnp.concatenate([t_ref[0, d0 + 2 * j], t_ref[1, d0 + 2 * j]], axis=0)
             for j in range(NA_WIN_H // 2)], axis=1)

    def body(i, carry):
        rr = [i * NA_ROWS_PER_STEP + j for j in range(NA_ROWS_PER_STEP)]
        rss = [jnp.clip(r - NA_WIN_H // 2, 0, rows - NA_WIN_H) for r in rr]
        ss = []
        for r, rs in zip(rr, rss):
            q = q_ref[0, pl.ds(pl.multiple_of(r * GRID_W, GRID_W), GRID_W), :]
            zero = jnp.zeros_like(q)
            q2 = jnp.concatenate([jnp.where(first, q, zero), jnp.where(first, zero, q)], axis=0)
            kw = k_ref[0, pl.ds(pl.multiple_of(rs * GRID_W, GRID_W), win), :]
            s = lax.dot_general(q2, kw, (((1,), (1,)), ((), ())), preferred_element_type=F32)
            ss.append(s * scale + bias(r - rs))
        ps, ls = [], []
        for s in ss:
            p = jnp.exp2(s - jnp.max(s, axis=-1, keepdims=True))
            ls.append(jnp.sum(p, axis=-1, keepdims=True))
            ps.append(p.astype(BF16))
        for r, rs, p, l in zip(rr, rss, ps, ls):
            vw = v_ref[0, pl.ds(pl.multiple_of(rs * GRID_W, GRID_W), win), :]
            o = jnp.dot(p, vw, preferred_element_type=F32) / l
            o = jnp.where(first, o[:GRID_W], o[GRID_W:])
            o_ref[0, pl.ds(pl.multiple_of(r * GRID_W, GRID_W), GRID_W), :] = o.astype(o_ref.dtype)
        return carry

    lax.fori_loop(0, rows // NA_ROWS_PER_STEP, body, 0)


def _na_bias_table(rpb):
    h, ndr, ndc = rpb.shape
    qc = jnp.arange(GRID_W)
    kc = jnp.arange(GRID_W)
    cs = jnp.clip(qc - NA_WIN_W // 2, 0, GRID_W - NA_WIN_W)
    inwin = (kc[None, :] >= cs[:, None]) & (kc[None, :] < cs[:, None] + NA_WIN_W)
    period = 2 * GRID_W
    u = jnp.concatenate([rpb[:, :, NA_WIN_W - 1:], jnp.zeros((h, ndr, period - ndc), F32),
                         rpb[:, :, :NA_WIN_W - 1]], axis=-1)
    flat = jnp.tile(u, (1, 1, GRID_W))[:, :, :GRID_W * (period - 1)]
    col = flat.reshape(h, ndr, GRID_W, period - 1)[:, :, :, :GRID_W]
    col = jnp.where(inwin[None, None], col * LOG2E, NEG_BIG)
    return jnp.concatenate([col[:, :-1], col[:, 1:]], axis=-1)


def _neighborhood_attention(qkv, rpb):
    b, s, d3 = qkv.shape
    d = d3 // 3
    rows = s // GRID_W
    assert rows >= NA_WIN_H and d // NA_HEADS == LANES // 2 and rows % NA_ROWS_PER_STEP == 0
    nslab = d // LANES
    table = _na_bias_table(rpb.astype(F32))
    win = NA_WIN_H * GRID_W
    return pl.pallas_call(
        functools.partial(_na_kernel, rows=rows),
        grid=(b, nslab),
        in_specs=[
            pl.BlockSpec((1, s, LANES), lambda bi, hp: (bi, 0, hp)),
            pl.BlockSpec((1, s, LANES), lambda bi, hp: (bi, 0, nslab + hp)),
            pl.BlockSpec((1, s, LANES), lambda bi, hp: (bi, 0, 2 * nslab + hp)),
            pl.BlockSpec((2,) + table.shape[1:], lambda bi, hp: (hp, 0, 0, 0)),
        ],
        out_specs=pl.BlockSpec((1, s, LANES), lambda bi, hp: (bi, 0, hp)),
        out_shape=jax.ShapeDtypeStruct((b, s, d), BF16),
        compiler_params=_params("arbitrary", "arbitrary"),
        name="na_attn",
    )(qkv, qkv, qkv, table)


def _swiglu_tile(h, wgu, wd, f):
    acc = None
    for c in range(f // MXU_DIM):
        lo = c * MXU_DIM
        g = jnp.dot(h, wgu[:, lo:lo + MXU_DIM], preferred_element_type=F32)
        u = jnp.dot(h, wgu[:, f + lo:f + lo + MXU_DIM], preferred_element_type=F32)
        a = (g * jax.nn.sigmoid(g) * u).astype(BF16)
        y = jnp.dot(a, wd[lo:lo + MXU_DIM, :], preferred_element_type=F32)
        acc = y if acc is None else acc + y
    return acc


def _ffn_kernel(a_ref, wo_ref, x_ref, ga_ref, sh_ref, sc_ref, g_ref, wgu_ref, wd_ref, o_ref, *, f):
    x = x_ref[0] + ga_ref[0] * jnp.dot(a_ref[0], wo_ref[...], preferred_element_type=F32)
    h = _modulate(x, sh_ref[0], sc_ref[0]).astype(BF16)
    o_ref[0] = x + g_ref[0] * _swiglu_tile(h, wgu_ref, wd_ref, f)


def _proj_ffn(att, w_o, x, g_a, sh, sc, g, wgu, wd, tm):
    b, s, d = x.shape
    f = wd.shape[0]
    vec = pl.BlockSpec((1, 1, d), lambda bi, i: (bi, 0, 0))
    tile = pl.BlockSpec((1, tm, d), lambda bi, i: (bi, i, 0))
    return pl.pallas_call(
        functools.partial(_ffn_kernel, f=f),
        grid=(b, s // tm),
        in_specs=[
            tile,
            pl.BlockSpec((d, d), lambda bi, i: (0, 0)),
            tile,
            vec, vec, vec, vec,
            pl.BlockSpec((d, 2 * f), lambda bi, i: (0, 0)),
            pl.BlockSpec((f, d), lambda bi, i: (0, 0)),
        ],
        out_specs=tile,
        out_shape=jax.ShapeDtypeStruct((b, s, d), F32),
        compiler_params=_params("arbitrary", "arbitrary"),
        name="proj_ffn",
    )(att, w_o, x, g_a, sh, sc, g, wgu, wd)


def _mla_proj_kernel(x_ref, sh_ref, sc_ref, wd_ref, qn_ref, kvn_ref, wq_ref, wk_ref, wv_ref,
                     cs_ref, sn_ref, cst_ref, snt_ref, q_ref, k_ref, v_ref, *, q_lora, kv_lora, qscale):
    tm = x_ref.shape[1]
    chunks = [slice(c, c + MLA_PROJ_CHUNK) for c in range(0, tm, MLA_PROJ_CHUNK)]
    r0 = q_lora + kv_lora
    nt = (((1,), (1,)), ((), ()))
    downs = []
    for rows in chunks:
        h = _modulate(x_ref[0, rows], sh_ref[0], sc_ref[0]).astype(BF16)
        downs.append(jnp.dot(h, wd_ref[...], preferred_element_type=F32))
    lat = []
    for rows, down in zip(chunks, downs):
        cq = (_rms(down[:, :q_lora]) * qn_ref[...]).astype(BF16)
        ckv = (_rms(down[:, q_lora:r0]) * kvn_ref[...]).astype(BF16)
        kr = down[:, r0:r0 + LANES] * cs_ref[rows] + down[:, r0 + LANES:r0 + 2 * LANES] * sn_ref[rows]
        lat.append((cq, ckv, jnp.concatenate([ckv, kr.astype(BF16)], axis=1)))
    qts = []
    for rows, (cq, ckv, z) in zip(chunks, lat):
        k_ref[0, rows] = jnp.dot(z, wk_ref[...], preferred_element_type=F32).astype(BF16)
        v_ref[0, :, rows] = lax.dot_general(wv_ref[...], ckv, nt, preferred_element_type=F32).astype(BF16)
        qts.append(lax.dot_general(wq_ref[...], cq, nt, preferred_element_type=F32))
    for rows, qt in zip(chunks, qts):
        cst = cst_ref[:, rows]
        snt = snt_ref[:, rows]
        for hd in range(MLA_HEADS):
            base = hd * LANES
            blk = qt[base:base + LANES]
            rot = blk[MLA_NOPE:MLA_NOPE + MLA_ROPE] * cst + blk[MLA_NOPE + MLA_ROPE:] * snt
            q_ref[0, base:base + MLA_NOPE, rows] = (blk[:MLA_NOPE] * qscale).astype(BF16)
            q_ref[0, base + MLA_NOPE:base + MLA_NOPE + MLA_ROPE, rows] = (rot * qscale).astype(BF16)
            q_ref[0, base + MLA_NOPE + MLA_ROPE:base + LANES, rows] = jnp.zeros(
                (LANES - MLA_NOPE - MLA_ROPE, rot.shape[1]), BF16)


def _rope_perm_weights(w_rope):
    x1 = w_rope[..., 0::2]
    x2 = w_rope[..., 1::2]
    return jnp.concatenate([x1, x2], axis=-1), jnp.concatenate([-x2, x1], axis=-1)


def _mla_proj(x, sh, sc, w_down, q_norm, w_uq, kv_norm, w_ukv, tm):
    b, s, d = x.shape
    q_lora = q_norm.shape[0]
    kv_lora = kv_norm.shape[0]
    hd_q = MLA_NOPE + MLA_ROPE
    assert hd_q <= LANES and MLA_ROPE % 2 == 0 and q_lora % LANES == 0 and kv_lora == LANES

    r_a, r_b = _rope_perm_weights(w_down[:, q_lora + kv_lora:])
    pad = jnp.zeros((d, LANES - MLA_ROPE), F32)
    wd_ext = jnp.concatenate([w_down[:, :q_lora + kv_lora], r_a, pad, r_b, pad], axis=1).astype(BF16)

    wq = w_uq.reshape(q_lora, MLA_HEADS, hd_q)
    ra, rb = _rope_perm_weights(wq[..., MLA_NOPE:])
    wq_ext = jnp.concatenate([wq[..., :MLA_NOPE], ra, rb], axis=-1)
    wq_ext_t = wq_ext.reshape(q_lora, MLA_HEADS * LANES).T.astype(BF16)

    wkv = w_ukv.reshape(kv_lora, MLA_HEADS, MLA_NOPE + MLA_V)
    top = jnp.concatenate([wkv[..., :MLA_NOPE], jnp.zeros((kv_lora, MLA_HEADS, LANES - MLA_NOPE), F32)], axis=-1)
    copy = jnp.concatenate([jnp.zeros((LANES, MLA_NOPE), F32), jnp.eye(LANES, MLA_ROPE, dtype=F32),
                            jnp.zeros((LANES, LANES - MLA_NOPE - MLA_ROPE), F32)], axis=1)
    bot = jnp.broadcast_to(copy[:, None, :], (LANES, MLA_HEADS, LANES))
    wk_ext = jnp.concatenate([top, bot], axis=0).reshape(2 * LANES, MLA_HEADS * LANES).astype(BF16)
    wv_t = wkv[:, :, MLA_NOPE:].reshape(kv_lora, MLA_HEADS * MLA_V).T.astype(BF16)

    t = jnp.arange(s)
    row = (t // GRID_W).astype(F32)
    col = (t % GRID_W).astype(F32)
    nf = MLA_ROPE // 4
    inv = ROPE_THETA ** (-jnp.arange(nf, dtype=F32) / nf)
    ang = jnp.concatenate([row[:, None] * inv, col[:, None] * inv], axis=-1)
    cos2 = jnp.concatenate([jnp.cos(ang), jnp.cos(ang)], axis=1)
    sin2 = jnp.concatenate([jnp.sin(ang), jnp.sin(ang)], axis=1)
    lpad = jnp.zeros((s, LANES - MLA_ROPE), F32)
    cos_tok = jnp.concatenate([cos2, lpad], axis=1)
    sin_tok = jnp.concatenate([sin2, lpad], axis=1)
    cos_t = cos2.T
    sin_t = sin2.T

    qscale = float(hd_q ** -0.5 * LOG2E)
    nq = MLA_HEADS * LANES
    nv = MLA_HEADS * MLA_V
    nd = wd_ext.shape[1]
    full = lambda shape: pl.BlockSpec(shape, lambda bi, i: (0,) * len(shape))
    vec = pl.BlockSpec((1, 1, d), lambda bi, i: (bi, 0, 0))
    return pl.pallas_call(
        functools.partial(_mla_proj_kernel, q_lora=q_lora, kv_lora=kv_lora, qscale=qscale),
        grid=(b, s // tm),
        in_specs=[
            pl.BlockSpec((1, tm, d), lambda bi, i: (bi, i, 0)),
            vec, vec,
            full((d, nd)), full((1, q_lora)), full((1, kv_lora)),
            full((nq, q_lora)), full((2 * LANES, nq)), full((nv, kv_lora)),
            pl.BlockSpec((tm, LANES), lambda bi, i: (i, 0)),
            pl.BlockSpec((tm, LANES), lambda bi, i: (i, 0)),
            pl.BlockSpec((MLA_ROPE, tm), lambda bi, i: (0, i)),
            pl.BlockSpec((MLA_ROPE, tm), lambda bi, i: (0, i)),
        ],
        out_specs=[
            pl.BlockSpec((1, nq, tm), lambda bi, i: (bi, 0, i)),
            pl.BlockSpec((1, tm, nq), lambda bi, i: (bi, i, 0)),
            pl.BlockSpec((1, nv, tm), lambda bi, i: (bi, 0, i)),
        ],
        out_shape=[
            jax.ShapeDtypeStruct((b, nq, s), BF16),
            jax.ShapeDtypeStruct((b, s, nq), BF16),
            jax.ShapeDtypeStruct((b, nv, s), BF16),
        ],
        compiler_params=_params("arbitrary", "arbitrary"),
        name="mla_proj",
    )(x, sh, sc, wd_ext, q_norm.reshape(1, -1), kv_norm.reshape(1, -1), wq_ext_t, wk_ext, wv_t,
      cos_tok, sin_tok, cos_t, sin_t)


def _mla_attn_kernel(q_ref, k_ref, v_ref, o_ref, *s_refs, tq, tk, nkt, nq):
    ones = jnp.ones((SUM_ROWS, tk), BF16)

    def scores(qi, kt):
        qt = q_ref[0, :, pl.ds(pl.multiple_of(qi * tq, tq), tq)]
        st = jnp.dot(k_ref[0, kt * tk:(kt + 1) * tk, :], qt, preferred_element_type=F32)
        s_refs[kt % MLA_SCORE_SLOTS][...] = st
        return jnp.max(st, axis=0, keepdims=True)

    def softmax_pv(kt, tile_max, carry):
        m, acc = carry
        m_new = jnp.maximum(m, tile_max)
        p = jnp.exp2(s_refs[kt % MLA_SCORE_SLOTS][...] - m_new).astype(BF16)
        vb = jnp.concatenate([v_ref[0, :, kt * tk:(kt + 1) * tk], ones], axis=0)
        acc = jnp.exp2(m - m_new) * acc + jnp.dot(vb, p, preferred_element_type=F32)
        return m_new, acc

    in_flight = tuple(scores(0, kt) for kt in range(MLA_SCORE_AHEAD))

    def body(qi, in_flight):
        in_flight = list(in_flight)
        nxt = jnp.minimum(qi + 1, nq - 1)
        carry = (jnp.full((1, tq), NEG_BIG, F32), jnp.zeros((MLA_V + SUM_ROWS, tq), F32))
        for kt in range(nkt):
            ahead = kt + MLA_SCORE_AHEAD
            in_flight.append(scores(qi, ahead) if ahead < nkt else scores(nxt, ahead - nkt))
            carry = softmax_pv(kt, in_flight.pop(0), carry)
        m, acc = carry
        out = acc[:MLA_V] / acc[MLA_V:MLA_V + 1]
        o_ref[0, :, pl.ds(pl.multiple_of(qi * tq, tq), tq)] = out.astype(o_ref.dtype)
        return tuple(in_flight)

    lax.fori_loop(0, nq, body, in_flight)


def _mla_attention(q_t, k, v_t, tq, tk):
    b, _, s = q_t.shape
    nv = v_t.shape[1]
    nkt = s // tk
    assert nkt % MLA_SCORE_SLOTS == 0 and nkt >= MLA_SCORE_AHEAD
    return pl.pallas_call(
        functools.partial(_mla_attn_kernel, tq=tq, tk=tk, nkt=nkt, nq=s // tq),
        grid=(b, MLA_HEADS),
        in_specs=[
            pl.BlockSpec((1, LANES, s), lambda bi, hd: (bi, hd, 0)),
            pl.BlockSpec((1, s, LANES), lambda bi, hd: (bi, 0, hd)),
            pl.BlockSpec((1, MLA_V, s), lambda bi, hd: (bi, hd, 0)),
        ],
        out_specs=pl.BlockSpec((1, MLA_V, s), lambda bi, hd: (bi, hd, 0)),
        out_shape=jax.ShapeDtypeStruct((b, nv, s), BF16),
        scratch_shapes=[pltpu.VMEM((tk, tq), F32)] * MLA_SCORE_SLOTS,
        compiler_params=_params("arbitrary", "arbitrary"),
        name="mla_attn",
    )(q_t, k, v_t)


def _router_kernel(a_ref, wo_ref, x_ref, ga_ref, sh_ref, sc_ref, wr_ref, xo_ref, h_ref, i_ref, w_ref, *, n_exp):
    tm = x_ref.shape[1]
    chunks = [slice(c, c + ROUTER_CHUNK) for c in range(0, tm, ROUTER_CHUNK)]
    w = wr_ref[...]
    w_hi, w_lo = _split_bf16(w)
    lane = lax.broadcasted_iota(jnp.int32, (ROUTER_CHUNK, LANES), 1)
    valid = lane < n_exp

    hs = []
    for rows in chunks:
        y = lax.dot_general(a_ref[0, :, rows], wo_ref[...], (((0,), (0,)), ((), ())),
                            preferred_element_type=F32)
        x = x_ref[0, rows] + ga_ref[0] * y
        xo_ref[0, rows] = x
        h = _modulate(x, sh_ref[0], sc_ref[0])
        h_ref[0, rows] = h
        hs.append(h)

    all_logits = []
    for h in hs:
        h_hi, h_lo = _split_bf16(h)
        all_logits.append(jnp.dot(h_hi, w_hi, preferred_element_type=F32)
                          + (jnp.dot(h_lo, w_hi, preferred_element_type=F32)
                             + jnp.dot(h_hi, w_lo, preferred_element_type=F32)))

    rests = [jnp.where(valid, logits, -jnp.inf) for logits in all_logits]
    vals = [[] for _ in chunks]
    idxs = [[] for _ in chunks]
    for _ in range(TOP_K):
        for c in range(len(chunks)):
            v = jnp.max(rests[c], axis=-1, keepdims=True)
            idx = jnp.min(jnp.where(rests[c] == v, lane, LANES), axis=-1, keepdims=True)
            vals[c].append(v)
            idxs[c].append(idx)
            rests[c] = jnp.where(lane == idx, -jnp.inf, rests[c])
    for c, rows in enumerate(chunks):
        ev = [jnp.exp(v - vals[c][0]) for v in vals[c]]
        tot = ev[0]
        for v in ev[1:]:
            tot = tot + v
        i_out = jnp.zeros((ROUTER_CHUNK, LANES), jnp.int32)
        w_out = jnp.zeros((ROUTER_CHUNK, LANES), F32)
        for k in range(TOP_K):
            i_out = jnp.where(lane == k, idxs[c][k], i_out)
            w_out = jnp.where(lane == k, ev[k] / tot, w_out)
        i_ref[0, rows] = i_out
        w_ref[0, rows] = w_out


def _proj_router(o_t, w_o, x, g_a, sh, sc, w_router, tm):
    b, s, d = x.shape
    k = w_o.shape[0]
    n_exp = w_router.shape[1]
    wr = jnp.zeros((d, LANES), F32).at[:, :n_exp].set(w_router)
    vec = pl.BlockSpec((1, 1, d), lambda bi, i: (bi, 0, 0))
    tile = pl.BlockSpec((1, tm, d), lambda bi, i: (bi, i, 0))
    lanes_spec = pl.BlockSpec((1, tm, LANES), lambda bi, i: (bi, i, 0))
    return pl.pallas_call(
        functools.partial(_router_kernel, n_exp=n_exp),
        grid=(b, s // tm),
        in_specs=[
            pl.BlockSpec((1, k, tm), lambda bi, i: (bi, 0, i)),
            pl.BlockSpec((k, d), lambda bi, i: (0, 0)),
            tile,
            vec, vec, vec,
            pl.BlockSpec((d, LANES), lambda bi, i: (0, 0)),
        ],
        out_specs=[tile, tile, lanes_spec, lanes_spec],
        out_shape=[jax.ShapeDtypeStruct((b, s, d), F32),
                   jax.ShapeDtypeStruct((b, s, d), F32),
                   jax.ShapeDtypeStruct((b, s, LANES), jnp.int32),
                   jax.ShapeDtypeStruct((b, s, LANES), F32)],
        compiler_params=_params("arbitrary", "arbitrary"),
        name="proj_router",
    )(o_t, w_o, x, g_a, sh, sc, wr)


def _route_plan(idx, n_exp, tile):
    t, k = idx.shape
    e_flat = idx.reshape(t * k)
    onehot = (e_flat[:, None] == jnp.arange(n_exp, dtype=jnp.int32)[None, :]).astype(jnp.int32)
    csum = jnp.cumsum(onehot, axis=0)
    counts = csum[-1]
    rank = jnp.sum(onehot * csum, axis=1) - 1
    padded = ((counts + tile - 1) // tile) * tile
    gend = jnp.cumsum(padded)
    gstart = gend - padded
    dest = jnp.sum(onehot * gstart[None, :], axis=1) + rank
    p_rows = t * k + n_exp * tile
    tile_start = jnp.arange(p_rows // tile, dtype=jnp.int32) * tile
    tile_expert = jnp.minimum(jnp.sum((tile_start[:, None] >= gend[None, :]).astype(jnp.int32), axis=1),
                              n_exp - 1)
    tile_live = (tile_start < gend[-1]).astype(jnp.int32)
    return dest.astype(jnp.int32), tile_expert, tile_live, p_rows


def _row_scatter_kernel(dest_ref, h_ref, xs_in_ref, xs_ref, sem, *, tm):
    del xs_in_ref

    def row_copy(r, k):
        return pltpu.make_async_copy(h_ref.at[pl.ds(r, 1)], xs_ref.at[pl.ds(dest_ref[TOP_K * r + k], 1)], sem)

    def issue(r, carry):
        for k in range(TOP_K):
            row_copy(r, k).start(priority=k % 2)
        return carry

    lax.fori_loop(0, tm, issue, 0, unroll=4)
    for k in range(TOP_K):
        pltpu.make_async_copy(h_ref, xs_ref.at[pl.ds(0, tm)], sem).wait()


def _row_scatter(h, dest, p_rows, tm):
    n, d = h.shape
    return pl.pallas_call(
        functools.partial(_row_scatter_kernel, tm=tm),
        grid=(n // tm,),
        in_specs=[pl.BlockSpec((TOP_K * tm,), lambda i: (i,), memory_space=pltpu.SMEM),
                  pl.BlockSpec((tm, d), lambda i: (i, 0)),
                  pl.BlockSpec(memory_space=pl.ANY)],
        out_specs=pl.BlockSpec(memory_space=pl.ANY),
        out_shape=jax.ShapeDtypeStruct((p_rows, d), h.dtype),
        input_output_aliases={2: 0},
        scratch_shapes=[pltpu.SemaphoreType.DMA(())],
        compiler_params=_params("arbitrary"),
        name="moe_dispatch",
    )(dest, h, jnp.zeros((p_rows, d), h.dtype))


def _experts_kernel(te_ref, live_ref, x_ref, wgu_ref, wd_ref, o_ref, *, f):
    i = pl.program_id(0)

    @pl.when(live_ref[i] != 0)
    def _():
        o_ref[...] = _swiglu_tile(x_ref[...].astype(BF16), wgu_ref.at[0], wd_ref.at[0], f)

    @pl.when(live_ref[i] == 0)
    def _():
        o_ref[...] = jnp.zeros_like(o_ref)


def _experts(xs, tile_expert, tile_live, wgu, wd, tile):
    p_rows, d = xs.shape
    n_exp, f, _ = wd.shape
    grid_spec = pltpu.PrefetchScalarGridSpec(
        num_scalar_prefetch=2,
        grid=(p_rows // tile,),
        in_specs=[
            pl.BlockSpec((tile, d), lambda i, te, lv: (i, 0)),
            pl.BlockSpec((1, d, 2 * f), lambda i, te, lv: (te[i], 0, 0)),
            pl.BlockSpec((1, f, d), lambda i, te, lv: (te[i], 0, 0)),
        ],
        out_specs=pl.BlockSpec((tile, d), lambda i, te, lv: (i, 0)),
    )
    return pl.pallas_call(
        functools.partial(_experts_kernel, f=f),
        grid_spec=grid_spec,
        out_shape=jax.ShapeDtypeStruct((p_rows, d), F32),
        compiler_params=_params("arbitrary"),
        name="moe_experts",
    )(tile_expert, tile_live, xs, wgu, wd)


def _combine_kernel(dest_ref, x_ref, g_ref, fn_ref, w_ref, ys_ref, o_ref, buf_ref, sem, *, tm):
    def row_copy(r, k):
        return pltpu.make_async_copy(ys_ref.at[pl.ds(dest_ref[TOP_K * r + k], 1)],
                                     buf_ref.at[k, pl.ds(r, 1)], sem)

    def issue(r, carry):
        for k in range(TOP_K):
            row_copy(r, k).start(priority=k % 2)
        return carry

    lax.fori_loop(0, tm, issue, 0, unroll=4)
    for k in range(TOP_K):
        pltpu.make_async_copy(ys_ref.at[pl.ds(0, tm)], buf_ref.at[k], sem).wait()
    w = w_ref[...]
    y = w[:, 0:1] * buf_ref[0]
    for k in range(1, TOP_K):
        y = y + w[:, k:k + 1] * buf_ref[k]
    o_ref[...] = _rms(x_ref[...] + g_ref[0] * y) * fn_ref[...]


def _combine(x, g, final_norm, wts, ys, dest, tm):
    b, s, d = x.shape
    tiles_per_batch = s // tm
    return pl.pallas_call(
        functools.partial(_combine_kernel, tm=tm),
        grid=(b * tiles_per_batch,),
        in_specs=[
            pl.BlockSpec((TOP_K * tm,), lambda i: (i,), memory_space=pltpu.SMEM),
            pl.BlockSpec((tm, d), lambda i: (i, 0)),
            pl.BlockSpec((1, 1, d), lambda i: (i // tiles_per_batch, 0, 0)),
            pl.BlockSpec((1, d), lambda i: (0, 0)),
            pl.BlockSpec((tm, LANES), lambda i: (i, 0)),
            pl.BlockSpec(memory_space=pl.ANY),
        ],
        out_specs=pl.BlockSpec((tm, d), lambda i: (i, 0)),
        out_shape=jax.ShapeDtypeStruct((b * s, d), F32),
        scratch_shapes=[pltpu.VMEM((TOP_K, tm, d), F32), pltpu.SemaphoreType.DMA(())],
        compiler_params=_params("arbitrary"),
        name="moe_combine",
    )(dest, x.reshape(b * s, d), g, final_norm.reshape(1, d), wts.reshape(b * s, LANES), ys).reshape(b, s, d)


def kernel(x, c, w_ada, b_ada, na_w_qkv, na_rpb, na_w_o, ffn_w_gu, ffn_w_down, mla_w_down, mla_q_norm,
           mla_w_uq, mla_kv_norm, mla_w_ukv, mla_w_o, moe_w_router, moe_w_gu, moe_w_down, final_norm):
    b, s, d = x.shape
    assert w_ada.shape[0] == 2, "one neighbourhood-attention layer followed by one latent-attention layer"
    t = _tiles(s)
    mods = _adaln(c, w_ada, b_ada)

    def split(layer):
        return [v.reshape(b, 1, d) for v in jnp.split(mods[layer], 6, axis=-1)]

    sh_a, sc_a, g_a, sh_f, sc_f, g_f = split(0)
    qkv = _ln_matmul(x, sh_a, sc_a, na_w_qkv[0].astype(BF16), t.wide)
    att = _neighborhood_attention(qkv, na_rpb[0])
    x = _proj_ffn(att, na_w_o[0].astype(BF16), x, g_a, sh_f, sc_f, g_f,
                  ffn_w_gu[0].astype(BF16), ffn_w_down[0].astype(BF16), t.token)

    sh_a, sc_a, g_a, sh_f, sc_f, g_f = split(1)
    q_t, k, v_t = _mla_proj(x, sh_a, sc_a, mla_w_down[0], mla_q_norm[0], mla_w_uq[0], mla_kv_norm[0],
                            mla_w_ukv[0], t.token)
    o_t = _mla_attention(q_t, k, v_t, t.mla_q, t.mla_k)
    x, h, idx, wts = _proj_router(o_t, mla_w_o[0].astype(BF16), x, g_a, sh_f, sc_f, moe_w_router[0], t.wide)
    n_exp = moe_w_router.shape[-1]
    dest, tile_expert, tile_live, p_rows = _route_plan(idx[..., :TOP_K].reshape(b * s, TOP_K), n_exp, t.token)
    xs = _row_scatter(h.reshape(b * s, d), dest, p_rows, t.wide)
    ys = _experts(xs, tile_expert, tile_live, moe_w_gu[0].astype(BF16), moe_w_down[0].astype(BF16), t.token)
    return _combine(x, g_f, final_norm, wts, ys, dest, t.wide)
```

```python
import functools
from typing import NamedTuple

import jax
import jax.numpy as jnp
from jax import lax
from jax.experimental import pallas as pl
from jax.experimental.pallas import tpu as pltpu

F32 = jnp.float32
BF16 = jnp.bfloat16

GRID_W = 64
NA_HEADS = 16
NA_WIN_H = 8
NA_WIN_W = 16
MLA_HEADS = 16
MLA_NOPE = 64
MLA_ROPE = 32
MLA_V = 64
ROPE_THETA = 10000.0
TOP_K = 2
NORM_EPS = 1e-6

LANES = 128
SUBLANES = 8
MXU_DIM = 256
SUM_ROWS = 2 * SUBLANES
VMEM_LIMIT = 56 * 1024 * 1024

NA_ROWS_PER_STEP = 32
NA_LOOKAHEAD = 3
ROUTER_CHUNK = 256
MLA_PROJ_CHUNK = 256
MLA_SCORE_AHEAD = 2
MLA_SCORE_SLOTS = 4
ADALN_COLS = 1536
NEG_BIG = -1e30
LOG2E = 1.4426950408889634


def _params(*sem):
    return pltpu.CompilerParams(dimension_semantics=sem, vmem_limit_bytes=VMEM_LIMIT)


class _Tiles(NamedTuple):
    token: int
    wide: int
    mla_q: int
    mla_k: int


def _tiles(s):
    return _Tiles(token=min(512, s), wide=min(1024, s), mla_q=min(512, s), mla_k=256)


def _rms(x):
    return x * lax.rsqrt(jnp.mean(x * x, axis=-1, keepdims=True) + NORM_EPS)


def _modulate(x, sh, sc):
    return _rms(x) * (1.0 + sc) + sh


def _split_bf16(a):
    hi = a.astype(BF16)
    return hi, (a - hi.astype(F32)).astype(BF16)


def _dot_3pass(a, b):
    a_hi, a_lo = _split_bf16(a)
    b_hi, b_lo = _split_bf16(b)
    return (jnp.dot(a_hi, b_hi, preferred_element_type=F32)
            + (jnp.dot(a_lo, b_hi, preferred_element_type=F32)
               + jnp.dot(a_hi, b_lo, preferred_element_type=F32)))


def _adaln_kernel(c_ref, w_ref, b_ref, o_ref):
    c = c_ref[...]
    ca = c * jax.nn.sigmoid(c)
    o_ref[0] = _dot_3pass(ca, w_ref[0]) + b_ref[0]


def _adaln(c, w_ada, b_ada):
    depth, d, n = w_ada.shape
    b = c.shape[0]
    rows = SUBLANES
    assert b <= rows
    cp = jnp.zeros((rows, d), F32).at[:b].set(c)
    tn = ADALN_COLS
    out = pl.pallas_call(
        _adaln_kernel,
        grid=(depth, n // tn),
        in_specs=[
            pl.BlockSpec((rows, d), lambda l, j: (0, 0)),
            pl.BlockSpec((1, d, tn), lambda l, j: (l, 0, j)),
            pl.BlockSpec((1, 1, tn), lambda l, j: (l, 0, j)),
        ],
        out_specs=pl.BlockSpec((1, rows, tn), lambda l, j: (l, 0, j)),
        out_shape=jax.ShapeDtypeStruct((depth, rows, n), F32),
        compiler_params=_params("arbitrary", "arbitrary"),
        name="adaln",
    )(cp, w_ada, b_ada.reshape(depth, 1, n))
    return out[:, :b]


def _ln_matmul_kernel(x_ref, sh_ref, sc_ref, w_ref, o_ref):
    h = _modulate(x_ref[0], sh_ref[0], sc_ref[0]).astype(BF16)
    o_ref[0] = jnp.dot(h, w_ref[...], preferred_element_type=F32).astype(o_ref.dtype)


def _ln_matmul(x, sh, sc, w, tm):
    b, s, d = x.shape
    n = w.shape[1]
    vec = pl.BlockSpec((1, 1, d), lambda bi, i: (bi, 0, 0))
    return pl.pallas_call(
        _ln_matmul_kernel,
        grid=(b, s // tm),
        in_specs=[
            pl.BlockSpec((1, tm, d), lambda bi, i: (bi, i, 0)),
            vec, vec,
            pl.BlockSpec((d, n), lambda bi, i: (0, 0)),
        ],
        out_specs=pl.BlockSpec((1, tm, n), lambda bi, i: (bi, i, 0)),
        out_shape=jax.ShapeDtypeStruct((b, s, n), BF16),
        compiler_params=_params("arbitrary", "arbitrary"),
        name="ln_matmul",
    )(x, sh, sc, w)


def _na_kernel(q_ref, k_ref, v_ref, t_ref, o_ref, *, rows, rows_per_step):
    win = NA_WIN_H * GRID_W
    lane = lax.broadcasted_iota(jnp.int32, (GRID_W, LANES), 1)
    first = lane < (LANES // 2)
    scale = (LANES // 2) ** -0.5 * LOG2E

    def bias(pat):
        d0 = NA_WIN_H - 1 - pat
        return jnp.concatenate(
            [jnp.concatenate([t_ref[0, d0 + 2 * j], t_ref[1, d0 + 2 * j]], axis=0)
             for j in range(NA_WIN_H // 2)], axis=1)

    def body(i, carry):
        rr = [i * rows_per_step + j for j in range(rows_per_step)]
        rss = [jnp.clip(r - NA_WIN_H // 2, 0, rows - NA_WIN_H) for r in rr]

        def scores(j):
            r, rs = rr[j], rss[j]
            q = q_ref[0, pl.ds(pl.multiple_of(r * GRID_W, GRID_W), GRID_W), :]
            zero = jnp.zeros_like(q)
            q2 = jnp.concatenate([jnp.where(first, q, zero), jnp.where(first, zero, q)], axis=0)
            kw = k_ref[0, pl.ds(pl.multiple_of(rs * GRID_W, GRID_W), win), :]
            s = lax.dot_general(q2, kw, (((1,), (1,)), ((), ())), preferred_element_type=F32)
            return s * scale + bias(r - rs)

        def softmax_pv(j, s):
            r, rs = rr[j], rss[j]
            p = jnp.exp2(s - jnp.max(s, axis=-1, keepdims=True))
            l = jnp.sum(p, axis=-1, keepdims=True)
            vw = v_ref[0, pl.ds(pl.multiple_of(rs * GRID_W, GRID_W), win), :]
            o = jnp.dot(p.astype(BF16), vw, preferred_element_type=F32) / l
            o = jnp.where(first, o[:GRID_W], o[GRID_W:])
            o_ref[0, pl.ds(pl.multiple_of(r * GRID_W, GRID_W), GRID_W), :] = o.astype(o_ref.dtype)

        in_flight = [scores(j) for j in range(min(NA_LOOKAHEAD, rows_per_step))]
        for j in range(rows_per_step):
            if j + NA_LOOKAHEAD < rows_per_step:
                in_flight.append(scores(j + NA_LOOKAHEAD))
            softmax_pv(j, in_flight.pop(0))
        return carry

    lax.fori_loop(0, rows // rows_per_step, body, 0)


def _na_bias_table(rpb):
    h, ndr, ndc = rpb.shape
    qc = jnp.arange(GRID_W)
    kc = jnp.arange(GRID_W)
    cs = jnp.clip(qc - NA_WIN_W // 2, 0, GRID_W - NA_WIN_W)
    inwin = (kc[None, :] >= cs[:, None]) & (kc[None, :] < cs[:, None] + NA_WIN_W)
    period = 2 * GRID_W
    u = jnp.concatenate([rpb[:, :, NA_WIN_W - 1:], jnp.zeros((h, ndr, period - ndc), F32),
                         rpb[:, :, :NA_WIN_W - 1]], axis=-1)
    flat = jnp.tile(u, (1, 1, GRID_W))[:, :, :GRID_W * (period - 1)]
    col = flat.reshape(h, ndr, GRID_W, period - 1)[:, :, :, :GRID_W]
    col = jnp.where(inwin[None, None], col * LOG2E, NEG_BIG)
    return jnp.concatenate([col[:, :-1], col[:, 1:]], axis=-1)


def _neighborhood_attention(qkv, rpb):
    b, s, d3 = qkv.shape
    d = d3 // 3
    rows = s // GRID_W
    rows_per_step = min(NA_ROWS_PER_STEP, rows)
    assert rows >= NA_WIN_H and d // NA_HEADS == LANES // 2 and rows % rows_per_step == 0
    nslab = d // LANES
    table = _na_bias_table(rpb.astype(F32))
    win = NA_WIN_H * GRID_W
    return pl.pallas_call(
        functools.partial(_na_kernel, rows=rows, rows_per_step=rows_per_step),
        grid=(b, nslab),
        in_specs=[
            pl.BlockSpec((1, s, LANES), lambda bi, hp: (bi, 0, hp)),
            pl.BlockSpec((1, s, LANES), lambda bi, hp: (bi, 0, nslab + hp)),
            pl.BlockSpec((1, s, LANES), lambda bi, hp: (bi, 0, 2 * nslab + hp)),
            pl.BlockSpec((2,) + table.shape[1:], lambda bi, hp: (hp, 0, 0, 0)),
        ],
        out_specs=pl.BlockSpec((1, s, LANES), lambda bi, hp: (bi, 0, hp)),
        out_shape=jax.ShapeDtypeStruct((b, s, d), BF16),
        compiler_params=_params("arbitrary", "arbitrary"),
        name="na_attn",
    )(qkv, qkv, qkv, table)


def _swiglu_tile(h, wgu, wd, f):
    acc = None
    for c in range(f // MXU_DIM):
        lo = c * MXU_DIM
        g = jnp.dot(h, wgu[:, lo:lo + MXU_DIM], preferred_element_type=F32)
        u = jnp.dot(h, wgu[:, f + lo:f + lo + MXU_DIM], preferred_element_type=F32)
        a = (g * jax.nn.sigmoid(g) * u).astype(BF16)
        y = jnp.dot(a, wd[lo:lo + MXU_DIM, :], preferred_element_type=F32)
        acc = y if acc is None else acc + y
    return acc


def _ffn_kernel(a_ref, wo_ref, x_ref, ga_ref, sh_ref, sc_ref, g_ref, wgu_ref, wd_ref, o_ref, *, f):
    x = x_ref[0] + ga_ref[0] * jnp.dot(a_ref[0], wo_ref[...], preferred_element_type=F32)
    h = _modulate(x, sh_ref[0], sc_ref[0]).astype(BF16)
    o_ref[0] = x + g_ref[0] * _swiglu_tile(h, wgu_ref, wd_ref, f)


def _proj_ffn(att, w_o, x, g_a, sh, sc, g, wgu, wd, tm):
    b, s, d = x.shape
    f = wd.shape[0]
    vec = pl.BlockSpec((1, 1, d), lambda bi, i: (bi, 0, 0))
    tile = pl.BlockSpec((1, tm, d), lambda bi, i: (bi, i, 0))
    return pl.pallas_call(
        functools.partial(_ffn_kernel, f=f),
        grid=(b, s // tm),
        in_specs=[
            tile,
            pl.BlockSpec((d, d), lambda bi, i: (0, 0)),
            tile,
            vec, vec, vec, vec,
            pl.BlockSpec((d, 2 * f), lambda bi, i: (0, 0)),
            pl.BlockSpec((f, d), lambda bi, i: (0, 0)),
        ],
        out_specs=tile,
        out_shape=jax.ShapeDtypeStruct((b, s, d), F32),
        compiler_params=_params("arbitrary", "arbitrary"),
        name="proj_ffn",
    )(att, w_o, x, g_a, sh, sc, g, wgu, wd)


def _mla_proj_kernel(x_ref, sh_ref, sc_ref, wd_ref, qn_ref, kvn_ref, wq_ref, wk_ref, wv_ref,
                     cs_ref, sn_ref, cst_ref, snt_ref, q_ref, k_ref, v_ref, *, q_lora, kv_lora, qscale):
    tm = x_ref.shape[1]
    chunks = [slice(c, c + MLA_PROJ_CHUNK) for c in range(0, tm, MLA_PROJ_CHUNK)]
    r0 = q_lora + kv_lora
    nt = (((1,), (1,)), ((), ()))
    downs = []
    for rows in chunks:
        h = _modulate(x_ref[0, rows], sh_ref[0], sc_ref[0]).astype(BF16)
        downs.append(jnp.dot(h, wd_ref[...], preferred_element_type=F32))
    lat = []
    for rows, down in zip(chunks, downs):
        cq = (_rms(down[:, :q_lora]) * qn_ref[...]).astype(BF16)
        ckv = (_rms(down[:, q_lora:r0]) * kvn_ref[...]).astype(BF16)
        kr = down[:, r0:r0 + LANES] * cs_ref[rows] + down[:, r0 + LANES:r0 + 2 * LANES] * sn_ref[rows]
        lat.append((cq, ckv, jnp.concatenate([ckv, kr.astype(BF16)], axis=1)))
    qts = []
    for rows, (cq, ckv, z) in zip(chunks, lat):
        k_ref[0, rows] = jnp.dot(z, wk_ref[...], preferred_element_type=F32).astype(BF16)
        v_ref[0, :, rows] = lax.dot_general(wv_ref[...], ckv, nt, preferred_element_type=F32).astype(BF16)
        qts.append(lax.dot_general(wq_ref[...], cq, nt, preferred_element_type=F32))
    for rows, qt in zip(chunks, qts):
        cst = cst_ref[:, rows]
        snt = snt_ref[:, rows]
        for hd in range(MLA_HEADS):
            base = hd * LANES
            blk = qt[base:base + LANES]
            rot = blk[MLA_NOPE:MLA_NOPE + MLA_ROPE] * cst + blk[MLA_NOPE + MLA_ROPE:] * snt
            q_ref[0, base:base + MLA_NOPE, rows] = (blk[:MLA_NOPE] * qscale).astype(BF16)
            q_ref[0, base + MLA_NOPE:base + MLA_NOPE + MLA_ROPE, rows] = (rot * qscale).astype(BF16)
            q_ref[0, base + MLA_NOPE + MLA_ROPE:base + LANES, rows] = jnp.zeros(
                (LANES - MLA_NOPE - MLA_ROPE, rot.shape[1]), BF16)


def _rope_perm_weights(w_rope):
    x1 = w_rope[..., 0::2]
    x2 = w_rope[..., 1::2]
    return jnp.concatenate([x1, x2], axis=-1), jnp.concatenate([-x2, x1], axis=-1)


def _mla_proj(x, sh, sc, w_down, q_norm, w_uq, kv_norm, w_ukv, tm):
    b, s, d = x.shape
    q_lora = q_norm.shape[0]
    kv_lora = kv_norm.shape[0]
    hd_q = MLA_NOPE + MLA_ROPE
    assert hd_q <= LANES and MLA_ROPE % 2 == 0 and q_lora % LANES == 0 and kv_lora == LANES

    r_a, r_b = _rope_perm_weights(w_down[:, q_lora + kv_lora:])
    pad = jnp.zeros((d, LANES - MLA_ROPE), F32)
    wd_ext = jnp.concatenate([w_down[:, :q_lora + kv_lora], r_a, pad, r_b, pad], axis=1).astype(BF16)

    wq = w_uq.reshape(q_lora, MLA_HEADS, hd_q)
    ra, rb = _rope_perm_weights(wq[..., MLA_NOPE:])
    wq_ext = jnp.concatenate([wq[..., :MLA_NOPE], ra, rb], axis=-1)
    wq_ext_t = wq_ext.reshape(q_lora, MLA_HEADS * LANES).T.astype(BF16)

    wkv = w_ukv.reshape(kv_lora, MLA_HEADS, MLA_NOPE + MLA_V)
    top = jnp.concatenate([wkv[..., :MLA_NOPE], jnp.zeros((kv_lora, MLA_HEADS, LANES - MLA_NOPE), F32)], axis=-1)
    copy = jnp.concatenate([jnp.zeros((LANES, MLA_NOPE), F32), jnp.eye(LANES, MLA_ROPE, dtype=F32),
                            jnp.zeros((LANES, LANES - MLA_NOPE - MLA_ROPE), F32)], axis=1)
    bot = jnp.broadcast_to(copy[:, None, :], (LANES, MLA_HEADS, LANES))
    wk_ext = jnp.concatenate([top, bot], axis=0).reshape(2 * LANES, MLA_HEADS * LANES).astype(BF16)
    wv_t = wkv[:, :, MLA_NOPE:].reshape(kv_lora, MLA_HEADS * MLA_V).T.astype(BF16)

    t = jnp.arange(s)
    row = (t // GRID_W).astype(F32)
    col = (t % GRID_W).astype(F32)
    nf = MLA_ROPE // 4
    inv = ROPE_THETA ** (-jnp.arange(nf, dtype=F32) / nf)
    ang = jnp.concatenate([row[:, None] * inv, col[:, None] * inv], axis=-1)
    cos2 = jnp.concatenate([jnp.cos(ang), jnp.cos(ang)], axis=1)
    sin2 = jnp.concatenate([jnp.sin(ang), jnp.sin(ang)], axis=1)
    lpad = jnp.zeros((s, LANES - MLA_ROPE), F32)
    cos_tok = jnp.concatenate([cos2, lpad], axis=1)
    sin_tok = jnp.concatenate([sin2, lpad], axis=1)
    cos_t = cos2.T
    sin_t = sin2.T

    qscale = float(hd_q ** -0.5 * LOG2E)
    nq = MLA_HEADS * LANES
    nv = MLA_HEADS * MLA_V
    nd = wd_ext.shape[1]
    full = lambda shape: pl.BlockSpec(shape, lambda bi, i: (0,) * len(shape))
    vec = pl.BlockSpec((1, 1, d), lambda bi, i: (bi, 0, 0))
    return pl.pallas_call(
        functools.partial(_mla_proj_kernel, q_lora=q_lora, kv_lora=kv_lora, qscale=qscale),
        grid=(b, s // tm),
        in_specs=[
            pl.BlockSpec((1, tm, d), lambda bi, i: (bi, i, 0)),
            vec, vec,
            full((d, nd)), full((1, q_lora)), full((1, kv_lora)),
            full((nq, q_lora)), full((2 * LANES, nq)), full((nv, kv_lora)),
            pl.BlockSpec((tm, LANES), lambda bi, i: (i, 0)),
            pl.BlockSpec((tm, LANES), lambda bi, i: (i, 0)),
            pl.BlockSpec((MLA_ROPE, tm), lambda bi, i: (0, i)),
            pl.BlockSpec((MLA_ROPE, tm), lambda bi, i: (0, i)),
        ],
        out_specs=[
            pl.BlockSpec((1, nq, tm), lambda bi, i: (bi, 0, i)),
            pl.BlockSpec((1, tm, nq), lambda bi, i: (bi, i, 0)),
            pl.BlockSpec((1, nv, tm), lambda bi, i: (bi, 0, i)),
        ],
        out_shape=[
            jax.ShapeDtypeStruct((b, nq, s), BF16),
            jax.ShapeDtypeStruct((b, s, nq), BF16),
            jax.ShapeDtypeStruct((b, nv, s), BF16),
        ],
        compiler_params=_params("arbitrary", "arbitrary"),
        name="mla_proj",
    )(x, sh, sc, wd_ext, q_norm.reshape(1, -1), kv_norm.reshape(1, -1), wq_ext_t, wk_ext, wv_t,
      cos_tok, sin_tok, cos_t, sin_t)


def _mla_attn_kernel(q_ref, k_ref, v_ref, o_ref, *s_refs, tq, tk, nkt, nq):
    ones = jnp.ones((SUM_ROWS, tk), BF16)

    def scores(qi, kt):
        qt = q_ref[0, :, pl.ds(pl.multiple_of(qi * tq, tq), tq)]
        st = jnp.dot(k_ref[0, kt * tk:(kt + 1) * tk, :], qt, preferred_element_type=F32)
        s_refs[kt % MLA_SCORE_SLOTS][...] = st
        return jnp.max(st, axis=0, keepdims=True)

    def softmax_pv(kt, tile_max, carry):
        m, acc = carry
        m_new = jnp.maximum(m, tile_max)
        p = jnp.exp2(s_refs[kt % MLA_SCORE_SLOTS][...] - m_new).astype(BF16)
        vb = jnp.concatenate([v_ref[0, :, kt * tk:(kt + 1) * tk], ones], axis=0)
        acc = jnp.exp2(m - m_new) * acc + jnp.dot(vb, p, preferred_element_type=F32)
        return m_new, acc

    in_flight = tuple(scores(0, kt) for kt in range(MLA_SCORE_AHEAD))

    def body(qi, in_flight):
        in_flight = list(in_flight)
        nxt = jnp.minimum(qi + 1, nq - 1)
        carry = (jnp.full((1, tq), NEG_BIG, F32), jnp.zeros((MLA_V + SUM_ROWS, tq), F32))
        for kt in range(nkt):
            ahead = kt + MLA_SCORE_AHEAD
            in_flight.append(scores(qi, ahead) if ahead < nkt else scores(nxt, ahead - nkt))
            carry = softmax_pv(kt, in_flight.pop(0), carry)
        m, acc = carry
        out = acc[:MLA_V] / acc[MLA_V:MLA_V + 1]
        o_ref[0, :, pl.ds(pl.multiple_of(qi * tq, tq), tq)] = out.astype(o_ref.dtype)
        return tuple(in_flight)

    lax.fori_loop(0, nq, body, in_flight)


def _mla_attention(q_t, k, v_t, tq, tk):
    b, _, s = q_t.shape
    nv = v_t.shape[1]
    nkt = s // tk
    assert nkt % MLA_SCORE_SLOTS == 0 and nkt >= MLA_SCORE_AHEAD
    return pl.pallas_call(
        functools.partial(_mla_attn_kernel, tq=tq, tk=tk, nkt=nkt, nq=s // tq),
        grid=(b, MLA_HEADS),
        in_specs=[
            pl.BlockSpec((1, LANES, s), lambda bi, hd: (bi, hd, 0)),
            pl.BlockSpec((1, s, LANES), lambda bi, hd: (bi, 0, hd)),
            pl.BlockSpec((1, MLA_V, s), lambda bi, hd: (bi, hd, 0)),
        ],
        out_specs=pl.BlockSpec((1, MLA_V, s), lambda bi, hd: (bi, hd, 0)),
        out_shape=jax.ShapeDtypeStruct((b, nv, s), BF16),
        scratch_shapes=[pltpu.VMEM((tk, tq), F32)] * MLA_SCORE_SLOTS,
        compiler_params=_params("arbitrary", "arbitrary"),
        name="mla_attn",
    )(q_t, k, v_t)


def _router_kernel(a_ref, wo_ref, x_ref, ga_ref, sh_ref, sc_ref, wr_ref, xo_ref, h_ref, i_ref, w_ref, *, n_exp):
    tm = x_ref.shape[1]
    chunks = [slice(c, c + ROUTER_CHUNK) for c in range(0, tm, ROUTER_CHUNK)]
    w = wr_ref[...]
    w_hi, w_lo = _split_bf16(w)
    lane = lax.broadcasted_iota(jnp.int32, (ROUTER_CHUNK, LANES), 1)
    valid = lane < n_exp

    hs = []
    for rows in chunks:
        y = lax.dot_general(a_ref[0, :, rows], wo_ref[...], (((0,), (0,)), ((), ())),
                            preferred_element_type=F32)
        x = x_ref[0, rows] + ga_ref[0] * y
        xo_ref[0, rows] = x
        h = _modulate(x, sh_ref[0], sc_ref[0])
        h_ref[0, rows] = h
        hs.append(h)

    all_logits = []
    for h in hs:
        h_hi, h_lo = _split_bf16(h)
        all_logits.append(jnp.dot(h_hi, w_hi, preferred_element_type=F32)
                          + (jnp.dot(h_lo, w_hi, preferred_element_type=F32)
                             + jnp.dot(h_hi, w_lo, preferred_element_type=F32)))

    rests = [jnp.where(valid, logits, -jnp.inf) for logits in all_logits]
    vals = [[] for _ in chunks]
    idxs = [[] for _ in chunks]
    for _ in range(TOP_K):
        for c in range(len(chunks)):
            v = jnp.max(rests[c], axis=-1, keepdims=True)
            idx = jnp.min(jnp.where(rests[c] == v, lane, LANES), axis=-1, keepdims=True)
            vals[c].append(v)
            idxs[c].append(idx)
            rests[c] = jnp.where(lane == idx, -jnp.inf, rests[c])
    for c, rows in enumerate(chunks):
        ev = [jnp.exp(v - vals[c][0]) for v in vals[c]]
        tot = ev[0]
        for v in ev[1:]:
            tot = tot + v
        i_out = jnp.zeros((ROUTER_CHUNK, LANES), jnp.int32)
        w_out = jnp.zeros((ROUTER_CHUNK, LANES), F32)
        for k in range(TOP_K):
            i_out = jnp.where(lane == k, idxs[c][k], i_out)
            w_out = jnp.where(lane == k, ev[k] / tot, w_out)
        i_ref[0, rows] = i_out
        w_ref[0, rows] = w_out


def _proj_router(o_t, w_o, x, g_a, sh, sc, w_router, tm):
    b, s, d = x.shape
    k = w_o.shape[0]
    n_exp = w_router.shape[1]
    wr = jnp.zeros((d, LANES), F32).at[:, :n_exp].set(w_router)
    vec = pl.BlockSpec((1, 1, d), lambda bi, i: (bi, 0, 0))
    tile = pl.BlockSpec((1, tm, d), lambda bi, i: (bi, i, 0))
    lanes_spec = pl.BlockSpec((1, tm, LANES), lambda bi, i: (bi, i, 0))
    return pl.pallas_call(
        functools.partial(_router_kernel, n_exp=n_exp),
        grid=(b, s // tm),
        in_specs=[
            pl.BlockSpec((1, k, tm), lambda bi, i: (bi, 0, i)),
            pl.BlockSpec((k, d), lambda bi, i: (0, 0)),
            tile,
            vec, vec, vec,
            pl.BlockSpec((d, LANES), lambda bi, i: (0, 0)),
        ],
        out_specs=[tile, tile, lanes_spec, lanes_spec],
        out_shape=[jax.ShapeDtypeStruct((b, s, d), F32),
                   jax.ShapeDtypeStruct((b, s, d), F32),
                   jax.ShapeDtypeStruct((b, s, LANES), jnp.int32),
                   jax.ShapeDtypeStruct((b, s, LANES), F32)],
        compiler_params=_params("arbitrary", "arbitrary"),
        name="proj_router",
    )(o_t, w_o, x, g_a, sh, sc, wr)


def _route_plan(idx, n_exp, tile):
    t, k = idx.shape
    e_flat = idx.reshape(t * k)
    onehot = (e_flat[:, None] == jnp.arange(n_exp, dtype=jnp.int32)[None, :]).astype(jnp.int32)
    csum = jnp.cumsum(onehot, axis=0)
    counts = csum[-1]
    rank = jnp.sum(onehot * csum, axis=1) - 1
    padded = ((counts + tile - 1) // tile) * tile
    gend = jnp.cumsum(padded)
    gstart = gend - padded
    dest = jnp.sum(onehot * gstart[None, :], axis=1) + rank
    p_rows = t * k + n_exp * tile
    tile_start = jnp.arange(p_rows // tile, dtype=jnp.int32) * tile
    tile_expert = jnp.minimum(jnp.sum((tile_start[:, None] >= gend[None, :]).astype(jnp.int32), axis=1),
                              n_exp - 1)
    tile_live = (tile_start < gend[-1]).astype(jnp.int32)
    return dest.astype(jnp.int32), tile_expert, tile_live, p_rows


def _row_scatter_kernel(dest_ref, h_ref, xs_in_ref, xs_ref, sem, *, tm):
    del xs_in_ref

    def row_copy(r, k):
        return pltpu.make_async_copy(h_ref.at[pl.ds(r, 1)], xs_ref.at[pl.ds(dest_ref[TOP_K * r + k], 1)], sem)

    def issue(r, carry):
        for k in range(TOP_K):
            row_copy(r, k).start(priority=k % 2)
        return carry

    lax.fori_loop(0, tm, issue, 0, unroll=4)
    for k in range(TOP_K):
        pltpu.make_async_copy(h_ref, xs_ref.at[pl.ds(0, tm)], sem).wait()


def _row_scatter(h, dest, p_rows, tm):
    n, d = h.shape
    return pl.pallas_call(
        functools.partial(_row_scatter_kernel, tm=tm),
        grid=(n // tm,),
        in_specs=[pl.BlockSpec((TOP_K * tm,), lambda i: (i,), memory_space=pltpu.SMEM),
                  pl.BlockSpec((tm, d), lambda i: (i, 0)),
                  pl.BlockSpec(memory_space=pl.ANY)],
        out_specs=pl.BlockSpec(memory_space=pl.ANY),
        out_shape=jax.ShapeDtypeStruct((p_rows, d), h.dtype),
        input_output_aliases={2: 0},
        scratch_shapes=[pltpu.SemaphoreType.DMA(())],
        compiler_params=_params("arbitrary"),
        name="moe_dispatch",
    )(dest, h, jnp.zeros((p_rows, d), h.dtype))


def _experts_kernel(te_ref, live_ref, x_ref, wgu_ref, wd_ref, o_ref, *, f):
    i = pl.program_id(0)

    @pl.when(live_ref[i] != 0)
    def _():
        o_ref[...] = _swiglu_tile(x_ref[...].astype(BF16), wgu_ref.at[0], wd_ref.at[0], f)

    @pl.when(live_ref[i] == 0)
    def _():
        o_ref[...] = jnp.zeros_like(o_ref)


def _experts(xs, tile_expert, tile_live, wgu, wd, tile):
    p_rows, d = xs.shape
    n_exp, f, _ = wd.shape
    grid_spec = pltpu.PrefetchScalarGridSpec(
        num_scalar_prefetch=2,
        grid=(p_rows // tile,),
        in_specs=[
            pl.BlockSpec((tile, d), lambda i, te, lv: (i, 0)),
            pl.BlockSpec((1, d, 2 * f), lambda i, te, lv: (te[i], 0, 0)),
            pl.BlockSpec((1, f, d), lambda i, te, lv: (te[i], 0, 0)),
        ],
        out_specs=pl.BlockSpec((tile, d), lambda i, te, lv: (i, 0)),
    )
    return pl.pallas_call(
        functools.partial(_experts_kernel, f=f),
        grid_spec=grid_spec,
        out_shape=jax.ShapeDtypeStruct((p_rows, d), F32),
        compiler_params=_params("arbitrary"),
        name="moe_experts",
    )(tile_expert, tile_live, xs, wgu, wd)


def _combine_kernel(dest_ref, x_ref, g_ref, fn_ref, w_ref, ys_ref, o_ref, buf_ref, sem, *, tm):
    def row_copy(r, k):
        return pltpu.make_async_copy(ys_ref.at[pl.ds(dest_ref[TOP_K * r + k], 1)],
                                     buf_ref.at[k, pl.ds(r, 1)], sem)

    def issue(r, carry):
        for k in range(TOP_K):
            row_copy(r, k).start(priority=k % 2)
        return carry

    lax.fori_loop(0, tm, issue, 0, unroll=4)
    for k in range(TOP_K):
        pltpu.make_async_copy(ys_ref.at[pl.ds(0, tm)], buf_ref.at[k], sem).wait()
    w = w_ref[...]
    y = w[:, 0:1] * buf_ref[0]
    for k in range(1, TOP_K):
        y = y + w[:, k:k + 1] * buf_ref[k]
    o_ref[...] = _rms(x_ref[...] + g_ref[0] * y) * fn_ref[...]


def _combine(x, g, final_norm, wts, ys, dest, tm):
    b, s, d = x.shape
    tiles_per_batch = s // tm
    return pl.pallas_call(
        functools.partial(_combine_kernel, tm=tm),
        grid=(b * tiles_per_batch,),
        in_specs=[
            pl.BlockSpec((TOP_K * tm,), lambda i: (i,), memory_space=pltpu.SMEM),
            pl.BlockSpec((tm, d), lambda i: (i, 0)),
            pl.BlockSpec((1, 1, d), lambda i: (i // tiles_per_batch, 0, 0)),
            pl.BlockSpec((1, d), lambda i: (0, 0)),
            pl.BlockSpec((tm, LANES), lambda i: (i, 0)),
            pl.BlockSpec(memory_space=pl.ANY),
        ],
        out_specs=pl.BlockSpec((tm, d), lambda i: (i, 0)),
        out_shape=jax.ShapeDtypeStruct((b * s, d), F32),
        scratch_shapes=[pltpu.VMEM((TOP_K, tm, d), F32), pltpu.SemaphoreType.DMA(())],
        compiler_params=_params("arbitrary"),
        name="moe_combine",
    )(dest, x.reshape(b * s, d), g, final_norm.reshape(1, d), wts.reshape(b * s, LANES), ys).reshape(b, s, d)


def kernel(x, c, w_ada, b_ada, na_w_qkv, na_rpb, na_w_o, ffn_w_gu, ffn_w_down, mla_w_down, mla_q_norm,
           mla_w_uq, mla_kv_norm, mla_w_ukv, mla_w_o, moe_w_router, moe_w_gu, moe_w_down, final_norm):
    b, s, d = x.shape
    assert w_ada.shape[0] == 2, "one neighbourhood-attention layer followed by one latent-attention layer"
    t = _tiles(s)
    mods = _adaln(c, w_ada, b_ada)

    def split(layer):
        return [v.reshape(b, 1, d) for v in jnp.split(mods[layer], 6, axis=-1)]

    sh_a, sc_a, g_a, sh_f, sc_f, g_f = split(0)
    qkv = _ln_matmul(x, sh_a, sc_a, na_w_qkv[0].astype(BF16), t.wide)
    att = _neighborhood_attention(qkv, na_rpb[0])
    x = _proj_ffn(att, na_w_o[0].astype(BF16), x, g_a, sh_f, sc_f, g_f,
                  ffn_w_gu[0].astype(BF16), ffn_w_down[0].astype(BF16), t.token)

    sh_a, sc_a, g_a, sh_f, sc_f, g_f = split(1)
    q_t, k, v_t = _mla_proj(x, sh_a, sc_a, mla_w_down[0], mla_q_norm[0], mla_w_uq[0], mla_kv_norm[0],
                            mla_w_ukv[0], t.token)
    o_t = _mla_attention(q_t, k, v_t, t.mla_q, t.mla_k)
    x, h, idx, wts = _proj_router(o_t, mla_w_o[0].astype(BF16), x, g_a, sh_f, sc_f, moe_w_router[0], t.wide)
    n_exp = moe_w_router.shape[-1]
    dest, tile_expert, tile_live, p_rows = _route_plan(idx[..., :TOP_K].reshape(b * s, TOP_K), n_exp, t.token)
    xs = _row_scatter(h.reshape(b * s, d), dest, p_rows, t.wide)
    ys = _experts(xs, tile_expert, tile_live, moe_w_gu[0].astype(BF16), moe_w_down[0].astype(BF16), t.token)
    return _combine(x, g_f, final_norm, wts, ys, dest, t.wide)
```

```python
import functools
from typing import NamedTuple

import jax
import jax.numpy as jnp
from jax import lax
from jax.experimental import pallas as pl
from jax.experimental.pallas import tpu as pltpu

F32 = jnp.float32
BF16 = jnp.bfloat16

GRID_W = 64
NA_HEADS = 16
NA_WIN_H = 8
NA_WIN_W = 16
MLA_HEADS = 16
MLA_NOPE = 64
MLA_ROPE = 32
MLA_V = 64
ROPE_THETA = 10000.0
TOP_K = 2
NORM_EPS = 1e-6

LANES = 128
SUBLANES = 8
MXU_DIM = 256
SUM_ROWS = 2 * SUBLANES
VMEM_LIMIT = 56 * 1024 * 1024

NA_ROWS_PER_STEP = 32
NA_LOOKAHEAD = 3
ROUTER_CHUNK = 256
MLA_PROJ_CHUNK = 256
MLA_SCORE_AHEAD = 2
MLA_SCORE_SLOTS = 4
ADALN_COLS = 1536
NEG_BIG = -1e30
LOG2E = 1.4426950408889634


def _params(*sem):
    return pltpu.CompilerParams(dimension_semantics=sem, vmem_limit_bytes=VMEM_LIMIT)


class _Tiles(NamedTuple):
    token: int
    wide: int
    mla_q: int
    mla_k: int


def _tiles(s):
    return _Tiles(token=min(512, s), wide=min(1024, s), mla_q=min(512, s), mla_k=256)


def _rms(x):
    return x * lax.rsqrt(jnp.mean(x * x, axis=-1, keepdims=True) + NORM_EPS)


def _modulate(x, sh, sc):
    return _rms(x) * (1.0 + sc) + sh


def _split_bf16(a):
    hi = a.astype(BF16)
    return hi, (a - hi.astype(F32)).astype(BF16)


def _dot_3pass(a, b):
    a_hi, a_lo = _split_bf16(a)
    b_hi, b_lo = _split_bf16(b)
    return (jnp.dot(a_hi, b_hi, preferred_element_type=F32)
            + (jnp.dot(a_lo, b_hi, preferred_element_type=F32)
               + jnp.dot(a_hi, b_lo, preferred_element_type=F32)))


def _adaln_kernel(c_ref, w_ref, b_ref, o_ref):
    c = c_ref[...]
    ca = c * jax.nn.sigmoid(c)
    o_ref[0] = _dot_3pass(ca, w_ref[0]) + b_ref[0]


def _adaln(c, w_ada, b_ada):
    depth, d, n = w_ada.shape
    b = c.shape[0]
    rows = SUBLANES
    assert b <= rows
    cp = jnp.zeros((rows, d), F32).at[:b].set(c)
    tn = ADALN_COLS
    out = pl.pallas_call(
        _adaln_kernel,
        grid=(depth, n // tn),
        in_specs=[
            pl.BlockSpec((rows, d), lambda l, j: (0, 0)),
            pl.BlockSpec((1, d, tn), lambda l, j: (l, 0, j)),
            pl.BlockSpec((1, 1, tn), lambda l, j: (l, 0, j)),
        ],
        out_specs=pl.BlockSpec((1, rows, tn), lambda l, j: (l, 0, j)),
        out_shape=jax.ShapeDtypeStruct((depth, rows, n), F32),
        compiler_params=_params("arbitrary", "arbitrary"),
        name="adaln",
    )(cp, w_ada, b_ada.reshape(depth, 1, n))
    return out[:, :b]


def _ln_matmul_kernel(x_ref, sh_ref, sc_ref, w_ref, o_ref):
    h = _modulate(x_ref[0], sh_ref[0], sc_ref[0]).astype(BF16)
    o_ref[0] = jnp.dot(h, w_ref[...], preferred_element_type=F32).astype(o_ref.dtype)


def _ln_matmul(x, sh, sc, w, tm):
    b, s, d = x.shape
    n = w.shape[1]
    vec = pl.BlockSpec((1, 1, d), lambda bi, i: (bi, 0, 0))
    return pl.pallas_call(
        _ln_matmul_kernel,
        grid=(b, s // tm),
        in_specs=[
            pl.BlockSpec((1, tm, d), lambda bi, i: (bi, i, 0)),
            vec, vec,
            pl.BlockSpec((d, n), lambda bi, i: (0, 0)),
        ],
        out_specs=pl.BlockSpec((1, tm, n), lambda bi, i: (bi, i, 0)),
        out_shape=jax.ShapeDtypeStruct((b, s, n), BF16),
        compiler_params=_params("arbitrary", "arbitrary"),
        name="ln_matmul",
    )(x, sh, sc, w)


def _na_kernel(q_ref, k_ref, v_ref, t_ref, o_ref, *, rows, rows_per_step):
    win = NA_WIN_H * GRID_W
    lane = lax.broadcasted_iota(jnp.int32, (GRID_W, LANES), 1)
    first = lane < (LANES // 2)
    scale = (LANES // 2) ** -0.5 * LOG2E

    def bias(pat):
        d0 = NA_WIN_H - 1 - pat
        return jnp.concatenate(
            [jnp.concatenate([t_ref[0, d0 + 2 * j], t_ref[1, d0 + 2 * j]], axis=0)
             for j in range(NA_WIN_H // 2)], axis=1)

    def body(i, carry):
        rr = [i * rows_per_step + j for j in range(rows_per_step)]
        rss = [jnp.clip(r - NA_WIN_H // 2, 0, rows - NA_WIN_H) for r in rr]

        def scores(j):
            r, rs = rr[j], rss[j]
            q = q_ref[0, pl.ds(pl.multiple_of(r * GRID_W, GRID_W), GRID_W), :]
            zero = jnp.zeros_like(q)
            q2 = jnp.concatenate([jnp.where(first, q, zero), jnp.where(first, zero, q)], axis=0)
            kw = k_ref[0, pl.ds(pl.multiple_of(rs * GRID_W, GRID_W), win), :]
            s = lax.dot_general(q2, kw, (((1,), (1,)), ((), ())), preferred_element_type=F32)
            return s * scale + bias(r - rs)

        def softmax_pv(j, s):
            r, rs = rr[j], rss[j]
            p = jnp.exp2(s - jnp.max(s, axis=-1, keepdims=True))
            l = jnp.sum(p, axis=-1, keepdims=True)
            vw = v_ref[0, pl.ds(pl.multiple_of(rs * GRID_W, GRID_W), win), :]
            o = jnp.dot(p.astype(BF16), vw, preferred_element_type=F32) / l
            o = jnp.where(first, o[:GRID_W], o[GRID_W:])
            o_ref[0, pl.ds(pl.multiple_of(r * GRID_W, GRID_W), GRID_W), :] = o.astype(o_ref.dtype)

        in_flight = [scores(j) for j in range(min(NA_LOOKAHEAD, rows_per_step))]
        for j in range(rows_per_step):
            if j + NA_LOOKAHEAD < rows_per_step:
                in_flight.append(scores(j + NA_LOOKAHEAD))
            softmax_pv(j, in_flight.pop(0))
        return carry

    lax.fori_loop(0, rows // rows_per_step, body, 0)


def _na_bias_table(rpb):
    h, ndr, ndc = rpb.shape
    qc = jnp.arange(GRID_W)
    kc = jnp.arange(GRID_W)
    cs = jnp.clip(qc - NA_WIN_W // 2, 0, GRID_W - NA_WIN_W)
    inwin = (kc[None, :] >= cs[:, None]) & (kc[None, :] < cs[:, None] + NA_WIN_W)
    period = 2 * GRID_W
    u = jnp.concatenate([rpb[:, :, NA_WIN_W - 1:], jnp.zeros((h, ndr, period - ndc), F32),
                         rpb[:, :, :NA_WIN_W - 1]], axis=-1)
    flat = jnp.tile(u, (1, 1, GRID_W))[:, :, :GRID_W * (period - 1)]
    col = flat.reshape(h, ndr, GRID_W, period - 1)[:, :, :, :GRID_W]
    col = jnp.where(inwin[None, None], col * LOG2E, NEG_BIG)
    return jnp.concatenate([col[:, :-1], col[:, 1:]], axis=-1)


def _neighborhood_attention(qkv, rpb):
    b, s, d3 = qkv.shape
    d = d3 // 3
    rows = s // GRID_W
    rows_per_step = min(NA_ROWS_PER_STEP, rows)
    assert rows >= NA_WIN_H and d // NA_HEADS == LANES // 2 and rows % rows_per_step == 0
    nslab = d // LANES
    table = _na_bias_table(rpb.astype(F32))
    win = NA_WIN_H * GRID_W
    return pl.pallas_call(
        functools.partial(_na_kernel, rows=rows, rows_per_step=rows_per_step),
        grid=(b, nslab),
        in_specs=[
            pl.BlockSpec((1, s, LANES), lambda bi, hp: (bi, 0, hp)),
            pl.BlockSpec((1, s, LANES), lambda bi, hp: (bi, 0, nslab + hp)),
            pl.BlockSpec((1, s, LANES), lambda bi, hp: (bi, 0, 2 * nslab + hp)),
            pl.BlockSpec((2,) + table.shape[1:], lambda bi, hp: (hp, 0, 0, 0)),
        ],
        out_specs=pl.BlockSpec((1, s, LANES), lambda bi, hp: (bi, 0, hp)),
        out_shape=jax.ShapeDtypeStruct((b, s, d), BF16),
        compiler_params=_params("arbitrary", "arbitrary"),
        name="na_attn",
    )(qkv, qkv, qkv, table)


def _swiglu_tile(h, wgu, wd, f):
    acc = None
    for c in range(f // MXU_DIM):
        lo = c * MXU_DIM
        g = jnp.dot(h, wgu[:, lo:lo + MXU_DIM], preferred_element_type=F32)
        u = jnp.dot(h, wgu[:, f + lo:f + lo + MXU_DIM], preferred_element_type=F32)
        a = (g * jax.nn.sigmoid(g) * u).astype(BF16)
        y = jnp.dot(a, wd[lo:lo + MXU_DIM, :], preferred_element_type=F32)
        acc = y if acc is None else acc + y
    return acc


def _ffn_kernel(a_ref, wo_ref, x_ref, ga_ref, sh_ref, sc_ref, g_ref, wgu_ref, wd_ref, o_ref, *, f):
    x = x_ref[0] + ga_ref[0] * jnp.dot(a_ref[0], wo_ref[...], preferred_element_type=F32)
    h = _modulate(x, sh_ref[0], sc_ref[0]).astype(BF16)
    o_ref[0] = x + g_ref[0] * _swiglu_tile(h, wgu_ref, wd_ref, f)


def _proj_ffn(att, w_o, x, g_a, sh, sc, g, wgu, wd, tm):
    b, s, d = x.shape
    f = wd.shape[0]
    vec = pl.BlockSpec((1, 1, d), lambda bi, i: (bi, 0, 0))
    tile = pl.BlockSpec((1, tm, d), lambda bi, i: (bi, i, 0))
    return pl.pallas_call(
        functools.partial(_ffn_kernel, f=f),
        grid=(b, s // tm),
        in_specs=[
            tile,
            pl.BlockSpec((d, d), lambda bi, i: (0, 0)),
            tile,
            vec, vec, vec, vec,
            pl.BlockSpec((d, 2 * f), lambda bi, i: (0, 0)),
            pl.BlockSpec((f, d), lambda bi, i: (0, 0)),
        ],
        out_specs=tile,
        out_shape=jax.ShapeDtypeStruct((b, s, d), F32),
        compiler_params=_params("arbitrary", "arbitrary"),
        name="proj_ffn",
    )(att, w_o, x, g_a, sh, sc, g, wgu, wd)


def _mla_proj_kernel(x_ref, sh_ref, sc_ref, wd_ref, qn_ref, kvn_ref, wq_ref, wk_ref, wv_ref,
                     cs_ref, sn_ref, cst_ref, snt_ref, q_ref, k_ref, v_ref, *, q_lora, kv_lora, qscale):
    tm = x_ref.shape[1]
    chunks = [slice(c, c + MLA_PROJ_CHUNK) for c in range(0, tm, MLA_PROJ_CHUNK)]
    r0 = q_lora + kv_lora
    nt = (((1,), (1,)), ((), ()))
    downs = []
    for rows in chunks:
        h = _modulate(x_ref[0, rows], sh_ref[0], sc_ref[0]).astype(BF16)
        downs.append(jnp.dot(h, wd_ref[...], preferred_element_type=F32))
    lat = []
    for rows, down in zip(chunks, downs):
        cq = (_rms(down[:, :q_lora]) * qn_ref[...]).astype(BF16)
        ckv = (_rms(down[:, q_lora:r0]) * kvn_ref[...]).astype(BF16)
        kr = down[:, r0:r0 + LANES] * cs_ref[rows] + down[:, r0 + LANES:r0 + 2 * LANES] * sn_ref[rows]
        lat.append((cq, ckv, jnp.concatenate([ckv, kr.astype(BF16)], axis=1)))
    qts = []
    for rows, (cq, ckv, z) in zip(chunks, lat):
        k_ref[0, rows] = jnp.dot(z, wk_ref[...], preferred_element_type=F32).astype(BF16)
        v_ref[0, :, rows] = lax.dot_general(wv_ref[...], ckv, nt, preferred_element_type=F32).astype(BF16)
        qts.append(lax.dot_general(wq_ref[...], cq, nt, preferred_element_type=F32))
    for rows, qt in zip(chunks, qts):
        cst = cst_ref[:, rows]
        snt = snt_ref[:, rows]
        for hd in range(MLA_HEADS):
            base = hd * LANES
            blk = qt[base:base + LANES]
            rot = blk[MLA_NOPE:MLA_NOPE + MLA_ROPE] * cst + blk[MLA_NOPE + MLA_ROPE:] * snt
            q_ref[0, base:base + MLA_NOPE, rows] = (blk[:MLA_NOPE] * qscale).astype(BF16)
            q_ref[0, base + MLA_NOPE:base + MLA_NOPE + MLA_ROPE, rows] = (rot * qscale).astype(BF16)
            q_ref[0, base + MLA_NOPE + MLA_ROPE:base + LANES, rows] = jnp.zeros(
                (LANES - MLA_NOPE - MLA_ROPE, rot.shape[1]), BF16)


def _rope_perm_weights(w_rope):
    x1 = w_rope[..., 0::2]
    x2 = w_rope[..., 1::2]
    return jnp.concatenate([x1, x2], axis=-1), jnp.concatenate([-x2, x1], axis=-1)


def _mla_proj(x, sh, sc, w_down, q_norm, w_uq, kv_norm, w_ukv, tm):
    b, s, d = x.shape
    q_lora = q_norm.shape[0]
    kv_lora = kv_norm.shape[0]
    hd_q = MLA_NOPE + MLA_ROPE
    assert hd_q <= LANES and MLA_ROPE % 2 == 0 and q_lora % LANES == 0 and kv_lora == LANES

    r_a, r_b = _rope_perm_weights(w_down[:, q_lora + kv_lora:])
    pad = jnp.zeros((d, LANES - MLA_ROPE), F32)
    wd_ext = jnp.concatenate([w_down[:, :q_lora + kv_lora], r_a, pad, r_b, pad], axis=1).astype(BF16)

    wq = w_uq.reshape(q_lora, MLA_HEADS, hd_q)
    ra, rb = _rope_perm_weights(wq[..., MLA_NOPE:])
    wq_ext = jnp.concatenate([wq[..., :MLA_NOPE], ra, rb], axis=-1)
    wq_ext_t = wq_ext.reshape(q_lora, MLA_HEADS * LANES).T.astype(BF16)

    wkv = w_ukv.reshape(kv_lora, MLA_HEADS, MLA_NOPE + MLA_V)
    top = jnp.concatenate([wkv[..., :MLA_NOPE], jnp.zeros((kv_lora, MLA_HEADS, LANES - MLA_NOPE), F32)], axis=-1)
    copy = jnp.concatenate([jnp.zeros((LANES, MLA_NOPE), F32), jnp.eye(LANES, MLA_ROPE, dtype=F32),
                            jnp.zeros((LANES, LANES - MLA_NOPE - MLA_ROPE), F32)], axis=1)
    bot = jnp.broadcast_to(copy[:, None, :], (LANES, MLA_HEADS, LANES))
    wk_ext = jnp.concatenate([top, bot], axis=0).reshape(2 * LANES, MLA_HEADS * LANES).astype(BF16)
    wv_t = wkv[:, :, MLA_NOPE:].reshape(kv_lora, MLA_HEADS * MLA_V).T.astype(BF16)

    t = jnp.arange(s)
    row = (t // GRID_W).astype(F32)
    col = (t % GRID_W).astype(F32)
    nf = MLA_ROPE // 4
    inv = ROPE_THETA ** (-jnp.arange(nf, dtype=F32) / nf)
    ang = jnp.concatenate([row[:, None] * inv, col[:, None] * inv], axis=-1)
    cos2 = jnp.concatenate([jnp.cos(ang), jnp.cos(ang)], axis=1)
    sin2 = jnp.concatenate([jnp.sin(ang), jnp.sin(ang)], axis=1)
    lpad = jnp.zeros((s, LANES - MLA_ROPE), F32)
    cos_tok = jnp.concatenate([cos2, lpad], axis=1)
    sin_tok = jnp.concatenate([sin2, lpad], axis=1)
    cos_t = cos2.T
    sin_t = sin2.T

    qscale = float(hd_q ** -0.5 * LOG2E)
    nq = MLA_HEADS * LANES
    nv = MLA_HEADS * MLA_V
    nd = wd_ext.shape[1]
    full = lambda shape: pl.BlockSpec(shape, lambda bi, i: (0,) * len(shape))
    vec = pl.BlockSpec((1, 1, d), lambda bi, i: (bi, 0, 0))
    return pl.pallas_call(
        functools.partial(_mla_proj_kernel, q_lora=q_lora, kv_lora=kv_lora, qscale=qscale),
        grid=(b, s // tm),
        in_specs=[
            pl.BlockSpec((1, tm, d), lambda bi, i: (bi, i, 0)),
            vec, vec,
            full((d, nd)), full((1, q_lora)), full((1, kv_lora)),
            full((nq, q_lora)), full((2 * LANES, nq)), full((nv, kv_lora)),
            pl.BlockSpec((tm, LANES), lambda bi, i: (i, 0)),
            pl.BlockSpec((tm, LANES), lambda bi, i: (i, 0)),
            pl.BlockSpec((MLA_ROPE, tm), lambda bi, i: (0, i)),
            pl.BlockSpec((MLA_ROPE, tm), lambda bi, i: (0, i)),
        ],
        out_specs=[
            pl.BlockSpec((1, nq, tm), lambda bi, i: (bi, 0, i)),
            pl.BlockSpec((1, tm, nq), lambda bi, i: (bi, i, 0)),
            pl.BlockSpec((1, nv, tm), lambda bi, i: (bi, 0, i)),
        ],
        out_shape=[
            jax.ShapeDtypeStruct((b, nq, s), BF16),
            jax.ShapeDtypeStruct((b, s, nq), BF16),
            jax.ShapeDtypeStruct((b, nv, s), BF16),
        ],
        compiler_params=_params("arbitrary", "arbitrary"),
        name="mla_proj",
    )(x, sh, sc, wd_ext, q_norm.reshape(1, -1), kv_norm.reshape(1, -1), wq_ext_t, wk_ext, wv_t,
      cos_tok, sin_tok, cos_t, sin_t)


def _mla_attn_kernel(q_ref, k_ref, v_ref, o_ref, *s_refs, tq, tk, nkt, nq):
    ones = jnp.ones((SUM_ROWS, tk), BF16)

    def scores(qi, kt):
        qt = q_ref[0, :, pl.ds(pl.multiple_of(qi * tq, tq), tq)]
        st = jnp.dot(k_ref[0, kt * tk:(kt + 1) * tk, :], qt, preferred_element_type=F32)
        s_refs[kt % MLA_SCORE_SLOTS][...] = st
        return jnp.max(st, axis=0, keepdims=True)

    def softmax_pv(kt, tile_max, carry):
        m, acc = carry
        m_new = jnp.maximum(m, tile_max)
        p = jnp.exp2(s_refs[kt % MLA_SCORE_SLOTS][...] - m_new).astype(BF16)
        vb = jnp.concatenate([v_ref[0, :, kt * tk:(kt + 1) * tk], ones], axis=0)
        acc = jnp.exp2(m - m_new) * acc + jnp.dot(vb, p, preferred_element_type=F32)
        return m_new, acc

    in_flight = tuple(scores(0, kt) for kt in range(MLA_SCORE_AHEAD))

    def body(qi, in_flight):
        in_flight = list(in_flight)
        nxt = jnp.minimum(qi + 1, nq - 1)
        carry = (jnp.full((1, tq), NEG_BIG, F32), jnp.zeros((MLA_V + SUM_ROWS, tq), F32))
        for kt in range(nkt):
            ahead = kt + MLA_SCORE_AHEAD
            in_flight.append(scores(qi, ahead) if ahead < nkt else scores(nxt, ahead - nkt))
            carry = softmax_pv(kt, in_flight.pop(0), carry)
        m, acc = carry
        out = acc[:MLA_V] / acc[MLA_V:MLA_V + 1]
        o_ref[0, :, pl.ds(pl.multiple_of(qi * tq, tq), tq)] = out.astype(o_ref.dtype)
        return tuple(in_flight)

    lax.fori_loop(0, nq, body, in_flight)


def _mla_attention(q_t, k, v_t, tq, tk):
    b, _, s = q_t.shape
    nv = v_t.shape[1]
    nkt = s // tk
    assert nkt % MLA_SCORE_SLOTS == 0 and nkt >= MLA_SCORE_AHEAD
    return pl.pallas_call(
        functools.partial(_mla_attn_kernel, tq=tq, tk=tk, nkt=nkt, nq=s // tq),
        grid=(b, MLA_HEADS),
        in_specs=[
            pl.BlockSpec((1, LANES, s), lambda bi, hd: (bi, hd, 0)),
            pl.BlockSpec((1, s, LANES), lambda bi, hd: (bi, 0, hd)),
            pl.BlockSpec((1, MLA_V, s), lambda bi, hd: (bi, hd, 0)),
        ],
        out_specs=pl.BlockSpec((1, MLA_V, s), lambda bi, hd: (bi, hd, 0)),
        out_shape=jax.ShapeDtypeStruct((b, nv, s), BF16),
        scratch_shapes=[pltpu.VMEM((tk, tq), F32)] * MLA_SCORE_SLOTS,
        compiler_params=_params("arbitrary", "arbitrary"),
        name="mla_attn",
    )(q_t, k, v_t)


def _router_kernel(a_ref, wo_ref, x_ref, ga_ref, sh_ref, sc_ref, wr_ref, xo_ref, h_ref, i_ref, w_ref, *, n_exp):
    tm = x_ref.shape[1]
    chunks = [slice(c, c + ROUTER_CHUNK) for c in range(0, tm, ROUTER_CHUNK)]
    w = wr_ref[...]
    w_hi, w_lo = _split_bf16(w)
    lane = lax.broadcasted_iota(jnp.int32, (ROUTER_CHUNK, LANES), 1)
    valid = lane < n_exp

    hs = []
    for rows in chunks:
        y = lax.dot_general(a_ref[0, :, rows], wo_ref[...], (((0,), (0,)), ((), ())),
                            preferred_element_type=F32)
        x = x_ref[0, rows] + ga_ref[0] * y
        xo_ref[0, rows] = x
        h = _modulate(x, sh_ref[0], sc_ref[0])
        h_ref[0, rows] = h
        hs.append(h)

    all_logits = []
    for h in hs:
        h_hi, h_lo = _split_bf16(h)
        all_logits.append(jnp.dot(h_hi, w_hi, preferred_element_type=F32)
                          + (jnp.dot(h_lo, w_hi, preferred_element_type=F32)
                             + jnp.dot(h_hi, w_lo, preferred_element_type=F32)))

    rests = [jnp.where(valid, logits, -jnp.inf) for logits in all_logits]
    vals = [[] for _ in chunks]
    idxs = [[] for _ in chunks]
    for _ in range(TOP_K):
        for c in range(len(chunks)):
            v = jnp.max(rests[c], axis=-1, keepdims=True)
            idx = jnp.min(jnp.where(rests[c] == v, lane, LANES), axis=-1, keepdims=True)
            vals[c].append(v)
            idxs[c].append(idx)
            rests[c] = jnp.where(lane == idx, -jnp.inf, rests[c])
    for c, rows in enumerate(chunks):
        ev = [jnp.exp(v - vals[c][0]) for v in vals[c]]
        tot = ev[0]
        for v in ev[1:]:
            tot = tot + v
        i_out = jnp.zeros((ROUTER_CHUNK, LANES), jnp.int32)
        w_out = jnp.zeros((ROUTER_CHUNK, LANES), F32)
        for k in range(TOP_K):
            i_out = jnp.where(lane == k, idxs[c][k], i_out)
            w_out = jnp.where(lane == k, ev[k] / tot, w_out)
        i_ref[0, rows] = i_out
        w_ref[0, rows] = w_out


def _proj_router(o_t, w_o, x, g_a, sh, sc, w_router, tm):
    b, s, d = x.shape
    k = w_o.shape[0]
    n_exp = w_router.shape[1]
    wr = jnp.zeros((d, LANES), F32).at[:, :n_exp].set(w_router)
    vec = pl.BlockSpec((1, 1, d), lambda bi, i: (bi, 0, 0))
    tile = pl.BlockSpec((1, tm, d), lambda bi, i: (bi, i, 0))
    lanes_spec = pl.BlockSpec((1, tm, LANES), lambda bi, i: (bi, i, 0))
    return pl.pallas_call(
        functools.partial(_router_kernel, n_exp=n_exp),
        grid=(b, s // tm),
        in_specs=[
            pl.BlockSpec((1, k, tm), lambda bi, i: (bi, 0, i)),
            pl.BlockSpec((k, d), lambda bi, i: (0, 0)),
            tile,
            vec, vec, vec,
            pl.BlockSpec((d, LANES), lambda bi, i: (0, 0)),
        ],
        out_specs=[tile, tile, lanes_spec, lanes_spec],
        out_shape=[jax.ShapeDtypeStruct((b, s, d), F32),
                   jax.ShapeDtypeStruct((b, s, d), F32),
                   jax.ShapeDtypeStruct((b, s, LANES), jnp.int32),
                   jax.ShapeDtypeStruct((b, s, LANES), F32)],
        compiler_params=_params("arbitrary", "arbitrary"),
        name="proj_router",
    )(o_t, w_o, x, g_a, sh, sc, wr)


def _route_plan(idx, n_exp, tile):
    t, k = idx.shape
    e_flat = idx.reshape(t * k)
    onehot = (e_flat[:, None] == jnp.arange(n_exp, dtype=jnp.int32)[None, :]).astype(jnp.int32)
    csum = jnp.cumsum(onehot, axis=0)
    counts = csum[-1]
    rank = jnp.sum(onehot * csum, axis=1) - 1
    padded = ((counts + tile - 1) // tile) * tile
    gend = jnp.cumsum(padded)
    gstart = gend - padded
    dest = jnp.sum(onehot * gstart[None, :], axis=1) + rank
    p_rows = t * k + n_exp * tile
    tile_start = jnp.arange(p_rows // tile, dtype=jnp.int32) * tile
    tile_expert = jnp.minimum(jnp.sum((tile_start[:, None] >= gend[None, :]).astype(jnp.int32), axis=1),
                              n_exp - 1)
    tile_live = (tile_start < gend[-1]).astype(jnp.int32)
    return dest.astype(jnp.int32), tile_expert, tile_live, p_rows


def _row_scatter_kernel(dest_ref, h_ref, xs_in_ref, xs_ref, sem, *, tm):
    del xs_in_ref

    def row_copy(r, k):
        return pltpu.make_async_copy(h_ref.at[pl.ds(r, 1)], xs_ref.at[pl.ds(dest_ref[TOP_K * r + k], 1)], sem)

    def issue(r, carry):
        for k in range(TOP_K):
            row_copy(r, k).start(priority=k % 2)
        return carry

    lax.fori_loop(0, tm, issue, 0, unroll=4)
    for k in range(TOP_K):
        pltpu.make_async_copy(h_ref, xs_ref.at[pl.ds(0, tm)], sem).wait()


def _row_scatter(h, dest, p_rows, tm):
    n, d = h.shape
    return pl.pallas_call(
        functools.partial(_row_scatter_kernel, tm=tm),
        grid=(n // tm,),
        in_specs=[pl.BlockSpec((TOP_K * tm,), lambda i: (i,), memory_space=pltpu.SMEM),
                  pl.BlockSpec((tm, d), lambda i: (i, 0)),
                  pl.BlockSpec(memory_space=pl.ANY)],
        out_specs=pl.BlockSpec(memory_space=pl.ANY),
        out_shape=jax.ShapeDtypeStruct((p_rows, d), h.dtype),
        input_output_aliases={2: 0},
        scratch_shapes=[pltpu.SemaphoreType.DMA(())],
        compiler_params=_params("arbitrary"),
        name="moe_dispatch",
    )(dest, h, jnp.zeros((p_rows, d), h.dtype))


def _experts_kernel(te_ref, live_ref, x_ref, wgu_ref, wd_ref, o_ref, *, f):
    i = pl.program_id(0)

    @pl.when(live_ref[i] != 0)
    def _():
        o_ref[...] = _swiglu_tile(x_ref[...].astype(BF16), wgu_ref.at[0], wd_ref.at[0], f)

    @pl.when(live_ref[i] == 0)
    def _():
        o_ref[...] = jnp.zeros_like(o_ref)


def _experts(xs, tile_expert, tile_live, wgu, wd, tile):
    p_rows, d = xs.shape
    n_exp, f, _ = wd.shape
    grid_spec = pltpu.PrefetchScalarGridSpec(
        num_scalar_prefetch=2,
        grid=(p_rows // tile,),
        in_specs=[
            pl.BlockSpec((tile, d), lambda i, te, lv: (i, 0)),
            pl.BlockSpec((1, d, 2 * f), lambda i, te, lv: (te[i], 0, 0)),
            pl.BlockSpec((1, f, d), lambda i, te, lv: (te[i], 0, 0)),
        ],
        out_specs=pl.BlockSpec((tile, d), lambda i, te, lv: (i, 0)),
    )
    return pl.pallas_call(
        functools.partial(_experts_kernel, f=f),
        grid_spec=grid_spec,
        out_shape=jax.ShapeDtypeStruct((p_rows, d), F32),
        compiler_params=_params("arbitrary"),
        name="moe_experts",
    )(tile_expert, tile_live, xs, wgu, wd)


def _combine_kernel(dest_ref, next_ref, x_ref, g_ref, fn_ref, w_ref, ys_ref, o_ref, buf_ref, sems, *, tm):
    i = pl.program_id(0)
    slot = lax.rem(i, 2)

    def gather(idx_ref, to):
        def issue(r, carry):
            for k in range(TOP_K):
                pltpu.make_async_copy(ys_ref.at[pl.ds(idx_ref[TOP_K * r + k], 1)],
                                      buf_ref.at[to, k, pl.ds(r, 1)], sems.at[to]).start(priority=k % 2)
            return carry

        lax.fori_loop(0, tm, issue, 0, unroll=4)

    @pl.when(i == 0)
    def _():
        gather(dest_ref, slot)

    @pl.when(i + 1 < pl.num_programs(0))
    def _():
        gather(next_ref, 1 - slot)

    for k in range(TOP_K):
        pltpu.make_async_copy(ys_ref.at[pl.ds(0, tm)], buf_ref.at[slot, k], sems.at[slot]).wait()
    w = w_ref[...]
    y = w[:, 0:1] * buf_ref[slot, 0]
    for k in range(1, TOP_K):
        y = y + w[:, k:k + 1] * buf_ref[slot, k]
    o_ref[...] = _rms(x_ref[...] + g_ref[0] * y) * fn_ref[...]


def _combine(x, g, final_norm, wts, ys, dest, tm):
    b, s, d = x.shape
    tiles_per_batch = s // tm
    n_steps = b * tiles_per_batch
    return pl.pallas_call(
        functools.partial(_combine_kernel, tm=tm),
        grid=(n_steps,),
        in_specs=[
            pl.BlockSpec((TOP_K * tm,), lambda i: (i,), memory_space=pltpu.SMEM),
            pl.BlockSpec((TOP_K * tm,), lambda i: (jnp.minimum(i + 1, n_steps - 1),), memory_space=pltpu.SMEM),
            pl.BlockSpec((tm, d), lambda i: (i, 0)),
            pl.BlockSpec((1, 1, d), lambda i: (i // tiles_per_batch, 0, 0)),
            pl.BlockSpec((1, d), lambda i: (0, 0)),
            pl.BlockSpec((tm, LANES), lambda i: (i, 0)),
            pl.BlockSpec(memory_space=pl.ANY),
        ],
        out_specs=pl.BlockSpec((tm, d), lambda i: (i, 0)),
        out_shape=jax.ShapeDtypeStruct((b * s, d), F32),
        scratch_shapes=[pltpu.VMEM((2, TOP_K, tm, d), F32), pltpu.SemaphoreType.DMA((2,))],
        compiler_params=_params("arbitrary"),
        name="moe_combine",
    )(dest, dest, x.reshape(b * s, d), g, final_norm.reshape(1, d), wts.reshape(b * s, LANES), ys
      ).reshape(b, s, d)


def kernel(x, c, w_ada, b_ada, na_w_qkv, na_rpb, na_w_o, ffn_w_gu, ffn_w_down, mla_w_down, mla_q_norm,
           mla_w_uq, mla_kv_norm, mla_w_ukv, mla_w_o, moe_w_router, moe_w_gu, moe_w_down, final_norm):
    b, s, d = x.shape
    assert w_ada.shape[0] == 2, "one neighbourhood-attention layer followed by one latent-attention layer"
    t = _tiles(s)
    mods = _adaln(c, w_ada, b_ada)

    def split(layer):
        return [v.reshape(b, 1, d) for v in jnp.split(mods[layer], 6, axis=-1)]

    sh_a, sc_a, g_a, sh_f, sc_f, g_f = split(0)
    qkv = _ln_matmul(x, sh_a, sc_a, na_w_qkv[0].astype(BF16), t.wide)
    att = _neighborhood_attention(qkv, na_rpb[0])
    x = _proj_ffn(att, na_w_o[0].astype(BF16), x, g_a, sh_f, sc_f, g_f,
                  ffn_w_gu[0].astype(BF16), ffn_w_down[0].astype(BF16), t.token)

    sh_a, sc_a, g_a, sh_f, sc_f, g_f = split(1)
    q_t, k, v_t = _mla_proj(x, sh_a, sc_a, mla_w_down[0], mla_q_norm[0], mla_w_uq[0], mla_kv_norm[0],
                            mla_w_ukv[0], t.token)
    o_t = _mla_attention(q_t, k, v_t, t.mla_q, t.mla_k)
    x, h, idx, wts = _proj_router(o_t, mla_w_o[0].astype(BF16), x, g_a, sh_f, sc_f, moe_w_router[0], t.wide)
    n_exp = moe_w_router.shape[-1]
    dest, tile_expert, tile_live, p_rows = _route_plan(idx[..., :TOP_K].reshape(b * s, TOP_K), n_exp, t.token)
    xs = _row_scatter(h.reshape(b * s, d), dest, p_rows, t.wide)
    ys = _experts(xs, tile_expert, tile_live, moe_w_gu[0].astype(BF16), moe_w_down[0].astype(BF16), t.token)
    return _combine(x, g_f, final_norm, wts, ys, dest, t.wide)
```
